```python
import math
import jax, jax.numpy as jnp
from jax import lax
import numpy as np

D_MODEL = 2048
BATCH = 4
SEQ = 2048
DEPTH = 2

CHUNK = 64
QBLOCK = 128
N_MIXERS = 2
N_SB_LAYERS = (DEPTH + 1) // 2
N_ML_LAYERS = DEPTH // 2
EPS = 1e-6

SB_HEADS = 16
SB_HEAD_DIM = D_MODEL // SB_HEADS
SB_WIDTH = SB_HEADS * SB_HEAD_DIM

ML_HEADS = 8
ML_QK_DIM = (D_MODEL // 2) // ML_HEADS
ML_V_DIM = D_MODEL // ML_HEADS
ML_QK_WIDTH = ML_HEADS * ML_QK_DIM
ML_V_WIDTH = ML_HEADS * ML_V_DIM
ML_IN_WIDTH = 2 * ML_QK_WIDTH + 2 * ML_V_WIDTH + 2 * ML_HEADS

N_GROUPS = 4
EXPERTS_PER_GROUP = 8
N_EXPERTS = N_GROUPS * EXPERTS_PER_GROUP
TOP_K = 2
EXPERT_FF = D_MODEL // 8

kernel_name = "hybrid_stickbreaking_mlstm_hmoe"


def rmsnorm(x, g):
    xf = x.astype(jnp.float32)
    y = xf * lax.rsqrt(jnp.mean(xf * xf, axis=-1, keepdims=True) + EPS)
    return (y * g.astype(jnp.float32)).astype(x.dtype)


def stick_breaking_mixer(h, w_in, w_out):
    B, S, _ = h.shape
    qkv = h @ w_in
    q, k, v = jnp.split(qkv, 3, axis=-1)
    to_heads = lambda a: a.reshape(B, S, SB_HEADS, SB_HEAD_DIM).transpose(0, 2, 1, 3).astype(jnp.float32)
    q = to_heads(q) * (SB_HEAD_DIM ** -0.5)
    k, v = to_heads(k), to_heads(v)
    outs = []
    for blk in range(S // QBLOCK):
        t0 = blk * QBLOCK
        kn = t0 + QBLOCK
        qb, kb, vb = q[:, :, t0:kn], k[:, :, :kn], v[:, :, :kn]
        z = jnp.einsum('bhqd,bhkd->bhqk', qb, kb)
        t_idx = t0 + jnp.arange(QBLOCK)[:, None]
        s_idx = jnp.arange(kn)[None, :]
        mask = s_idx < t_idx
        log_1m = jnp.where(mask, jax.nn.log_sigmoid(-z), 0.0)
        suffix = lax.cumsum(log_1m, axis=3, reverse=True) - log_1m
        A = jnp.where(mask, jnp.exp(jax.nn.log_sigmoid(z) + suffix), 0.0)
        outs.append(jnp.einsum('bhqk,bhkd->bhqd', A, vb))
    o = jnp.concatenate(outs, axis=2)
    o = o.transpose(0, 2, 1, 3).reshape(B, S, SB_WIDTH).astype(h.dtype)
    return o @ w_out


def _mlstm_chunk_step(carry, inp):
    C, n, m = carry
    qc, kc, vc, ic, fc = inp
    L = qc.shape[2]
    b = jnp.cumsum(fc, axis=-1)
    bL = b[..., -1]
    causal = jnp.tril(jnp.ones((L, L), dtype=bool))
    D = b[..., :, None] - b[..., None, :] + ic[..., None, :]
    D = jnp.where(causal, D, -1e30)
    m_inter = b + m[..., None]
    m_t = jnp.maximum(m_inter, jnp.max(D, axis=-1))
    W = jnp.einsum('bhtd,bhsd->bhts', qc, kc) * jnp.where(causal, jnp.exp(D - m_t[..., None]), 0.0)
    inter = jnp.exp(m_inter - m_t)
    num = inter[..., None] * jnp.einsum('bhtd,bhde->bhte', qc, C) + jnp.einsum('bhts,bhse->bhte', W, vc)
    den = inter * jnp.einsum('bhtd,bhd->bht', qc, n) + jnp.sum(W, axis=-1)
    h = num / jnp.maximum(jnp.abs(den), jnp.exp(-m_t))[..., None]
    key_log = bL[..., None] - b + ic
    m_new = jnp.maximum(bL + m, jnp.max(key_log, axis=-1))
    decay = jnp.exp(key_log - m_new[..., None])
    carry_scale = jnp.exp(bL + m - m_new)
    C_new = carry_scale[..., None, None] * C + jnp.einsum('bhs,bhsd,bhse->bhde', decay, kc, vc)
    n_new = carry_scale[..., None] * n + jnp.einsum('bhs,bhsd->bhd', decay, kc)
    return (C_new, n_new, m_new), h


def mlstm_mixer(h, w_in, b_i, b_f, g_head, w_out):
    B, S, _ = h.shape
    NC = S // CHUNK
    proj = (h @ w_in).astype(jnp.float32)
    q, k, v, o, i_raw, f_raw = jnp.split(
        proj, np.cumsum([ML_QK_WIDTH, ML_QK_WIDTH, ML_V_WIDTH, ML_V_WIDTH, ML_HEADS]).tolist(), axis=-1)
    i_log = i_raw + b_i.astype(jnp.float32)
    f_log = jax.nn.log_sigmoid(f_raw + b_f.astype(jnp.float32))
    def chunked(a, d):
        return a.reshape(B, NC, CHUNK, ML_HEADS, d).transpose(1, 0, 3, 2, 4)
    qc = chunked(q, ML_QK_DIM) * (ML_QK_DIM ** -0.5)
    kc = chunked(k, ML_QK_DIM)
    vc = chunked(v, ML_V_DIM)
    ic = i_log.reshape(B, NC, CHUNK, ML_HEADS).transpose(1, 0, 3, 2)
    fc = f_log.reshape(B, NC, CHUNK, ML_HEADS).transpose(1, 0, 3, 2)
    init = (jnp.zeros((B, ML_HEADS, ML_QK_DIM, ML_V_DIM), jnp.float32),
            jnp.zeros((B, ML_HEADS, ML_QK_DIM), jnp.float32),
            jnp.zeros((B, ML_HEADS), jnp.float32))
    _, hs = lax.scan(_mlstm_chunk_step, init, (qc, kc, vc, ic, fc))
    hs = hs.transpose(1, 0, 3, 2, 4).reshape(B, S, ML_HEADS, ML_V_DIM)
    hs = hs * lax.rsqrt(jnp.mean(hs * hs, axis=-1, keepdims=True) + EPS)
    hs = hs.reshape(B, S, ML_V_WIDTH) * g_head.astype(jnp.float32)
    out = (jax.nn.sigmoid(o) * hs).astype(h.dtype)
    return out @ w_out


def hierarchical_moe(h, w_group, b_group, w_expert, b_expert, w1, w3, w2):
    B, S, D = h.shape
    t = h.reshape(B * S, D)
    g_logits = (t @ w_group + b_group).astype(jnp.float32)
    p_group = jax.nn.softmax(g_logits, axis=-1)
    g_sel = jnp.argmax(g_logits, axis=-1)
    p_sel = jnp.take_along_axis(p_group, g_sel[:, None], axis=-1)
    e_logits = (t @ w_expert + b_expert).astype(jnp.float32).reshape(-1, N_GROUPS, EXPERTS_PER_GROUP)
    e_sel = jnp.take_along_axis(e_logits, g_sel[:, None, None], axis=1)[:, 0]
    top_v, top_i = lax.top_k(e_sel, TOP_K)
    w = jax.nn.softmax(top_v, axis=-1) * p_sel
    e_idx = g_sel[:, None] * EXPERTS_PER_GROUP + top_i
    gates = jnp.sum(jax.nn.one_hot(e_idx, N_EXPERTS, dtype=jnp.float32) * w[..., None], axis=1)
    hid = jax.nn.silu(jnp.einsum('td,edf->tef', t, w1)) * jnp.einsum('td,edf->tef', t, w3)
    hid = hid * gates[..., None].astype(hid.dtype)
    out = jnp.einsum('tef,efd->td', hid, w2)
    return out.reshape(B, S, D)


def setup_inputs(seed: int = 0) -> dict:
    key = jax.random.key(seed)
    ks = jax.random.split(key, 20)
    f32 = jnp.float32
    nrm = lambda k, shape, scale: jax.random.normal(k, shape, f32) * scale
    x = nrm(ks[0], (BATCH, SEQ, D_MODEL), 1.0)
    norm_mix = 1.0 + nrm(ks[1], (DEPTH, D_MODEL), 0.01)
    norm_ffn = 1.0 + nrm(ks[2], (DEPTH, D_MODEL), 0.01)
    norm_final = 1.0 + nrm(ks[3], (D_MODEL,), 0.01)
    sb_w_in = nrm(ks[4], (N_SB_LAYERS, D_MODEL, 3 * SB_WIDTH), D_MODEL ** -0.5)
    sb_w_out = nrm(ks[5], (N_SB_LAYERS, SB_WIDTH, D_MODEL), SB_WIDTH ** -0.5)
    ml_w_in = nrm(ks[6], (N_ML_LAYERS, D_MODEL, ML_IN_WIDTH), D_MODEL ** -0.5)
    ml_b_i = nrm(ks[7], (N_ML_LAYERS, ML_HEADS), 0.1)
    ml_b_f = jnp.linspace(3.0, 6.0, ML_HEADS, dtype=f32)[None, :] + nrm(ks[8], (N_ML_LAYERS, ML_HEADS), 0.1)
    ml_g_head = 1.0 + nrm(ks[9], (N_ML_LAYERS, ML_V_WIDTH), 0.01)
    ml_w_out = nrm(ks[10], (N_ML_LAYERS, ML_V_WIDTH, D_MODEL), ML_V_WIDTH ** -0.5)
    moe_w_group = nrm(ks[11], (DEPTH, D_MODEL, N_GROUPS), D_MODEL ** -0.5)
    moe_b_group = nrm(ks[12], (DEPTH, N_GROUPS), 0.01)
    moe_w_expert = nrm(ks[13], (DEPTH, D_MODEL, N_EXPERTS), D_MODEL ** -0.5)
    moe_b_expert = nrm(ks[14], (DEPTH, N_EXPERTS), 0.01)
    moe_w1 = nrm(ks[15], (DEPTH, N_EXPERTS, D_MODEL, EXPERT_FF), D_MODEL ** -0.5)
    moe_w3 = nrm(ks[16], (DEPTH, N_EXPERTS, D_MODEL, EXPERT_FF), D_MODEL ** -0.5)
    moe_w2 = nrm(ks[17], (DEPTH, N_EXPERTS, EXPERT_FF, D_MODEL), EXPERT_FF ** -0.5)
    return {"x": x, "norm_mix": norm_mix, "norm_ffn": norm_ffn, "norm_final": norm_final,
            "sb_w_in": sb_w_in, "sb_w_out": sb_w_out,
            "ml_w_in": ml_w_in, "ml_b_i": ml_b_i, "ml_b_f": ml_b_f, "ml_g_head": ml_g_head,
            "ml_w_out": ml_w_out,
            "moe_w_group": moe_w_group, "moe_b_group": moe_b_group,
            "moe_w_expert": moe_w_expert, "moe_b_expert": moe_b_expert,
            "moe_w1": moe_w1, "moe_w3": moe_w3, "moe_w2": moe_w2}


def reference(x, norm_mix, norm_ffn, norm_final, sb_w_in, sb_w_out,
              ml_w_in, ml_b_i, ml_b_f, ml_g_head, ml_w_out,
              moe_w_group, moe_b_group, moe_w_expert, moe_b_expert, moe_w1, moe_w3, moe_w2):
    for i in range(DEPTH):
        h = rmsnorm(x, norm_mix[i])
        j = i // N_MIXERS
        if i % N_MIXERS == 0:
            x = x + stick_breaking_mixer(h, sb_w_in[j], sb_w_out[j])
        else:
            x = x + mlstm_mixer(h, ml_w_in[j], ml_b_i[j], ml_b_f[j], ml_g_head[j], ml_w_out[j])
        h = rmsnorm(x, norm_ffn[i])
        x = x + hierarchical_moe(h, moe_w_group[i], moe_b_group[i], moe_w_expert[i], moe_b_expert[i],
                                 moe_w1[i], moe_w3[i], moe_w2[i])
    return rmsnorm(x, norm_final)
```

```python
import functools

import jax
import jax.numpy as jnp
from jax import lax
from jax.experimental import pallas as pl
from jax.experimental.pallas import tpu as pltpu

F32 = jnp.float32
BF16 = jnp.bfloat16
EPS = 1e-6

V7X_VMEM_BYTES = 64 * 1024 * 1024
VMEM_LIMIT_BYTES = V7X_VMEM_BYTES - 8 * 1024 * 1024
LANES = 128
SUBLANES = 8

SB_HEADS = 16
ML_HEADS = 8
N_GROUPS = 4
EXPERTS_PER_GROUP = 8
N_EXPERTS = N_GROUPS * EXPERTS_PER_GROUP
TOP_K = 2
ML_CHUNK = 128
SB_BLOCK = 128
ROUTER_ROWS = 128

NT_DIMS = (((1,), (1,)), ((), ()))


def _params(*sem):
    return pltpu.CompilerParams(dimension_semantics=sem, vmem_limit_bytes=VMEM_LIMIT_BYTES)


def _split2(x):
    hi = x.astype(BF16)
    lo = (x - hi.astype(F32)).astype(BF16)
    return hi, lo


def _split3(x):
    hi = x.astype(BF16)
    r = x - hi.astype(F32)
    mid = r.astype(BF16)
    lo = (r - mid.astype(F32)).astype(BF16)
    return hi, mid, lo


def _rms(x, g):
    return x * lax.rsqrt(jnp.mean(x * x, axis=-1, keepdims=True) + EPS) * g


def _rmsnorm_body(x_ref, g_ref, h_ref):
    h_ref[...] = _rms(x_ref[...], g_ref[...]).astype(h_ref.dtype)


def rmsnorm_bf16(x, g, *, tm=512):
    t, d = x.shape
    return pl.pallas_call(
        _rmsnorm_body,
        grid=(t // tm,),
        in_specs=[pl.BlockSpec((tm, d), lambda i: (i, 0)), pl.BlockSpec((1, d), lambda i: (0, 0))],
        out_specs=pl.BlockSpec((tm, d), lambda i: (i, 0)),
        out_shape=jax.ShapeDtypeStruct((t, d), BF16),
        compiler_params=_params("parallel"),
        name="rmsnorm",
    )(x, g.reshape(1, d))


def _mm_body(h_ref, w_ref, o_ref):
    o_ref[...] = jnp.dot(h_ref[...], w_ref[...], preferred_element_type=F32).astype(o_ref.dtype)


def matmul_bf16(h, w, *, tm=1024, tn=1024):
    t, k = h.shape
    n = w.shape[1]
    return pl.pallas_call(
        _mm_body,
        grid=(n // tn, t // tm),
        in_specs=[pl.BlockSpec((tm, k), lambda j, i: (i, 0)), pl.BlockSpec((k, tn), lambda j, i: (0, j))],
        out_specs=pl.BlockSpec((tm, tn), lambda j, i: (i, j)),
        out_shape=jax.ShapeDtypeStruct((t, n), BF16),
        compiler_params=_params("parallel", "parallel"),
        name="proj_in",
    )(h, w)


def _sb_attn_body(q_ref, k_ref, v_ref, cm_ref, o_ref, *, scale, nblk):
    blk = SB_BLOCK
    row = lax.broadcasted_iota(jnp.int32, (blk, blk), 0)
    col = lax.broadcasted_iota(jnp.int32, (blk, blk), 1)
    causal = col < row

    def tile(qb, k0, r_carry, acc, masked):
        kt = k_ref[pl.ds(k0, blk), :]
        vt = v_ref[pl.ds(k0, blk), :]
        z = lax.dot_general(qb, kt, NT_DIMS, preferred_element_type=F32)
        sp = jnp.maximum(z, 0.0) + jnp.log(1.0 + jnp.exp(-jnp.abs(z)))
        log_beta = z - sp
        if masked:
            sp = jnp.where(causal, sp, 0.0)
        hi, lo = _split2(sp)
        cs = jnp.dot(jnp.concatenate([hi, lo], axis=1), cm_ref[...], preferred_element_type=F32)
        a = jnp.exp(log_beta + cs[:, :blk] + r_carry)
        if masked:
            a = jnp.where(causal, a, 0.0)
        acc = acc + jnp.dot(a.astype(BF16), vt, preferred_element_type=F32)
        return r_carry + cs[:, blk:], acc

    def q_step(qi, carry):
        q0 = pl.multiple_of(qi * blk, blk)
        qb = (q_ref[pl.ds(q0, blk), :].astype(F32) * scale).astype(BF16)
        zeros = jnp.zeros((blk, blk), F32)
        r_carry, acc = tile(qb, q0, zeros, zeros, True)

        def k_step(j, c):
            k0 = pl.multiple_of((qi - 1 - j) * blk, blk)
            return tile(qb, k0, c[0], c[1], False)

        _, acc = lax.fori_loop(0, qi, k_step, (r_carry, acc))
        o_ref[pl.ds(q0, blk), :] = acc.astype(o_ref.dtype)
        return carry

    lax.fori_loop(0, nblk, q_step, 0)


def _cumsum_matrix(blk):
    j = jnp.arange(2 * blk)[:, None] % blk
    c = jnp.arange(2 * blk)[None, :]
    return jnp.where((c >= blk) | (j > c), -1.0, 0.0).astype(BF16)


def sb_attention(qkv, *, batch, seq, heads, dh):
    assert dh == SB_BLOCK
    nblk = seq // SB_BLOCK
    body = functools.partial(_sb_attn_body, scale=dh ** -0.5, nblk=nblk)
    cm = _cumsum_matrix(SB_BLOCK)
    return pl.pallas_call(
        body,
        grid=(batch, heads),
        in_specs=[
            pl.BlockSpec((seq, dh), lambda b, h: (b, h)),
            pl.BlockSpec((seq, dh), lambda b, h: (b, heads + h)),
            pl.BlockSpec((seq, dh), lambda b, h: (b, 2 * heads + h)),
            pl.BlockSpec((2 * SB_BLOCK, 2 * SB_BLOCK), lambda b, h: (0, 0)),
        ],
        out_specs=pl.BlockSpec((seq, dh), lambda b, h: (b, h)),
        out_shape=jax.ShapeDtypeStruct((batch * seq, heads * dh), BF16),
        compiler_params=_params("parallel", "parallel"),
        name="sb_attn",
    )(qkv, qkv, qkv, cm)


def _proj_route_body(o_ref, w_ref, x_ref, g_ref, wrh_ref, wrl_ref, br_ref, tri_ref,
                     x1_ref, h2_ref, ri_ref, rw_ref, cnt_ref, base_ref, *, tm):
    i = pl.program_id(0)

    @pl.when(i == 0)
    def _():
        base_ref[...] = jnp.zeros_like(base_ref)

    x1 = x_ref[...] + jnp.dot(o_ref[...], w_ref[...], preferred_element_type=F32)
    x1_ref[...] = x1
    h2 = _rms(x1, g_ref[...])
    h2_ref[...] = h2

    hh, hl = _split2(h2)
    wrh = wrh_ref[...]
    lg = (lax.dot_general(wrh, hh, NT_DIMS, preferred_element_type=F32)
          + lax.dot_general(wrh, hl, NT_DIMS, preferred_element_type=F32)
          + lax.dot_general(wrl_ref[...], hh, NT_DIMS, preferred_element_type=F32)) + br_ref[...]

    r8 = lax.broadcasted_iota(jnp.int32, (SUBLANES, tm), 0)
    neg_inf = jnp.float32(-jnp.inf)
    gl = jnp.where(r8 < N_GROUPS, lg[N_EXPERTS:N_EXPERTS + SUBLANES, :], neg_inf)
    gmax = jnp.max(gl, axis=0, keepdims=True)
    gsel = jnp.min(jnp.where(gl == gmax, r8, SUBLANES), axis=0, keepdims=True)
    p_sel = 1.0 / jnp.sum(jnp.exp(gl - gmax), axis=0, keepdims=True)

    es = lg[0:EXPERTS_PER_GROUP, :]
    for g in range(1, N_GROUPS):
        es = jnp.where(gsel == g, lg[g * EXPERTS_PER_GROUP:(g + 1) * EXPERTS_PER_GROUP, :], es)
    m1 = jnp.max(es, axis=0, keepdims=True)
    i1 = jnp.min(jnp.where(es == m1, r8, SUBLANES), axis=0, keepdims=True)
    es2 = jnp.where(r8 == i1, neg_inf, es)
    m2 = jnp.max(es2, axis=0, keepdims=True)
    i2 = jnp.min(jnp.where(es2 == m2, r8, SUBLANES), axis=0, keepdims=True)
    t = jnp.exp(m2 - m1)
    w1 = p_sel / (1.0 + t)
    w2 = w1 * t
    e1 = gsel * EXPERTS_PER_GROUP + i1
    e2 = gsel * EXPERTS_PER_GROUP + i2

    r32 = lax.broadcasted_iota(jnp.int32, (N_EXPERTS, tm), 0)
    hit1 = r32 == e1
    hit2 = r32 == e2
    onehot = (jnp.where(hit1, 1.0, 0.0) + jnp.where(hit2, 1.0, 0.0)).astype(BF16)
    cs = jnp.dot(onehot, tri_ref[...], preferred_element_type=F32)
    base = base_ref[...]
    before = cs[:, :tm] + base
    rank1 = jnp.sum(jnp.where(hit1, before, 0.0), axis=0, keepdims=True)
    rank2 = jnp.sum(jnp.where(hit2, before, 0.0), axis=0, keepdims=True)
    tot = cs[:, tm:]
    new_base = base + jnp.concatenate([tot] * (tm // LANES), axis=1)
    base_ref[...] = new_base
    cnt_ref[...] = new_base[:, :LANES]

    zi = jnp.zeros((SUBLANES, tm), jnp.int32)
    ri = jnp.where(r8 == 0, e1, zi)
    ri = jnp.where(r8 == 1, e2, ri)
    ri = jnp.where(r8 == 2, rank1.astype(jnp.int32), ri)
    ri = jnp.where(r8 == 3, rank2.astype(jnp.int32), ri)
    ri_ref[...] = ri
    zf = jnp.zeros((SUBLANES, tm), F32)
    rw = jnp.where(r8 == 0, w1, zf)
    rw = jnp.where(r8 == 1, w2, rw)
    rw_ref[...] = rw


def proj_route(o, w_out, x, g, w_group, b_group, w_expert, b_expert, *, tm=256):
    t, d = x.shape
    k = o.shape[1]
    wr = jnp.zeros((ROUTER_ROWS, d), F32)
    wr = wr.at[:N_EXPERTS].set(w_expert.T).at[N_EXPERTS:N_EXPERTS + N_GROUPS].set(w_group.T)
    wrh, wrl = _split2(wr)
    br = jnp.zeros((ROUTER_ROWS,), F32).at[:N_EXPERTS].set(b_expert).at[N_EXPERTS:N_EXPERTS + N_GROUPS].set(b_group)
    br = jnp.broadcast_to(br[:, None], (ROUTER_ROWS, tm))
    tp = jnp.arange(tm)[:, None]
    tri = jnp.concatenate([(tp < jnp.arange(tm)[None, :]).astype(BF16), jnp.ones((tm, LANES), BF16)], axis=1)
    body = functools.partial(_proj_route_body, tm=tm)
    row = lambda i: (i, 0)
    const = lambda i: (0, 0)
    return pl.pallas_call(
        body,
        grid=(t // tm,),
        in_specs=[
            pl.BlockSpec((tm, k), row),
            pl.BlockSpec((k, d), const),
            pl.BlockSpec((tm, d), row),
            pl.BlockSpec((1, d), const),
            pl.BlockSpec((ROUTER_ROWS, d), const),
            pl.BlockSpec((ROUTER_ROWS, d), const),
            pl.BlockSpec((ROUTER_ROWS, tm), const),
            pl.BlockSpec((tm, tm + LANES), const),
        ],
        out_specs=[
            pl.BlockSpec((tm, d), row),
            pl.BlockSpec((tm, d), row),
            pl.BlockSpec((SUBLANES, tm), lambda i: (0, i)),
            pl.BlockSpec((SUBLANES, tm), lambda i: (0, i)),
            pl.BlockSpec((N_EXPERTS, LANES), const),
        ],
        out_shape=[
            jax.ShapeDtypeStruct((t, d), F32),
            jax.ShapeDtypeStruct((t, d), F32),
            jax.ShapeDtypeStruct((SUBLANES, t), jnp.int32),
            jax.ShapeDtypeStruct((SUBLANES, t), F32),
            jax.ShapeDtypeStruct((N_EXPERTS, LANES), F32),
        ],
        scratch_shapes=[pltpu.VMEM((N_EXPERTS, tm), F32)],
        compiler_params=_params("arbitrary"),
        name="proj_route",
    )(o, w_out, x, g.reshape(1, d), wrh, wrl, br, tri)


def _experts_body(te_ref, nv_ref, nt_ref, src_ref, nxt_ref, h_hbm, w1_ref, w3_ref, w2_ref, y_hbm,
                  xbuf, ybuf, w1b, w3b, w2b, gsem, ssem, *, tm, n_tok):
    i = pl.program_id(0)
    n_tiles = nt_ref[0]
    slot = lax.rem(i, 2)

    def row_in(tok, r, s):
        return pltpu.make_async_copy(h_hbm.at[pl.ds(tok, 1)], xbuf.at[s, pl.ds(r, 1)], gsem.at[s])

    def row_out(dst, r, s):
        return pltpu.make_async_copy(ybuf.at[s, pl.ds(r, 1)], y_hbm.at[pl.ds(dst, 1)], ssem.at[s])

    def gather_start(idx_ref, s):
        def go(r, c):
            p = idx_ref[0, 0, r]
            row_in(jnp.where(p < 0, 0, p & (n_tok - 1)), r, s).start()
            return c
        lax.fori_loop(0, tm, go, 0)

    def gather_wait(s):
        def go(r, c):
            row_in(0, r, s).wait()
            return c
        lax.fori_loop(0, tm, go, 0)

    def scatter_wait(n, s):
        def go(r, c):
            row_out(0, r, s).wait()
            return c
        lax.fori_loop(0, n, go, 0)

    @pl.when(i == 0)
    def _():
        gather_start(src_ref, 0)

    @pl.when(i < n_tiles)
    def _():
        gather_wait(slot)

        @pl.when(i + 1 < n_tiles)
        def _():
            gather_start(nxt_ref, 1 - slot)

        @pl.when((i == 0) | (te_ref[i] != te_ref[jnp.maximum(i - 1, 0)]))
        def _():
            w1b[...] = w1_ref[0].astype(BF16)
            w3b[...] = w3_ref[0].astype(BF16)
            w2b[...] = w2_ref[0].astype(BF16)

        @pl.when(i >= 2)
        def _():
            scatter_wait(nv_ref[jnp.maximum(i - 2, 0)], slot)

        xb = xbuf[slot].astype(BF16)
        a = jnp.dot(xb, w1b[...], preferred_element_type=F32)
        b = jnp.dot(xb, w3b[...], preferred_element_type=F32)
        hid = (a * (1.0 / (1.0 + jnp.exp(-a))) * b).astype(BF16)
        ybuf[slot] = jnp.dot(hid, w2b[...], preferred_element_type=F32)

        def put(r, c):
            row_out(src_ref[0, 0, r], r, slot).start()
            return c
        lax.fori_loop(0, nv_ref[i], put, 0)

        @pl.when(i == n_tiles - 1)
        def _():
            scatter_wait(nv_ref[i], slot)

            @pl.when(i >= 1)
            def _():
                scatter_wait(nv_ref[jnp.maximum(i - 1, 0)], 1 - slot)


def routed_experts(h2, ri, cnt, w1, w3, w2, *, tm=256):
    t, d = h2.shape
    f = w1.shape[2]
    nt_max = TOP_K * t // tm + N_EXPERTS
    counts = cnt[:, 0].astype(jnp.int32)
    padded = ((counts + tm - 1) // tm) * tm
    ends = jnp.cumsum(padded)
    offs = ends - padded
    pos = offs[ri[0:TOP_K]] + ri[TOP_K:2 * TOP_K]
    pair = jnp.arange(TOP_K * t, dtype=jnp.int32)
    src = jnp.full((nt_max * tm,), -1, jnp.int32).at[pos.reshape(-1)].set(pair)
    n_tiles = (ends[-1] // tm).astype(jnp.int32)
    tile_start = jnp.arange(nt_max, dtype=jnp.int32) * tm
    te = jnp.searchsorted(ends, tile_start, side="right").astype(jnp.int32)
    te = jnp.minimum(te, te[jnp.maximum(n_tiles - 1, 0)])
    te = jnp.clip(te, 0, N_EXPERTS - 1)
    nv = jnp.clip(counts[te] - (tile_start - offs[te]), 0, tm).astype(jnp.int32)
    nv = jnp.where(jnp.arange(nt_max) < n_tiles, nv, 0)
    src3 = src.reshape(nt_max, 1, tm)

    body = functools.partial(_experts_body, tm=tm, n_tok=t)
    smem_tile = lambda imap: pl.BlockSpec((1, 1, tm), imap, memory_space=pltpu.SMEM)
    grid_spec = pltpu.PrefetchScalarGridSpec(
        num_scalar_prefetch=3,
        grid=(nt_max,),
        in_specs=[
            smem_tile(lambda i, te, nv, nt: (i, 0, 0)),
            smem_tile(lambda i, te, nv, nt: (jnp.minimum(i + 1, nt_max - 1), 0, 0)),
            pl.BlockSpec(memory_space=pl.ANY),
            pl.BlockSpec((1, d, f), lambda i, te, nv, nt: (te[i], 0, 0)),
            pl.BlockSpec((1, d, f), lambda i, te, nv, nt: (te[i], 0, 0)),
            pl.BlockSpec((1, f, d), lambda i, te, nv, nt: (te[i], 0, 0)),
        ],
        out_specs=pl.BlockSpec(memory_space=pl.ANY),
        scratch_shapes=[
            pltpu.VMEM((2, tm, d), F32),
            pltpu.VMEM((2, tm, d), F32),
            pltpu.VMEM((d, f), BF16),
            pltpu.VMEM((d, f), BF16),
            pltpu.VMEM((f, d), BF16),
            pltpu.SemaphoreType.DMA((2,)),
            pltpu.SemaphoreType.DMA((2,)),
        ],
    )
    return pl.pallas_call(
        body,
        grid_spec=grid_spec,
        out_shape=jax.ShapeDtypeStruct((TOP_K * t, d), F32),
        compiler_params=_params("arbitrary"),
        name="experts",
    )(te, nv, n_tiles.reshape(1), src3, src3, h2, w1, w3, w2)


def _combine_body(x_ref, y0_ref, y1_ref, w_ref, g_ref, x2_ref, h_ref):
    w = w_ref[...]
    x2 = x_ref[...] + w[:, 0:1] * y0_ref[...] + w[:, 1:2] * y1_ref[...]
    if x2_ref is not None:
        x2_ref[...] = x2
    h_ref[...] = _rms(x2, g_ref[...]).astype(h_ref.dtype)


def _combine_final_body(x_ref, y0_ref, y1_ref, w_ref, g_ref, h_ref):
    _combine_body(x_ref, y0_ref, y1_ref, w_ref, g_ref, None, h_ref)


def moe_combine(x1, y, rw, g, *, final, tm=256):
    t, d = x1.shape
    y3 = y.reshape(TOP_K, t, d)
    w = rw[0:TOP_K].T
    row = lambda i: (i, 0)
    in_specs = [
        pl.BlockSpec((tm, d), row),
        pl.BlockSpec((None, tm, d), lambda i: (0, i, 0)),
        pl.BlockSpec((None, tm, d), lambda i: (1, i, 0)),
        pl.BlockSpec((tm, TOP_K), row),
        pl.BlockSpec((1, d), lambda i: (0, 0)),
    ]
    args = (x1, y3, y3, w, g.reshape(1, d))
    if final:
        return pl.pallas_call(
            _combine_final_body, grid=(t // tm,), in_specs=in_specs,
            out_specs=pl.BlockSpec((tm, d), row),
            out_shape=jax.ShapeDtypeStruct((t, d), F32),
            compiler_params=_params("parallel"), name="combine_final",
        )(*args)
    return pl.pallas_call(
        _combine_body, grid=(t // tm,), in_specs=in_specs,
        out_specs=[pl.BlockSpec((tm, d), row), pl.BlockSpec((tm, d), row)],
        out_shape=[jax.ShapeDtypeStruct((t, d), F32), jax.ShapeDtypeStruct((t, d), BF16)],
        compiler_params=_params("parallel"), name="combine",
    )(*args)


def _log_sigmoid(x):
    return jnp.minimum(x, 0.0) - jnp.log(1.0 + jnp.exp(-jnp.abs(x)))


def _gates_body(h_ref, wch_ref, wcl_ref, wrh_ref, wrl_ref, bc_ref, br_ref, gc_ref, gr_ref, *, nh):
    h = h_ref[...]
    col = (jnp.dot(h, wch_ref[...], preferred_element_type=F32)
           + jnp.dot(h, wcl_ref[...], preferred_element_type=F32)) + bc_ref[...]
    lane = lax.broadcasted_iota(jnp.int32, col.shape, 1)
    gc_ref[...] = jnp.where(lane < nh, col, _log_sigmoid(col))
    rw = (lax.dot_general(wrh_ref[...], h, NT_DIMS, preferred_element_type=F32)
          + lax.dot_general(wrl_ref[...], h, NT_DIMS, preferred_element_type=F32)) + br_ref[...]
    sub = lax.broadcasted_iota(jnp.int32, rw.shape, 0)
    gr_ref[...] = jnp.where(sub < nh, rw, _log_sigmoid(rw))


def mlstm_gates(h, w_if, b_i, b_f, *, tm=512):
    t, d = h.shape
    nh = b_i.shape[0]
    wc = jnp.zeros((d, LANES), F32).at[:, :2 * nh].set(w_if)
    wch, wcl = _split2(wc)
    wrh, wrl = _split2(w_if.T)
    bias = jnp.concatenate([b_i, b_f])
    bc = jnp.zeros((1, LANES), F32).at[0, :2 * nh].set(bias)
    br = jnp.broadcast_to(bias[:, None], (2 * nh, tm))
    const = lambda i: (0, 0)
    return pl.pallas_call(
        functools.partial(_gates_body, nh=nh),
        grid=(t // tm,),
        in_specs=[
            pl.BlockSpec((tm, d), lambda i: (i, 0)),
            pl.BlockSpec((d, LANES), const), pl.BlockSpec((d, LANES), const),
            pl.BlockSpec((2 * nh, d), const), pl.BlockSpec((2 * nh, d), const),
            pl.BlockSpec((1, LANES), const), pl.BlockSpec((2 * nh, tm), const),
        ],
        out_specs=[pl.BlockSpec((tm, LANES), lambda i: (i, 0)), pl.BlockSpec((2 * nh, tm), lambda i: (0, i))],
        out_shape=[jax.ShapeDtypeStruct((t, LANES), F32), jax.ShapeDtypeStruct((2 * nh, t), F32)],
        compiler_params=_params("parallel"),
        name="mlstm_gates",
    )(h, wch, wcl, wrh, wrl, bc, br)


def _mlstm_body(q_ref, k_ref, v_ref, o_ref, gc_ref, gr_ref, gh_ref, tril_ref, triu_ref, out_ref,
                c_ref, n_ref, m_ref, *, scale, nchunk, dv):
    L = ML_CHUNK
    row = lax.broadcasted_iota(jnp.int32, (L, L), 0)
    col = lax.broadcasted_iota(jnp.int32, (L, L), 1)
    tmask = col <= row
    c_ref[...] = jnp.zeros_like(c_ref)
    n_ref[...] = jnp.zeros_like(n_ref)
    m_ref[...] = jnp.zeros_like(m_ref)

    def cumsum_exact(parts, tri, left):
        acc = None
        for p in parts:
            d = (jnp.dot(tri, p, preferred_element_type=F32) if left
                 else jnp.dot(p, tri, preferred_element_type=F32))
            acc = d if acc is None else acc + d
        return acc

    def step(c, carry):
        c0 = pl.multiple_of(c * L, L)
        qc = (q_ref[pl.ds(c0, L), :].astype(F32) * scale).astype(BF16)
        kc = k_ref[pl.ds(c0, L), :]
        vc = v_ref[pl.ds(c0, L), :]
        gc = gc_ref[0, pl.ds(c0, L), :]
        gr = gr_ref[0, :, pl.ds(c0, L)]
        i_col, f_col = gc[:, 0:1], gc[:, 1:2]
        i_row, f_row = gr[0:1, :], gr[1:2, :]
        m_prev = m_ref[...]

        b_col = cumsum_exact(_split3(jnp.broadcast_to(f_col, (L, L))), tril_ref[...], True)
        b_row = cumsum_exact(_split3(jnp.broadcast_to(f_row, (SUBLANES, L))), triu_ref[...], False)[0:1, :]
        bt = b_col[:, 0:1]
        b_last = b_col[L - 1:L, 0:1]

        d_mat = jnp.where(tmask, b_col + (i_row - b_row), -1e30)
        m_t = jnp.maximum(bt + m_prev, jnp.max(d_mat, axis=1, keepdims=True))
        e = jnp.where(tmask, jnp.exp(d_mat - m_t), 0.0)
        w = lax.dot_general(qc, kc, NT_DIMS, preferred_element_type=F32) * e
        inter = jnp.exp(bt + m_prev - m_t)
        c_prev = c_ref[...]
        n_prev = n_ref[...]
        num = inter * jnp.dot(qc, c_prev.astype(BF16), preferred_element_type=F32) \
            + jnp.dot(w.astype(BF16), vc, preferred_element_type=F32)
        qn = jnp.sum(qc.astype(F32) * n_prev, axis=1, keepdims=True)
        den = inter * qn + jnp.sum(w, axis=1, keepdims=True)
        h = num / jnp.maximum(jnp.abs(den), jnp.exp(-m_t))

        hs = h * lax.rsqrt(jnp.mean(h * h, axis=1, keepdims=True) + EPS) * gh_ref[...]
        og = o_ref[pl.ds(c0, L), :].astype(F32)
        out_ref[pl.ds(c0, L), :] = (hs / (1.0 + jnp.exp(-og))).astype(out_ref.dtype)

        key_log = b_last - bt + i_col
        m_new = jnp.maximum(b_last + m_prev, jnp.max(key_log, axis=0, keepdims=True))
        decay = jnp.exp(key_log - m_new)
        keep = jnp.exp(b_last + m_prev - m_new)
        kd = kc.astype(F32) * decay
        c_ref[...] = keep * c_prev + jnp.dot(kd.T.astype(BF16), vc, preferred_element_type=F32)
        n_ref[...] = keep * n_prev + jnp.sum(kd, axis=0, keepdims=True)
        m_ref[...] = m_new
        return carry

    lax.fori_loop(0, nchunk, step, 0)


def mlstm(qkvo, gcol, grow, g_head, *, batch, seq, heads, dqk, dv):
    t = batch * seq
    nh = heads
    gc3 = jnp.stack([gcol[:, :nh].T, gcol[:, nh:2 * nh].T], axis=-1)
    gr3 = jnp.stack([grow[:nh], grow[nh:2 * nh]], axis=1)
    L = ML_CHUNK
    idx = jnp.arange(L)
    tril = (idx[None, :] <= idx[:, None]).astype(BF16)
    triu = (idx[:, None] <= idx[None, :]).astype(BF16)
    kq = nh * dqk // dqk
    del kq
    body = functools.partial(_mlstm_body, scale=dqk ** -0.5, nchunk=seq // L, dv=dv)
    v_blk0 = 2 * nh * dqk // dv
    o_blk0 = v_blk0 + nh
    const = lambda b, h: (0, 0)
    return pl.pallas_call(
        body,
        grid=(batch, nh),
        in_specs=[
            pl.BlockSpec((seq, dqk), lambda b, h: (b, h)),
            pl.BlockSpec((seq, dqk), lambda b, h: (b, nh + h)),
            pl.BlockSpec((seq, dv), lambda b, h: (b, v_blk0 + h)),
            pl.BlockSpec((seq, dv), lambda b, h: (b, o_blk0 + h)),
            pl.BlockSpec((1, seq, 2), lambda b, h: (h, b, 0)),
            pl.BlockSpec((1, 2, seq), lambda b, h: (h, 0, b)),
            pl.BlockSpec((1, dv), lambda b, h: (0, h)),
            pl.BlockSpec((L, L), const),
            pl.BlockSpec((L, L), const),
        ],
        out_specs=pl.BlockSpec((seq, dv), lambda b, h: (b, h)),
        out_shape=jax.ShapeDtypeStruct((t, nh * dv), BF16),
        scratch_shapes=[pltpu.VMEM((dqk, dv), F32), pltpu.VMEM((1, dqk), F32), pltpu.VMEM((1, 1), F32)],
        compiler_params=_params("parallel", "parallel"),
        name="mlstm",
    )(qkvo, qkvo, qkvo, qkvo, gc3, gr3, g_head.reshape(1, nh * dv), tril, triu)


def _moe(o, w_out, x, norm_ffn, w_group, b_group, w_expert, b_expert, w1, w3, w2, g_next, *, final):
    x1, h2, ri, rw, cnt = proj_route(o, w_out.astype(BF16), x, norm_ffn, w_group, b_group, w_expert, b_expert)
    y = routed_experts(h2, ri, cnt, w1, w3, w2)
    return moe_combine(x1, y, rw, g_next, final=final)


def kernel(x, norm_mix, norm_ffn, norm_final, sb_w_in, sb_w_out, ml_w_in, ml_b_i, ml_b_f, ml_g_head, ml_w_out,
           moe_w_group, moe_b_group, moe_w_expert, moe_b_expert, moe_w1, moe_w3, moe_w2):
    batch, seq, d = x.shape
    xt = x.reshape(batch * seq, d)

    h = rmsnorm_bf16(xt, norm_mix[0])
    qkv = matmul_bf16(h, sb_w_in[0].astype(BF16))
    o = sb_attention(qkv, batch=batch, seq=seq, heads=SB_HEADS, dh=d // SB_HEADS)
    xt, h = _moe(o, sb_w_out[0], xt, norm_ffn[0], moe_w_group[0], moe_b_group[0], moe_w_expert[0],
                 moe_b_expert[0], moe_w1[0], moe_w3[0], moe_w2[0], norm_mix[1], final=False)

    dqk = (d // 2) // ML_HEADS
    dv = d // ML_HEADS
    n_main = 2 * ML_HEADS * dqk + 2 * ML_HEADS * dv
    qkvo = matmul_bf16(h, ml_w_in[0][:, :n_main].astype(BF16))
    gcol, grow = mlstm_gates(h, ml_w_in[0][:, n_main:], ml_b_i[0], ml_b_f[0])
    o = mlstm(qkvo, gcol, grow, ml_g_head[0], batch=batch, seq=seq, heads=ML_HEADS, dqk=dqk, dv=dv)
    out = _moe(o, ml_w_out[0], xt, norm_ffn[1], moe_w_group[1], moe_b_group[1], moe_w_expert[1],
               moe_b_expert[1], moe_w1[1], moe_w3[1], moe_w2[1], norm_final, final=True)
    return out.reshape(batch, seq, d)
```

```python
import functools

import jax
import jax.numpy as jnp
from jax import lax
from jax.experimental import pallas as pl
from jax.experimental.pallas import tpu as pltpu

F32 = jnp.float32
BF16 = jnp.bfloat16
EPS = 1e-6

V7X_VMEM_BYTES = 64 * 1024 * 1024
VMEM_LIMIT_BYTES = V7X_VMEM_BYTES - 8 * 1024 * 1024
LANES = 128
SUBLANES = 8

SB_HEADS = 16
ML_HEADS = 8
N_GROUPS = 4
EXPERTS_PER_GROUP = 8
N_EXPERTS = N_GROUPS * EXPERTS_PER_GROUP
TOP_K = 2
ML_CHUNK = 128
SB_BLOCK = 128
ROUTER_ROWS = 128

NT_DIMS = (((1,), (1,)), ((), ()))


def _params(*sem):
    return pltpu.CompilerParams(dimension_semantics=sem, vmem_limit_bytes=VMEM_LIMIT_BYTES)


def _split2(x):
    hi = x.astype(BF16)
    lo = (x - hi.astype(F32)).astype(BF16)
    return hi, lo


def _split3(x):
    hi = x.astype(BF16)
    r = x - hi.astype(F32)
    mid = r.astype(BF16)
    lo = (r - mid.astype(F32)).astype(BF16)
    return hi, mid, lo


def _rms(x, g):
    return x * lax.rsqrt(jnp.mean(x * x, axis=-1, keepdims=True) + EPS) * g


def _rmsnorm_body(x_ref, g_ref, h_ref):
    h_ref[...] = _rms(x_ref[...], g_ref[...]).astype(h_ref.dtype)


def rmsnorm_bf16(x, g, *, tm=512):
    t, d = x.shape
    return pl.pallas_call(
        _rmsnorm_body,
        grid=(t // tm,),
        in_specs=[pl.BlockSpec((tm, d), lambda i: (i, 0)), pl.BlockSpec((1, d), lambda i: (0, 0))],
        out_specs=pl.BlockSpec((tm, d), lambda i: (i, 0)),
        out_shape=jax.ShapeDtypeStruct((t, d), BF16),
        compiler_params=_params("parallel"),
        name="rmsnorm",
    )(x, g.reshape(1, d))


def _mm_body(h_ref, w_ref, o_ref, wb_ref):
    @pl.when(pl.program_id(1) == 0)
    def _():
        wb_ref[...] = w_ref[...].astype(BF16)

    o_ref[...] = jnp.dot(h_ref[...], wb_ref[...], preferred_element_type=F32).astype(o_ref.dtype)


def matmul_bf16(h, w, n, *, tm=1024, tn=1024):
    t, k = h.shape
    return pl.pallas_call(
        _mm_body,
        grid=(n // tn, t // tm),
        in_specs=[pl.BlockSpec((tm, k), lambda j, i: (i, 0)), pl.BlockSpec((None, k, tn), lambda j, i: (0, 0, j))],
        out_specs=pl.BlockSpec((tm, tn), lambda j, i: (i, j)),
        out_shape=jax.ShapeDtypeStruct((t, n), BF16),
        scratch_shapes=[pltpu.VMEM((k, tn), BF16)],
        compiler_params=_params("parallel", "arbitrary"),
        name="proj_in",
    )(h, w)


def _sb_attn_body(q_ref, k_ref, v_ref, cm_ref, o_ref, *, scale, nblk, hg, kc):
    blk = SB_BLOCK
    dh = SB_BLOCK
    wid = kc * blk
    rel = lax.broadcasted_iota(jnp.int32, (blk, wid), 1) - lax.broadcasted_iota(jnp.int32, (blk, wid), 0)

    def chunk(qbs, q0, c, rs, accs, masked):
        k0 = pl.multiple_of(c * wid, wid)
        if masked:
            valid = rel < (q0 - k0)
        z = jnp.concatenate(
            [lax.dot_general(qbs[h], k_ref[pl.ds(k0, wid), h * dh:(h + 1) * dh], NT_DIMS,
                             preferred_element_type=F32) for h in range(hg)], axis=1)
        sp = jnp.maximum(z, 0.0) + jnp.log(1.0 + jnp.exp(-jnp.abs(z)))
        log_beta = z - sp
        if masked:
            valid_all = jnp.concatenate([valid] * hg, axis=1)
            sp = jnp.where(valid_all, sp, 0.0)
        hi, lo = _split2(sp)
        nb = hg * kc
        x = jnp.concatenate(
            [jnp.concatenate([hi[:, j * blk:(j + 1) * blk], lo[:, j * blk:(j + 1) * blk]], axis=1)
             for j in range(nb)], axis=0)
        cs = jnp.dot(x, cm_ref[...], preferred_element_type=F32)
        new_rs = []
        suffix = [None] * nb
        for h in range(hg):
            r = rs[h]
            for j in reversed(range(kc)):
                csj = cs[(h * kc + j) * blk:(h * kc + j + 1) * blk]
                suffix[h * kc + j] = csj[:, :blk] + r
                r = r + csj[:, blk:]
            new_rs.append(r)
        a = jnp.exp(log_beta + jnp.concatenate(suffix, axis=1))
        if masked:
            a = jnp.where(valid_all, a, 0.0)
        a = a.astype(BF16)
        new_accs = [accs[h] + jnp.dot(a[:, h * wid:(h + 1) * wid], v_ref[pl.ds(k0, wid), h * dh:(h + 1) * dh],
                                      preferred_element_type=F32) for h in range(hg)]
        return tuple(new_rs), tuple(new_accs)

    def q_step(qi, carry):
        q0 = pl.multiple_of(qi * blk, blk)
        qbs = [(q_ref[pl.ds(q0, blk), h * dh:(h + 1) * dh].astype(F32) * scale).astype(BF16) for h in range(hg)]
        zeros = tuple(jnp.zeros((blk, blk), F32) for _ in range(hg))
        top = qi // kc
        rs, accs = chunk(qbs, q0, top, zeros, zeros, True)

        def k_step(j, c):
            return chunk(qbs, q0, top - 1 - j, c[0], c[1], False)

        _, accs = lax.fori_loop(0, top, k_step, (rs, accs))
        for h in range(hg):
            o_ref[pl.ds(q0, blk), h * dh:(h + 1) * dh] = accs[h].astype(o_ref.dtype)
        return carry

    lax.fori_loop(0, nblk, q_step, 0)


def _cumsum_matrix(blk):
    j = jnp.arange(2 * blk)[:, None] % blk
    c = jnp.arange(2 * blk)[None, :]
    return jnp.where((c >= blk) | (j > c), -1.0, 0.0).astype(BF16)


def sb_attention(qkv, *, batch, seq, heads, dh, hg=8, kc=4):
    assert dh == SB_BLOCK
    nblk = seq // SB_BLOCK
    assert nblk % kc == 0 and heads % hg == 0
    body = functools.partial(_sb_attn_body, scale=dh ** -0.5, nblk=nblk, hg=hg, kc=kc)
    cm = _cumsum_matrix(SB_BLOCK)
    ng = heads // hg
    return pl.pallas_call(
        body,
        grid=(batch, ng),
        in_specs=[
            pl.BlockSpec((seq, hg * dh), lambda b, h: (b, h)),
            pl.BlockSpec((seq, hg * dh), lambda b, h: (b, ng + h)),
            pl.BlockSpec((seq, hg * dh), lambda b, h: (b, 2 * ng + h)),
            pl.BlockSpec((2 * SB_BLOCK, 2 * SB_BLOCK), lambda b, h: (0, 0)),
        ],
        out_specs=pl.BlockSpec((seq, hg * dh), lambda b, h: (b, h)),
        out_shape=jax.ShapeDtypeStruct((batch * seq, heads * dh), BF16),
        compiler_params=_params("parallel", "parallel"),
        name="sb_attn",
    )(qkv, qkv, qkv, cm)


def _proj_route_body(o_ref, w_ref, x_ref, g_ref, wrh_ref, wrl_ref, br_ref, tri_ref,
                     x1_ref, h2_ref, ri_ref, rw_ref, cnt_ref, base_ref, *, tm):
    i = pl.program_id(0)

    @pl.when(i == 0)
    def _():
        base_ref[...] = jnp.zeros_like(base_ref)

    x1 = x_ref[...] + jnp.dot(o_ref[...], w_ref[...], preferred_element_type=F32)
    x1_ref[...] = x1
    h2 = _rms(x1, g_ref[...])
    h2_ref[...] = h2

    hh, hl = _split2(h2)
    wrh = wrh_ref[...]
    lg = (lax.dot_general(wrh, hh, NT_DIMS, preferred_element_type=F32)
          + lax.dot_general(wrh, hl, NT_DIMS, preferred_element_type=F32)
          + lax.dot_general(wrl_ref[...], hh, NT_DIMS, preferred_element_type=F32)) + br_ref[...]

    r8 = lax.broadcasted_iota(jnp.int32, (SUBLANES, tm), 0)
    neg_inf = jnp.float32(-jnp.inf)
    gl = jnp.where(r8 < N_GROUPS, lg[N_EXPERTS:N_EXPERTS + SUBLANES, :], neg_inf)
    gmax = jnp.max(gl, axis=0, keepdims=True)
    gsel = jnp.min(jnp.where(gl == gmax, r8, SUBLANES), axis=0, keepdims=True)
    p_sel = 1.0 / jnp.sum(jnp.exp(gl - gmax), axis=0, keepdims=True)

    es = lg[0:EXPERTS_PER_GROUP, :]
    for g in range(1, N_GROUPS):
        es = jnp.where(gsel == g, lg[g * EXPERTS_PER_GROUP:(g + 1) * EXPERTS_PER_GROUP, :], es)
    m1 = jnp.max(es, axis=0, keepdims=True)
    i1 = jnp.min(jnp.where(es == m1, r8, SUBLANES), axis=0, keepdims=True)
    es2 = jnp.where(r8 == i1, neg_inf, es)
    m2 = jnp.max(es2, axis=0, keepdims=True)
    i2 = jnp.min(jnp.where(es2 == m2, r8, SUBLANES), axis=0, keepdims=True)
    t = jnp.exp(m2 - m1)
    w1 = p_sel / (1.0 + t)
    w2 = w1 * t
    e1 = gsel * EXPERTS_PER_GROUP + i1
    e2 = gsel * EXPERTS_PER_GROUP + i2

    r32 = lax.broadcasted_iota(jnp.int32, (N_EXPERTS, tm), 0)
    hit1 = r32 == e1
    hit2 = r32 == e2
    onehot = (jnp.where(hit1, 1.0, 0.0) + jnp.where(hit2, 1.0, 0.0)).astype(BF16)
    cs = jnp.dot(onehot, tri_ref[...], preferred_element_type=F32)
    base = base_ref[...]
    before = cs[:, :tm] + base
    rank1 = jnp.sum(jnp.where(hit1, before, 0.0), axis=0, keepdims=True)
    rank2 = jnp.sum(jnp.where(hit2, before, 0.0), axis=0, keepdims=True)
    tot = cs[:, tm:]
    new_base = base + jnp.concatenate([tot] * (tm // LANES), axis=1)
    base_ref[...] = new_base
    cnt_ref[...] = new_base[:, :LANES]

    zi = jnp.zeros((SUBLANES, tm), jnp.int32)
    ri = jnp.where(r8 == 0, e1, zi)
    ri = jnp.where(r8 == 1, e2, ri)
    ri = jnp.where(r8 == 2, rank1.astype(jnp.int32), ri)
    ri = jnp.where(r8 == 3, rank2.astype(jnp.int32), ri)
    ri_ref[...] = ri
    zf = jnp.zeros((SUBLANES, tm), F32)
    rw = jnp.where(r8 == 0, w1, zf)
    rw = jnp.where(r8 == 1, w2, rw)
    rw_ref[...] = rw


def proj_route(o, w_out, x, g, w_group, b_group, w_expert, b_expert, *, tm=256):
    t, d = x.shape
    k = o.shape[1]
    wr = jnp.zeros((ROUTER_ROWS, d), F32)
    wr = wr.at[:N_EXPERTS].set(w_expert.T).at[N_EXPERTS:N_EXPERTS + N_GROUPS].set(w_group.T)
    wrh, wrl = _split2(wr)
    br = jnp.zeros((ROUTER_ROWS,), F32).at[:N_EXPERTS].set(b_expert).at[N_EXPERTS:N_EXPERTS + N_GROUPS].set(b_group)
    br = jnp.broadcast_to(br[:, None], (ROUTER_ROWS, tm))
    tp = jnp.arange(tm)[:, None]
    tri = jnp.concatenate([(tp < jnp.arange(tm)[None, :]).astype(BF16), jnp.ones((tm, LANES), BF16)], axis=1)
    body = functools.partial(_proj_route_body, tm=tm)
    row = lambda i: (i, 0)
    const = lambda i: (0, 0)
    return pl.pallas_call(
        body,
        grid=(t // tm,),
        in_specs=[
            pl.BlockSpec((tm, k), row),
            pl.BlockSpec((k, d), const),
            pl.BlockSpec((tm, d), row),
            pl.BlockSpec((1, d), const),
            pl.BlockSpec((ROUTER_ROWS, d), const),
            pl.BlockSpec((ROUTER_ROWS, d), const),
            pl.BlockSpec((ROUTER_ROWS, tm), const),
            pl.BlockSpec((tm, tm + LANES), const),
        ],
        out_specs=[
            pl.BlockSpec((tm, d), row),
            pl.BlockSpec((tm, d), row),
            pl.BlockSpec((SUBLANES, tm), lambda i: (0, i)),
            pl.BlockSpec((SUBLANES, tm), lambda i: (0, i)),
            pl.BlockSpec((N_EXPERTS, LANES), const),
        ],
        out_shape=[
            jax.ShapeDtypeStruct((t, d), F32),
            jax.ShapeDtypeStruct((t, d), F32),
            jax.ShapeDtypeStruct((SUBLANES, t), jnp.int32),
            jax.ShapeDtypeStruct((SUBLANES, t), F32),
            jax.ShapeDtypeStruct((N_EXPERTS, LANES), F32),
        ],
        scratch_shapes=[pltpu.VMEM((N_EXPERTS, tm), F32)],
        compiler_params=_params("arbitrary"),
        name="proj_route",
    )(o, w_out, x, g.reshape(1, d), wrh, wrl, br, tri)


def _experts_body(te_ref, nt_ref, src_ref, nxt_ref, h_hbm, w1_ref, w3_ref, w2_ref, y_ref,
                  xbuf, w1b, w3b, w2b, gsem, *, tm, n_tok, nt_max):
    i = pl.program_id(0)
    n_tiles = nt_ref[0]
    slot = lax.rem(i, 2)

    def row_in(tok, r, s):
        return pltpu.make_async_copy(h_hbm.at[pl.ds(tok, 1)], xbuf.at[s, pl.ds(r, 1)], gsem.at[s])

    def gather_start(idx_ref, s):
        for r in range(tm):
            p = idx_ref[0, 0, r]
            row_in(jnp.where(p < 0, 0, p & (n_tok - 1)), r, s).start()

    def gather_wait(s):
        for r in range(tm):
            row_in(0, r, s).wait()

    @pl.when(i == 0)
    def _():
        gather_start(src_ref, 0)

    @pl.when(i < n_tiles)
    def _():
        gather_wait(slot)

        @pl.when((i == 0) | (te_ref[i] != te_ref[jnp.maximum(i - 1, 0)]))
        def _():
            w1b[...] = w1_ref[0].astype(BF16)
            w3b[...] = w3_ref[0].astype(BF16)
            w2b[...] = w2_ref[0].astype(BF16)

        xb = xbuf[slot].astype(BF16)
        gather_start(nxt_ref, 1 - slot)
        a = jnp.dot(xb, w1b[...], preferred_element_type=F32)
        b = jnp.dot(xb, w3b[...], preferred_element_type=F32)
        hid = (a * (1.0 / (1.0 + jnp.exp(-a))) * b).astype(BF16)
        y_ref[...] = jnp.dot(hid, w2b[...], preferred_element_type=F32)

    @pl.when(i >= n_tiles)
    def _():
        y_ref[...] = jnp.zeros_like(y_ref)

    @pl.when(i == n_tiles)
    def _():
        gather_wait(slot)

    @pl.when((i == nt_max - 1) & (i < n_tiles))
    def _():
        gather_wait(1 - slot)


def _route_tables(ri, cnt, *, tm, n_tok):
    nt_max = TOP_K * n_tok // tm + N_EXPERTS
    counts = cnt[:, 0].astype(jnp.int32)
    padded = ((counts + tm - 1) // tm) * tm
    ends = jnp.cumsum(padded)
    offs = ends - padded
    eid = jnp.arange(N_EXPERTS, dtype=jnp.int32)
    sel = ri[0:TOP_K, :, None] == eid
    pos = jnp.sum(jnp.where(sel, offs, 0), axis=-1) + ri[TOP_K:2 * TOP_K]
    pair = jnp.arange(TOP_K * n_tok, dtype=jnp.int32)
    src = jnp.full((nt_max * tm,), -1, jnp.int32).at[pos.reshape(-1)].set(pair)
    n_tiles = (ends[-1] // tm).astype(jnp.int32)
    tile_start = jnp.arange(nt_max, dtype=jnp.int32) * tm
    te = jnp.sum((tile_start[:, None] >= ends[None, :]).astype(jnp.int32), axis=1)
    last = jnp.sum((tile_start[jnp.maximum(n_tiles - 1, 0)] >= ends).astype(jnp.int32))
    te = jnp.clip(jnp.minimum(te, last), 0, N_EXPERTS - 1)
    return pos, src.reshape(nt_max, 1, tm), te, n_tiles.reshape(1), nt_max


def routed_experts(h2, src3, te, n_tiles, w1, w3, w2, layer, *, tm, nt_max):
    t, d = h2.shape
    f = w1.shape[-1]
    body = functools.partial(_experts_body, tm=tm, n_tok=t, nt_max=nt_max)
    smem_tile = lambda imap: pl.BlockSpec((1, 1, tm), imap, memory_space=pltpu.SMEM)
    grid_spec = pltpu.PrefetchScalarGridSpec(
        num_scalar_prefetch=2,
        grid=(nt_max,),
        in_specs=[
            smem_tile(lambda i, te, nt: (i, 0, 0)),
            smem_tile(lambda i, te, nt: (jnp.minimum(i + 1, nt_max - 1), 0, 0)),
            pl.BlockSpec(memory_space=pl.ANY),
            pl.BlockSpec((None, 1, d, f), lambda i, te, nt: (layer, te[i], 0, 0)),
            pl.BlockSpec((None, 1, d, f), lambda i, te, nt: (layer, te[i], 0, 0)),
            pl.BlockSpec((None, 1, f, d), lambda i, te, nt: (layer, te[i], 0, 0)),
        ],
        out_specs=pl.BlockSpec((tm, d), lambda i, te, nt: (i, 0)),
        scratch_shapes=[
            pltpu.VMEM((2, tm, d), F32),
            pltpu.VMEM((d, f), BF16),
            pltpu.VMEM((d, f), BF16),
            pltpu.VMEM((f, d), BF16),
            pltpu.SemaphoreType.DMA((2,)),
        ],
    )
    return pl.pallas_call(
        body,
        grid_spec=grid_spec,
        out_shape=jax.ShapeDtypeStruct((nt_max * tm, d), F32),
        compiler_params=_params("arbitrary"),
        name="experts",
    )(te, n_tiles, src3, src3, h2, w1, w3, w2)


def _combine_body(pos_ref, nxt_ref, x_ref, w_ref, g_ref, ys_hbm, *rest, tm, final):
    if final:
        h_ref, ybuf, sem = rest
        x2_ref = None
    else:
        x2_ref, h_ref, ybuf, sem = rest
    i = pl.program_id(0)
    n = pl.num_programs(0)
    slot = lax.rem(i, 2)

    def row_in(src, k, r, s):
        return pltpu.make_async_copy(ys_hbm.at[pl.ds(src, 1)], ybuf.at[s, k, pl.ds(r, 1)], sem.at[s])

    def gather_start(idx_ref, s):
        for k in range(TOP_K):
            for r in range(tm):
                row_in(idx_ref[0, 0, k * tm + r], k, r, s).start()

    def gather_wait(s):
        for k in range(TOP_K):
            for r in range(tm):
                row_in(0, k, r, s).wait()

    @pl.when(i == 0)
    def _():
        gather_start(pos_ref, 0)

    gather_wait(slot)

    @pl.when(i + 1 < n)
    def _():
        gather_start(nxt_ref, 1 - slot)

    w = w_ref[...]
    x2 = x_ref[...] + w[:, 0:1] * ybuf[slot, 0] + w[:, 1:2] * ybuf[slot, 1]
    if x2_ref is not None:
        x2_ref[...] = x2
    h_ref[...] = _rms(x2, g_ref[...]).astype(h_ref.dtype)


def moe_combine(x1, ys, pos, rw, g, *, final, tm=128):
    t, d = x1.shape
    nt = t // tm
    w = rw[0:TOP_K].T
    pos3 = pos.reshape(TOP_K, nt, tm).transpose(1, 0, 2).reshape(nt, 1, TOP_K * tm)
    row = lambda i: (i, 0)
    smem_tile = lambda imap: pl.BlockSpec((1, 1, TOP_K * tm), imap, memory_space=pltpu.SMEM)
    in_specs = [
        smem_tile(lambda i: (i, 0, 0)),
        smem_tile(lambda i: (jnp.minimum(i + 1, nt - 1), 0, 0)),
        pl.BlockSpec((tm, d), row),
        pl.BlockSpec((tm, TOP_K), row),
        pl.BlockSpec((1, d), lambda i: (0, 0)),
        pl.BlockSpec(memory_space=pl.ANY),
    ]
    scratch = [pltpu.VMEM((2, TOP_K, tm, d), F32), pltpu.SemaphoreType.DMA((2,))]
    args = (pos3, pos3, x1, w, g.reshape(1, d), ys)
    body = functools.partial(_combine_body, tm=tm, final=final)
    if final:
        return pl.pallas_call(
            body, grid=(nt,), in_specs=in_specs,
            out_specs=pl.BlockSpec((tm, d), row),
            out_shape=jax.ShapeDtypeStruct((t, d), F32),
            scratch_shapes=scratch,
            compiler_params=_params("arbitrary"), name="combine_final",
        )(*args)
    return pl.pallas_call(
        body, grid=(nt,), in_specs=in_specs,
        out_specs=[pl.BlockSpec((tm, d), row), pl.BlockSpec((tm, d), row)],
        out_shape=[jax.ShapeDtypeStruct((t, d), F32), jax.ShapeDtypeStruct((t, d), BF16)],
        scratch_shapes=scratch,
        compiler_params=_params("arbitrary"), name="combine",
    )(*args)


def _log_sigmoid(x):
    return jnp.minimum(x, 0.0) - jnp.log(1.0 + jnp.exp(-jnp.abs(x)))


def _gates_body(h_ref, wch_ref, wcl_ref, wrh_ref, wrl_ref, bc_ref, br_ref, gc_ref, gr_ref, *, nh):
    h = h_ref[...]
    col = (jnp.dot(h, wch_ref[...], preferred_element_type=F32)
           + jnp.dot(h, wcl_ref[...], preferred_element_type=F32)) + bc_ref[...]
    lane = lax.broadcasted_iota(jnp.int32, col.shape, 1)
    gc_ref[...] = jnp.where(lane < nh, col, _log_sigmoid(col))
    rw = (lax.dot_general(wrh_ref[...], h, NT_DIMS, preferred_element_type=F32)
          + lax.dot_general(wrl_ref[...], h, NT_DIMS, preferred_element_type=F32)) + br_ref[...]
    sub = lax.broadcasted_iota(jnp.int32, rw.shape, 0)
    gr_ref[...] = jnp.where(sub < nh, rw, _log_sigmoid(rw))


def mlstm_gates(h, w_if, b_i, b_f, *, tm=512):
    t, d = h.shape
    nh = b_i.shape[0]
    wc = jnp.zeros((d, LANES), F32).at[:, :2 * nh].set(w_if)
    wch, wcl = _split2(wc)
    wrh, wrl = _split2(w_if.T)
    bias = jnp.concatenate([b_i, b_f])
    bc = jnp.zeros((1, LANES), F32).at[0, :2 * nh].set(bias)
    br = jnp.broadcast_to(bias[:, None], (2 * nh, tm))
    const = lambda i: (0, 0)
    return pl.pallas_call(
        functools.partial(_gates_body, nh=nh),
        grid=(t // tm,),
        in_specs=[
            pl.BlockSpec((tm, d), lambda i: (i, 0)),
            pl.BlockSpec((d, LANES), const), pl.BlockSpec((d, LANES), const),
            pl.BlockSpec((2 * nh, d), const), pl.BlockSpec((2 * nh, d), const),
            pl.BlockSpec((1, LANES), const), pl.BlockSpec((2 * nh, tm), const),
        ],
        out_specs=[pl.BlockSpec((tm, LANES), lambda i: (i, 0)), pl.BlockSpec((2 * nh, tm), lambda i: (0, i))],
        out_shape=[jax.ShapeDtypeStruct((t, LANES), F32), jax.ShapeDtypeStruct((2 * nh, t), F32)],
        compiler_params=_params("parallel"),
        name="mlstm_gates",
    )(h, wch, wcl, wrh, wrl, bc, br)


def _mlstm_body(q_ref, k_ref, v_ref, o_ref, gc_ref, gr_ref, gh_ref, tril_ref, triu_ref, out_ref,
                c_ref, n_ref, m_ref, *, scale, nchunk, dv):
    L = ML_CHUNK
    row = lax.broadcasted_iota(jnp.int32, (L, L), 0)
    col = lax.broadcasted_iota(jnp.int32, (L, L), 1)
    tmask = col <= row
    c_ref[...] = jnp.zeros_like(c_ref)
    n_ref[...] = jnp.zeros_like(n_ref)
    m_ref[...] = jnp.zeros_like(m_ref)

    def cumsum_exact(parts, tri, left):
        acc = None
        for p in parts:
            d = (jnp.dot(tri, p, preferred_element_type=F32) if left
                 else jnp.dot(p, tri, preferred_element_type=F32))
            acc = d if acc is None else acc + d
        return acc

    def step(c, carry):
        c0 = pl.multiple_of(c * L, L)
        qc = (q_ref[pl.ds(c0, L), :].astype(F32) * scale).astype(BF16)
        kc = k_ref[pl.ds(c0, L), :]
        vc = v_ref[pl.ds(c0, L), :]
        gc = gc_ref[0, pl.ds(c0, L), :]
        gr = gr_ref[0, :, pl.ds(c0, L)]
        i_col, f_col = gc[:, 0:1], gc[:, 1:2]
        i_row, f_row = gr[0:1, :], gr[1:2, :]
        m_prev = m_ref[...]

        b_col = cumsum_exact(_split3(jnp.broadcast_to(f_col, (L, L))), tril_ref[...], True)
        b_row = cumsum_exact(_split3(jnp.broadcast_to(f_row, (SUBLANES, L))), triu_ref[...], False)[0:1, :]
        bt = b_col[:, 0:1]
        b_last = b_col[L - 1:L, 0:1]

        d_mat = jnp.where(tmask, b_col + (i_row - b_row), -1e30)
        m_t = jnp.maximum(bt + m_prev, jnp.max(d_mat, axis=1, keepdims=True))
        e = jnp.where(tmask, jnp.exp(d_mat - m_t), 0.0)
        w = lax.dot_general(qc, kc, NT_DIMS, preferred_element_type=F32) * e
        inter = jnp.exp(bt + m_prev - m_t)
        c_prev = c_ref[...]
        n_prev = n_ref[...]
        num = inter * jnp.dot(qc, c_prev.astype(BF16), preferred_element_type=F32) \
            + jnp.dot(w.astype(BF16), vc, preferred_element_type=F32)
        qn = jnp.sum(qc.astype(F32) * n_prev, axis=1, keepdims=True)
        den = inter * qn + jnp.sum(w, axis=1, keepdims=True)
        h = num / jnp.maximum(jnp.abs(den), jnp.exp(-m_t))

        hs = h * lax.rsqrt(jnp.mean(h * h, axis=1, keepdims=True) + EPS) * gh_ref[...]
        og = o_ref[pl.ds(c0, L), :].astype(F32)
        out_ref[pl.ds(c0, L), :] = (hs / (1.0 + jnp.exp(-og))).astype(out_ref.dtype)

        key_log = b_last - bt + i_col
        m_new = jnp.maximum(b_last + m_prev, jnp.max(key_log, axis=0, keepdims=True))
        decay = jnp.exp(key_log - m_new)
        keep = jnp.exp(b_last + m_prev - m_new)
        kd = kc.astype(F32) * decay
        c_ref[...] = keep * c_prev + jnp.dot(kd.T.astype(BF16), vc, preferred_element_type=F32)
        n_ref[...] = keep * n_prev + jnp.sum(kd, axis=0, keepdims=True)
        m_ref[...] = m_new
        return carry

    lax.fori_loop(0, nchunk, step, 0)


def mlstm(qkvo, gcol, grow, g_head, *, batch, seq, heads, dqk, dv):
    t = batch * seq
    nh = heads
    gc3 = jnp.stack([gcol[:, :nh].T, gcol[:, nh:2 * nh].T], axis=-1)
    gr3 = jnp.stack([grow[:nh], grow[nh:2 * nh]], axis=1)
    L = ML_CHUNK
    idx = jnp.arange(L)
    tril = (idx[None, :] <= idx[:, None]).astype(BF16)
    triu = (idx[:, None] <= idx[None, :]).astype(BF16)
    kq = nh * dqk // dqk
    del kq
    body = functools.partial(_mlstm_body, scale=dqk ** -0.5, nchunk=seq // L, dv=dv)
    v_blk0 = 2 * nh * dqk // dv
    o_blk0 = v_blk0 + nh
    const = lambda b, h: (0, 0)
    return pl.pallas_call(
        body,
        grid=(batch, nh),
        in_specs=[
            pl.BlockSpec((seq, dqk), lambda b, h: (b, h)),
            pl.BlockSpec((seq, dqk), lambda b, h: (b, nh + h)),
            pl.BlockSpec((seq, dv), lambda b, h: (b, v_blk0 + h)),
            pl.BlockSpec((seq, dv), lambda b, h: (b, o_blk0 + h)),
            pl.BlockSpec((1, seq, 2), lambda b, h: (h, b, 0)),
            pl.BlockSpec((1, 2, seq), lambda b, h: (h, 0, b)),
            pl.BlockSpec((1, dv), lambda b, h: (0, h)),
            pl.BlockSpec((L, L), const),
            pl.BlockSpec((L, L), const),
        ],
        out_specs=pl.BlockSpec((seq, dv), lambda b, h: (b, h)),
        out_shape=jax.ShapeDtypeStruct((t, nh * dv), BF16),
        scratch_shapes=[pltpu.VMEM((dqk, dv), F32), pltpu.VMEM((1, dqk), F32), pltpu.VMEM((1, 1), F32)],
        compiler_params=_params("parallel", "parallel"),
        name="mlstm",
    )(qkvo, qkvo, qkvo, qkvo, gc3, gr3, g_head.reshape(1, nh * dv), tril, triu)


def _moe(o, w_out, x, norm_ffn, w_group, b_group, w_expert, b_expert, w1, w3, w2, layer, g_next, *, final,
         tm=256):
    x1, h2, ri, rw, cnt = proj_route(o, w_out.astype(BF16), x, norm_ffn, w_group, b_group, w_expert, b_expert)
    pos, src3, te, n_tiles, nt_max = _route_tables(ri, cnt, tm=tm, n_tok=x.shape[0])
    ys = routed_experts(h2, src3, te, n_tiles, w1, w3, w2, layer, tm=tm, nt_max=nt_max)
    return moe_combine(x1, ys, pos, rw, g_next, final=final)


def kernel(x, norm_mix, norm_ffn, norm_final, sb_w_in, sb_w_out, ml_w_in, ml_b_i, ml_b_f, ml_g_head, ml_w_out,
           moe_w_group, moe_b_group, moe_w_expert, moe_b_expert, moe_w1, moe_w3, moe_w2):
    batch, seq, d = x.shape
    xt = x.reshape(batch * seq, d)

    h = rmsnorm_bf16(xt, norm_mix[0])
    qkv = matmul_bf16(h, sb_w_in, 3 * d)
    o = sb_attention(qkv, batch=batch, seq=seq, heads=SB_HEADS, dh=d // SB_HEADS)
    xt, h = _moe(o, sb_w_out[0], xt, norm_ffn[0], moe_w_group[0], moe_b_group[0], moe_w_expert[0],
                 moe_b_expert[0], moe_w1, moe_w3, moe_w2, 0, norm_mix[1], final=False)

    dqk = (d // 2) // ML_HEADS
    dv = d // ML_HEADS
    n_main = 2 * ML_HEADS * dqk + 2 * ML_HEADS * dv
    qkvo = matmul_bf16(h, ml_w_in, n_main)
    gcol, grow = mlstm_gates(h, ml_w_in[0][:, n_main:], ml_b_i[0], ml_b_f[0])
    o = mlstm(qkvo, gcol, grow, ml_g_head[0], batch=batch, seq=seq, heads=ML_HEADS, dqk=dqk, dv=dv)
    out = _moe(o, ml_w_out[0], xt, norm_ffn[1], moe_w_group[1], moe_b_group[1], moe_w_expert[1],
               moe_b_expert[1], moe_w1, moe_w3, moe_w2, 1, norm_final, final=True)
    return out.reshape(batch, seq, d)
```

```python
import functools

import jax
import jax.numpy as jnp
from jax import lax
from jax.experimental import pallas as pl
from jax.experimental.pallas import tpu as pltpu

F32 = jnp.float32
BF16 = jnp.bfloat16
EPS = 1e-6

V7X_VMEM_BYTES = 64 * 1024 * 1024
VMEM_LIMIT_BYTES = V7X_VMEM_BYTES - 8 * 1024 * 1024
LANES = 128
SUBLANES = 8

SB_HEADS = 16
ML_HEADS = 8
N_GROUPS = 4
EXPERTS_PER_GROUP = 8
N_EXPERTS = N_GROUPS * EXPERTS_PER_GROUP
TOP_K = 2
ML_CHUNK = 128
SB_BLOCK = 128
ROUTER_ROWS = 128

NT_DIMS = (((1,), (1,)), ((), ()))


def _params(*sem):
    return pltpu.CompilerParams(dimension_semantics=sem, vmem_limit_bytes=VMEM_LIMIT_BYTES)


def _split2(x):
    hi = x.astype(BF16)
    lo = (x - hi.astype(F32)).astype(BF16)
    return hi, lo


def _split3(x):
    hi = x.astype(BF16)
    r = x - hi.astype(F32)
    mid = r.astype(BF16)
    lo = (r - mid.astype(F32)).astype(BF16)
    return hi, mid, lo


def _rms(x, g):
    return x * lax.rsqrt(jnp.mean(x * x, axis=-1, keepdims=True) + EPS) * g


def _slab_rows(d):
    return d // LANES


def _slab_pitch(d):
    return _slab_rows(d) + 1


def _slab_store(ref, x, n):
    d = x.shape[1]
    pitch = _slab_pitch(d)
    for c in range(_slab_rows(d)):
        ref[pl.ds(c, n, stride=pitch), :] = x[:, c * LANES:(c + 1) * LANES]
    ref[pl.ds(_slab_rows(d), n, stride=pitch), :] = jnp.zeros((n, LANES), x.dtype)


def _slab_load(ref, lead, n, d):
    pitch = _slab_pitch(d)
    return jnp.concatenate([ref[lead + (pl.ds(c, n, stride=pitch), slice(None))] for c in range(_slab_rows(d))],
                           axis=1)


def _rmsnorm_body(x_ref, g_ref, h_ref):
    h_ref[...] = _rms(x_ref[...], g_ref[...]).astype(h_ref.dtype)


def rmsnorm_bf16(x, g, *, tm=512):
    t, d = x.shape
    return pl.pallas_call(
        _rmsnorm_body,
        grid=(t // tm,),
        in_specs=[pl.BlockSpec((tm, d), lambda i: (i, 0)), pl.BlockSpec((1, d), lambda i: (0, 0))],
        out_specs=pl.BlockSpec((tm, d), lambda i: (i, 0)),
        out_shape=jax.ShapeDtypeStruct((t, d), BF16),
        compiler_params=_params("parallel"),
        name="rmsnorm",
    )(x, g.reshape(1, d))


def _mm_body(h_ref, w_ref, o_ref, wb_ref):
    @pl.when(pl.program_id(1) == 0)
    def _():
        wb_ref[...] = w_ref[...].astype(BF16)

    o_ref[...] = jnp.dot(h_ref[...], wb_ref[...], preferred_element_type=F32).astype(o_ref.dtype)


def matmul_bf16(h, w, n, *, tm=1024, tn=1024):
    t, k = h.shape
    return pl.pallas_call(
        _mm_body,
        grid=(n // tn, t // tm),
        in_specs=[pl.BlockSpec((tm, k), lambda j, i: (i, 0)), pl.BlockSpec((None, k, tn), lambda j, i: (0, 0, j))],
        out_specs=pl.BlockSpec((tm, tn), lambda j, i: (i, j)),
        out_shape=jax.ShapeDtypeStruct((t, n), BF16),
        scratch_shapes=[pltpu.VMEM((k, tn), BF16)],
        compiler_params=_params("parallel", "arbitrary"),
        name="proj_in",
    )(h, w)


def _sb_attn_body(q_ref, k_ref, v_ref, cm_ref, o_ref, *, scale, nblk, hg, kc):
    blk = SB_BLOCK
    dh = SB_BLOCK
    wid = kc * blk
    rel = lax.broadcasted_iota(jnp.int32, (blk, wid), 1) - lax.broadcasted_iota(jnp.int32, (blk, wid), 0)

    def chunk(qbs, q0, c, rs, accs, masked):
        k0 = pl.multiple_of(c * wid, wid)
        if masked:
            valid = rel < (q0 - k0)
        z = jnp.concatenate(
            [lax.dot_general(qbs[h], k_ref[pl.ds(k0, wid), h * dh:(h + 1) * dh], NT_DIMS,
                             preferred_element_type=F32) for h in range(hg)], axis=1)
        sp = jnp.maximum(z, 0.0) + jnp.log(1.0 + jnp.exp(-jnp.abs(z)))
        log_beta = z - sp
        if masked:
            valid_all = jnp.concatenate([valid] * hg, axis=1)
            sp = jnp.where(valid_all, sp, 0.0)
        hi, lo = _split2(sp)
        nb = hg * kc
        x = jnp.concatenate(
            [jnp.concatenate([hi[:, j * blk:(j + 1) * blk], lo[:, j * blk:(j + 1) * blk]], axis=1)
             for j in range(nb)], axis=0)
        cs = jnp.dot(x, cm_ref[...], preferred_element_type=F32)
        new_rs = []
        suffix = [None] * nb
        for h in range(hg):
            r = rs[h]
            for j in reversed(range(kc)):
                csj = cs[(h * kc + j) * blk:(h * kc + j + 1) * blk]
                suffix[h * kc + j] = csj[:, :blk] + r
                r = r + csj[:, blk:]
            new_rs.append(r)
        a = jnp.exp(log_beta + jnp.concatenate(suffix, axis=1))
        if masked:
            a = jnp.where(valid_all, a, 0.0)
        a = a.astype(BF16)
        new_accs = [accs[h] + jnp.dot(a[:, h * wid:(h + 1) * wid], v_ref[pl.ds(k0, wid), h * dh:(h + 1) * dh],
                                      preferred_element_type=F32) for h in range(hg)]
        return tuple(new_rs), tuple(new_accs)

    def q_step(qi, carry):
        q0 = pl.multiple_of(qi * blk, blk)
        qbs = [(q_ref[pl.ds(q0, blk), h * dh:(h + 1) * dh].astype(F32) * scale).astype(BF16) for h in range(hg)]
        zeros = tuple(jnp.zeros((blk, blk), F32) for _ in range(hg))
        top = qi // kc
        rs, accs = chunk(qbs, q0, top, zeros, zeros, True)

        def k_step(j, c):
            return chunk(qbs, q0, top - 1 - j, c[0], c[1], False)

        _, accs = lax.fori_loop(0, top, k_step, (rs, accs))
        for h in range(hg):
            o_ref[pl.ds(q0, blk), h * dh:(h + 1) * dh] = accs[h].astype(o_ref.dtype)
        return carry

    lax.fori_loop(0, nblk, q_step, 0)


def _cumsum_matrix(blk):
    j = jnp.arange(2 * blk)[:, None] % blk
    c = jnp.arange(2 * blk)[None, :]
    return jnp.where((c >= blk) | (j > c), -1.0, 0.0).astype(BF16)


def sb_attention(qkv, *, batch, seq, heads, dh, hg=8, kc=4):
    assert dh == SB_BLOCK
    nblk = seq // SB_BLOCK
    assert nblk % kc == 0 and heads % hg == 0
    body = functools.partial(_sb_attn_body, scale=dh ** -0.5, nblk=nblk, hg=hg, kc=kc)
    cm = _cumsum_matrix(SB_BLOCK)
    ng = heads // hg
    return pl.pallas_call(
        body,
        grid=(batch, ng),
        in_specs=[
            pl.BlockSpec((seq, hg * dh), lambda b, h: (b, h)),
            pl.BlockSpec((seq, hg * dh), lambda b, h: (b, ng + h)),
            pl.BlockSpec((seq, hg * dh), lambda b, h: (b, 2 * ng + h)),
            pl.BlockSpec((2 * SB_BLOCK, 2 * SB_BLOCK), lambda b, h: (0, 0)),
        ],
        out_specs=pl.BlockSpec((seq, hg * dh), lambda b, h: (b, h)),
        out_shape=jax.ShapeDtypeStruct((batch * seq, heads * dh), BF16),
        compiler_params=_params("parallel", "parallel"),
        name="sb_attn",
    )(qkv, qkv, qkv, cm)


def _proj_route_body(o_ref, w_ref, x_ref, g_ref, wrh_ref, wrl_ref, br_ref, tri_ref,
                     x1_ref, h2_ref, ri_ref, rw_ref, cnt_ref, base_ref, *, tm):
    i = pl.program_id(0)

    @pl.when(i == 0)
    def _():
        base_ref[...] = jnp.zeros_like(base_ref)

    x1 = x_ref[...] + jnp.dot(o_ref[...], w_ref[...], preferred_element_type=F32)
    x1_ref[...] = x1
    h2 = _rms(x1, g_ref[...])
    _slab_store(h2_ref, h2, tm)

    hh, hl = _split2(h2)
    wrh = wrh_ref[...]
    lg = (lax.dot_general(wrh, hh, NT_DIMS, preferred_element_type=F32)
          + lax.dot_general(wrh, hl, NT_DIMS, preferred_element_type=F32)
          + lax.dot_general(wrl_ref[...], hh, NT_DIMS, preferred_element_type=F32)) + br_ref[...]

    r8 = lax.broadcasted_iota(jnp.int32, (SUBLANES, tm), 0)
    neg_inf = jnp.float32(-jnp.inf)
    gl = jnp.where(r8 < N_GROUPS, lg[N_EXPERTS:N_EXPERTS + SUBLANES, :], neg_inf)
    gmax = jnp.max(gl, axis=0, keepdims=True)
    gsel = jnp.min(jnp.where(gl == gmax, r8, SUBLANES), axis=0, keepdims=True)
    p_sel = 1.0 / jnp.sum(jnp.exp(gl - gmax), axis=0, keepdims=True)

    es = lg[0:EXPERTS_PER_GROUP, :]
    for g in range(1, N_GROUPS):
        es = jnp.where(gsel == g, lg[g * EXPERTS_PER_GROUP:(g + 1) * EXPERTS_PER_GROUP, :], es)
    m1 = jnp.max(es, axis=0, keepdims=True)
    i1 = jnp.min(jnp.where(es == m1, r8, SUBLANES), axis=0, keepdims=True)
    es2 = jnp.where(r8 == i1, neg_inf, es)
    m2 = jnp.max(es2, axis=0, keepdims=True)
    i2 = jnp.min(jnp.where(es2 == m2, r8, SUBLANES), axis=0, keepdims=True)
    t = jnp.exp(m2 - m1)
    w1 = p_sel / (1.0 + t)
    w2 = w1 * t
    e1 = gsel * EXPERTS_PER_GROUP + i1
    e2 = gsel * EXPERTS_PER_GROUP + i2

    r32 = lax.broadcasted_iota(jnp.int32, (N_EXPERTS, tm), 0)
    hit1 = r32 == e1
    hit2 = r32 == e2
    onehot = (jnp.where(hit1, 1.0, 0.0) + jnp.where(hit2, 1.0, 0.0)).astype(BF16)
    cs = jnp.dot(onehot, tri_ref[...], preferred_element_type=F32)
    base = base_ref[...]
    before = cs[:, :tm] + base
    rank1 = jnp.sum(jnp.where(hit1, before, 0.0), axis=0, keepdims=True)
    rank2 = jnp.sum(jnp.where(hit2, before, 0.0), axis=0, keepdims=True)
    tot = cs[:, tm:]
    new_base = base + jnp.concatenate([tot] * (tm // LANES), axis=1)
    base_ref[...] = new_base
    cnt_ref[...] = new_base[:, :LANES]

    zi = jnp.zeros((SUBLANES, tm), jnp.int32)
    ri = jnp.where(r8 == 0, e1, zi)
    ri = jnp.where(r8 == 1, e2, ri)
    ri = jnp.where(r8 == 2, rank1.astype(jnp.int32), ri)
    ri = jnp.where(r8 == 3, rank2.astype(jnp.int32), ri)
    ri_ref[...] = ri
    zf = jnp.zeros((SUBLANES, tm), F32)
    rw = jnp.where(r8 == 0, w1, zf)
    rw = jnp.where(r8 == 1, w2, rw)
    rw_ref[...] = rw


def proj_route(o, w_out, x, g, w_group, b_group, w_expert, b_expert, *, tm=256):
    t, d = x.shape
    k = o.shape[1]
    wr = jnp.zeros((ROUTER_ROWS, d), F32)
    wr = wr.at[:N_EXPERTS].set(w_expert.T).at[N_EXPERTS:N_EXPERTS + N_GROUPS].set(w_group.T)
    wrh, wrl = _split2(wr)
    br = jnp.zeros((ROUTER_ROWS,), F32).at[:N_EXPERTS].set(b_expert).at[N_EXPERTS:N_EXPERTS + N_GROUPS].set(b_group)
    br = jnp.broadcast_to(br[:, None], (ROUTER_ROWS, tm))
    tp = jnp.arange(tm)[:, None]
    tri = jnp.concatenate([(tp < jnp.arange(tm)[None, :]).astype(BF16), jnp.ones((tm, LANES), BF16)], axis=1)
    body = functools.partial(_proj_route_body, tm=tm)
    row = lambda i: (i, 0)
    const = lambda i: (0, 0)
    return pl.pallas_call(
        body,
        grid=(t // tm,),
        in_specs=[
            pl.BlockSpec((tm, k), row),
            pl.BlockSpec((k, d), const),
            pl.BlockSpec((tm, d), row),
            pl.BlockSpec((1, d), const),
            pl.BlockSpec((ROUTER_ROWS, d), const),
            pl.BlockSpec((ROUTER_ROWS, d), const),
            pl.BlockSpec((ROUTER_ROWS, tm), const),
            pl.BlockSpec((tm, tm + LANES), const),
        ],
        out_specs=[
            pl.BlockSpec((tm, d), row),
            pl.BlockSpec((tm * _slab_pitch(d), LANES), row),
            pl.BlockSpec((SUBLANES, tm), lambda i: (0, i)),
            pl.BlockSpec((SUBLANES, tm), lambda i: (0, i)),
            pl.BlockSpec((N_EXPERTS, LANES), const),
        ],
        out_shape=[
            jax.ShapeDtypeStruct((t, d), F32),
            jax.ShapeDtypeStruct((t * _slab_pitch(d), LANES), F32),
            jax.ShapeDtypeStruct((SUBLANES, t), jnp.int32),
            jax.ShapeDtypeStruct((SUBLANES, t), F32),
            jax.ShapeDtypeStruct((N_EXPERTS, LANES), F32),
        ],
        scratch_shapes=[pltpu.VMEM((N_EXPERTS, tm), F32)],
        compiler_params=_params("arbitrary"),
        name="proj_route",
    )(o, w_out, x, g.reshape(1, d), wrh, wrl, br, tri)


def _experts_body(te_ref, nt_ref, src_ref, nxt_ref, h_hbm, w1_ref, w3_ref, w2_ref, y_ref,
                  xbuf, w1b, w3b, w2b, gsem, *, tm, n_tok, nt_max, d):
    i = pl.program_id(0)
    n_tiles = nt_ref[0]
    slot = lax.rem(i, 2)
    rows, pitch = _slab_rows(d), _slab_pitch(d)

    def row_in(tok, r, s):
        return pltpu.make_async_copy(h_hbm.at[pl.ds(tok * pitch, rows)], xbuf.at[s, pl.ds(r * pitch, rows)],
                                     gsem.at[s])

    def gather_start(idx_ref, s):
        for r in range(tm):
            p = idx_ref[0, 0, r]
            row_in(jnp.where(p < 0, 0, p & (n_tok - 1)), r, s).start()

    def gather_wait(s):
        for r in range(tm):
            row_in(0, r, s).wait()

    @pl.when(i == 0)
    def _():
        gather_start(src_ref, 0)

    @pl.when(i < n_tiles)
    def _():
        gather_wait(slot)

        @pl.when((i == 0) | (te_ref[i] != te_ref[jnp.maximum(i - 1, 0)]))
        def _():
            w1b[...] = w1_ref[0].astype(BF16)
            w3b[...] = w3_ref[0].astype(BF16)
            w2b[...] = w2_ref[0].astype(BF16)

        xb = _slab_load(xbuf, (slot,), tm, d).astype(BF16)
        gather_start(nxt_ref, 1 - slot)
        a = jnp.dot(xb, w1b[...], preferred_element_type=F32)
        b = jnp.dot(xb, w3b[...], preferred_element_type=F32)
        hid = (a * (1.0 / (1.0 + jnp.exp(-a))) * b).astype(BF16)
        _slab_store(y_ref, jnp.dot(hid, w2b[...], preferred_element_type=F32), tm)

    @pl.when(i >= n_tiles)
    def _():
        y_ref[...] = jnp.zeros_like(y_ref)

    @pl.when(i == n_tiles)
    def _():
        gather_wait(slot)

    @pl.when((i == nt_max - 1) & (i < n_tiles))
    def _():
        gather_wait(1 - slot)


def _route_tables(ri, cnt, *, tm, n_tok):
    nt_max = TOP_K * n_tok // tm + N_EXPERTS
    counts = cnt[:, 0].astype(jnp.int32)
    padded = ((counts + tm - 1) // tm) * tm
    ends = jnp.cumsum(padded)
    offs = ends - padded
    eid = jnp.arange(N_EXPERTS, dtype=jnp.int32)
    sel = ri[0:TOP_K, :, None] == eid
    pos = jnp.sum(jnp.where(sel, offs, 0), axis=-1) + ri[TOP_K:2 * TOP_K]
    pair = jnp.arange(TOP_K * n_tok, dtype=jnp.int32)
    src = jnp.full((nt_max * tm,), -1, jnp.int32).at[pos.reshape(-1)].set(pair)
    n_tiles = (ends[-1] // tm).astype(jnp.int32)
    tile_start = jnp.arange(nt_max, dtype=jnp.int32) * tm
    te = jnp.sum((tile_start[:, None] >= ends[None, :]).astype(jnp.int32), axis=1)
    last = jnp.sum((tile_start[jnp.maximum(n_tiles - 1, 0)] >= ends).astype(jnp.int32))
    te = jnp.clip(jnp.minimum(te, last), 0, N_EXPERTS - 1)
    return pos, src.reshape(nt_max, 1, tm), te, n_tiles.reshape(1), nt_max


def routed_experts(h2, src3, te, n_tiles, w1, w3, w2, layer, *, tm, nt_max):
    d, f = w1.shape[-2:]
    pitch = _slab_pitch(d)
    t = h2.shape[0] // pitch
    body = functools.partial(_experts_body, tm=tm, n_tok=t, nt_max=nt_max, d=d)
    smem_tile = lambda imap: pl.BlockSpec((1, 1, tm), imap, memory_space=pltpu.SMEM)
    grid_spec = pltpu.PrefetchScalarGridSpec(
        num_scalar_prefetch=2,
        grid=(nt_max,),
        in_specs=[
            smem_tile(lambda i, te, nt: (i, 0, 0)),
            smem_tile(lambda i, te, nt: (jnp.minimum(i + 1, nt_max - 1), 0, 0)),
            pl.BlockSpec(memory_space=pl.ANY),
            pl.BlockSpec((None, 1, d, f), lambda i, te, nt: (layer, te[i], 0, 0)),
            pl.BlockSpec((None, 1, d, f), lambda i, te, nt: (layer, te[i], 0, 0)),
            pl.BlockSpec((None, 1, f, d), lambda i, te, nt: (layer, te[i], 0, 0)),
        ],
        out_specs=pl.BlockSpec((tm * pitch, LANES), lambda i, te, nt: (i, 0)),
        scratch_shapes=[
            pltpu.VMEM((2, tm * pitch, LANES), F32),
            pltpu.VMEM((d, f), BF16),
            pltpu.VMEM((d, f), BF16),
            pltpu.VMEM((f, d), BF16),
            pltpu.SemaphoreType.DMA((2,)),
        ],
    )
    return pl.pallas_call(
        body,
        grid_spec=grid_spec,
        out_shape=jax.ShapeDtypeStruct((nt_max * tm * pitch, LANES), F32),
        compiler_params=_params("arbitrary"),
        name="experts",
    )(te, n_tiles, src3, src3, h2, w1, w3, w2)


def _combine_body(pos_ref, nxt_ref, x_ref, w_ref, g_ref, ys_hbm, *rest, tm, final):
    if final:
        h_ref, ybuf, sem = rest
        x2_ref = None
    else:
        x2_ref, h_ref, ybuf, sem = rest
    i = pl.program_id(0)
    n = pl.num_programs(0)
    slot = lax.rem(i, 2)
    d = x_ref.shape[1]
    rows, pitch = _slab_rows(d), _slab_pitch(d)

    def row_in(src, k, r, s):
        return pltpu.make_async_copy(ys_hbm.at[pl.ds(src * pitch, rows)], ybuf.at[s, k, pl.ds(r * pitch, rows)],
                                     sem.at[s])

    def gather_start(idx_ref, s):
        for k in range(TOP_K):
            for r in range(tm):
                row_in(idx_ref[0, 0, k * tm + r], k, r, s).start()

    def gather_wait(s):
        for k in range(TOP_K):
            for r in range(tm):
                row_in(0, k, r, s).wait()

    @pl.when(i == 0)
    def _():
        gather_start(pos_ref, 0)

    gather_wait(slot)

    @pl.when(i + 1 < n)
    def _():
        gather_start(nxt_ref, 1 - slot)

    w = w_ref[...]
    x2 = x_ref[...] + w[:, 0:1] * _slab_load(ybuf, (slot, 0), tm, d) + w[:, 1:2] * _slab_load(ybuf, (slot, 1), tm, d)
    if x2_ref is not None:
        x2_ref[...] = x2
    h_ref[...] = _rms(x2, g_ref[...]).astype(h_ref.dtype)


def moe_combine(x1, ys, pos, rw, g, *, final, tm=128):
    t, d = x1.shape
    nt = t // tm
    w = rw[0:TOP_K].T
    pos3 = pos.reshape(TOP_K, nt, tm).transpose(1, 0, 2).reshape(nt, 1, TOP_K * tm)
    row = lambda i: (i, 0)
    smem_tile = lambda imap: pl.BlockSpec((1, 1, TOP_K * tm), imap, memory_space=pltpu.SMEM)
    in_specs = [
        smem_tile(lambda i: (i, 0, 0)),
        smem_tile(lambda i: (jnp.minimum(i + 1, nt - 1), 0, 0)),
        pl.BlockSpec((tm, d), row),
        pl.BlockSpec((tm, TOP_K), row),
        pl.BlockSpec((1, d), lambda i: (0, 0)),
        pl.BlockSpec(memory_space=pl.ANY),
    ]
    scratch = [pltpu.VMEM((2, TOP_K, tm * _slab_pitch(d), LANES), F32), pltpu.SemaphoreType.DMA((2,))]
    args = (pos3, pos3, x1, w, g.reshape(1, d), ys)
    body = functools.partial(_combine_body, tm=tm, final=final)
    if final:
        return pl.pallas_call(
            body, grid=(nt,), in_specs=in_specs,
            out_specs=pl.BlockSpec((tm, d), row),
            out_shape=jax.ShapeDtypeStruct((t, d), F32),
            scratch_shapes=scratch,
            compiler_params=_params("arbitrary"), name="combine_final",
        )(*args)
    return pl.pallas_call(
        body, grid=(nt,), in_specs=in_specs,
        out_specs=[pl.BlockSpec((tm, d), row), pl.BlockSpec((tm, d), row)],
        out_shape=[jax.ShapeDtypeStruct((t, d), F32), jax.ShapeDtypeStruct((t, d), BF16)],
        scratch_shapes=scratch,
        compiler_params=_params("arbitrary"), name="combine",
    )(*args)


def _log_sigmoid(x):
    return jnp.minimum(x, 0.0) - jnp.log(1.0 + jnp.exp(-jnp.abs(x)))


def _gates_body(h_ref, wch_ref, wcl_ref, bc_ref, gc_ref, gr_ref, *, nh):
    h = h_ref[...]
    col = (jnp.dot(h, wch_ref[...], preferred_element_type=F32)
           + jnp.dot(h, wcl_ref[...], preferred_element_type=F32)) + bc_ref[...]
    lane = lax.broadcasted_iota(jnp.int32, col.shape, 1)
    gates = jnp.where(lane < nh, col, _log_sigmoid(col))
    gc_ref[...] = gates
    gr_ref[...] = gates.T[:2 * nh, :]


def mlstm_gates(h, w_if, b_i, b_f, *, tm=512):
    t, d = h.shape
    nh = b_i.shape[0]
    wc = jnp.zeros((d, LANES), F32).at[:, :2 * nh].set(w_if)
    wch, wcl = _split2(wc)
    bc = jnp.zeros((1, LANES), F32).at[0, :2 * nh].set(jnp.concatenate([b_i, b_f]))
    const = lambda i: (0, 0)
    return pl.pallas_call(
        functools.partial(_gates_body, nh=nh),
        grid=(t // tm,),
        in_specs=[
            pl.BlockSpec((tm, d), lambda i: (i, 0)),
            pl.BlockSpec((d, LANES), const), pl.BlockSpec((d, LANES), const),
            pl.BlockSpec((1, LANES), const),
        ],
        out_specs=[pl.BlockSpec((tm, LANES), lambda i: (i, 0)), pl.BlockSpec((2 * nh, tm), lambda i: (0, i))],
        out_shape=[jax.ShapeDtypeStruct((t, LANES), F32), jax.ShapeDtypeStruct((2 * nh, t), F32)],
        compiler_params=_params("parallel"),
        name="mlstm_gates",
    )(h, wch, wcl, bc)


def _mlstm_body(q_ref, k_ref, v_ref, o_ref, gc_ref, gr_ref, gh_ref, tril_ref, triu_ref, out_ref,
                c_ref, n_ref, m_ref, *, scale, nchunk, dv):
    L = ML_CHUNK
    row = lax.broadcasted_iota(jnp.int32, (L, L), 0)
    col = lax.broadcasted_iota(jnp.int32, (L, L), 1)
    tmask = col <= row
    c_ref[...] = jnp.zeros_like(c_ref)
    n_ref[...] = jnp.zeros_like(n_ref)
    m_ref[...] = jnp.zeros_like(m_ref)

    def cumsum_exact(parts, tri, left):
        acc = None
        for p in parts:
            d = (jnp.dot(tri, p, preferred_element_type=F32) if left
                 else jnp.dot(p, tri, preferred_element_type=F32))
            acc = d if acc is None else acc + d
        return acc

    def step(c, carry):
        c0 = pl.multiple_of(c * L, L)
        qc = (q_ref[pl.ds(c0, L), :].astype(F32) * scale).astype(BF16)
        kc = k_ref[pl.ds(c0, L), :]
        vc = v_ref[pl.ds(c0, L), :]
        gc = gc_ref[0, pl.ds(c0, L), :]
        gr = gr_ref[0, :, pl.ds(c0, L)]
        i_col, f_col = gc[:, 0:1], gc[:, 1:2]
        i_row, f_row = gr[0:1, :], gr[1:2, :]
        m_prev = m_ref[...]

        b_col = cumsum_exact(_split3(jnp.broadcast_to(f_col, (L, L))), tril_ref[...], True)
        b_row = cumsum_exact(_split3(jnp.broadcast_to(f_row, (SUBLANES, L))), triu_ref[...], False)[0:1, :]
        bt = b_col[:, 0:1]
        b_last = b_col[L - 1:L, 0:1]

        d_mat = jnp.where(tmask, b_col + (i_row - b_row), -1e30)
        m_t = jnp.maximum(bt + m_prev, jnp.max(d_mat, axis=1, keepdims=True))
        e = jnp.where(tmask, jnp.exp(d_mat - m_t), 0.0)
        w = lax.dot_general(qc, kc, NT_DIMS, preferred_element_type=F32) * e
        inter = jnp.exp(bt + m_prev - m_t)
        c_prev = c_ref[...]
        n_prev = n_ref[...]
        num = inter * jnp.dot(qc, c_prev.astype(BF16), preferred_element_type=F32) \
            + jnp.dot(w.astype(BF16), vc, preferred_element_type=F32)
        qn = jnp.sum(qc.astype(F32) * n_prev, axis=1, keepdims=True)
        den = inter * qn + jnp.sum(w, axis=1, keepdims=True)
        h = num / jnp.maximum(jnp.abs(den), jnp.exp(-m_t))

        hs = h * lax.rsqrt(jnp.mean(h * h, axis=1, keepdims=True) + EPS) * gh_ref[...]
        og = o_ref[pl.ds(c0, L), :].astype(F32)
        out_ref[pl.ds(c0, L), :] = (hs / (1.0 + jnp.exp(-og))).astype(out_ref.dtype)

        key_log = b_last - bt + i_col
        m_new = jnp.maximum(b_last + m_prev, jnp.max(key_log, axis=0, keepdims=True))
        decay = jnp.exp(key_log - m_new)
        keep = jnp.exp(b_last + m_prev - m_new)
        kd = kc.astype(F32) * decay
        c_ref[...] = keep * c_prev + jnp.dot(kd.T.astype(BF16), vc, preferred_element_type=F32)
        n_ref[...] = keep * n_prev + jnp.sum(kd, axis=0, keepdims=True)
        m_ref[...] = m_new
        return carry

    lax.fori_loop(0, nchunk, step, 0)


def mlstm(qkvo, gcol, grow, g_head, *, batch, seq, heads, dqk, dv):
    t = batch * seq
    nh = heads
    gc3 = jnp.stack([gcol[:, :nh].T, gcol[:, nh:2 * nh].T], axis=-1)
    gr3 = jnp.stack([grow[:nh], grow[nh:2 * nh]], axis=1)
    L = ML_CHUNK
    idx = jnp.arange(L)
    tril = (idx[None, :] <= idx[:, None]).astype(BF16)
    triu = (idx[:, None] <= idx[None, :]).astype(BF16)
    kq = nh * dqk // dqk
    del kq
    body = functools.partial(_mlstm_body, scale=dqk ** -0.5, nchunk=seq // L, dv=dv)
    v_blk0 = 2 * nh * dqk // dv
    o_blk0 = v_blk0 + nh
    const = lambda b, h: (0, 0)
    return pl.pallas_call(
        body,
        grid=(batch, nh),
        in_specs=[
            pl.BlockSpec((seq, dqk), lambda b, h: (b, h)),
            pl.BlockSpec((seq, dqk), lambda b, h: (b, nh + h)),
            pl.BlockSpec((seq, dv), lambda b, h: (b, v_blk0 + h)),
            pl.BlockSpec((seq, dv), lambda b, h: (b, o_blk0 + h)),
            pl.BlockSpec((1, seq, 2), lambda b, h: (h, b, 0)),
            pl.BlockSpec((1, 2, seq), lambda b, h: (h, 0, b)),
            pl.BlockSpec((1, dv), lambda b, h: (0, h)),
            pl.BlockSpec((L, L), const),
            pl.BlockSpec((L, L), const),
        ],
        out_specs=pl.BlockSpec((seq, dv), lambda b, h: (b, h)),
        out_shape=jax.ShapeDtypeStruct((t, nh * dv), BF16),
        scratch_shapes=[pltpu.VMEM((dqk, dv), F32), pltpu.VMEM((1, dqk), F32), pltpu.VMEM((1, 1), F32)],
        compiler_params=_params("parallel", "parallel"),
        name="mlstm",
    )(qkvo, qkvo, qkvo, qkvo, gc3, gr3, g_head.reshape(1, nh * dv), tril, triu)


def _moe(o, w_out, x, norm_ffn, w_group, b_group, w_expert, b_expert, w1, w3, w2, layer, g_next, *, final,
         tm=256):
    x1, h2, ri, rw, cnt = proj_route(o, w_out.astype(BF16), x, norm_ffn, w_group, b_group, w_expert, b_expert)
    pos, src3, te, n_tiles, nt_max = _route_tables(ri, cnt, tm=tm, n_tok=x.shape[0])
    ys = routed_experts(h2, src3, te, n_tiles, w1, w3, w2, layer, tm=tm, nt_max=nt_max)
    return moe_combine(x1, ys, pos, rw, g_next, final=final)


def kernel(x, norm_mix, norm_ffn, norm_final, sb_w_in, sb_w_out, ml_w_in, ml_b_i, ml_b_f, ml_g_head, ml_w_out,
           moe_w_group, moe_b_group, moe_w_expert, moe_b_expert, moe_w1, moe_w3, moe_w2):
    batch, seq, d = x.shape
    xt = x.reshape(batch * seq, d)

    h = rmsnorm_bf16(xt, norm_mix[0])
    qkv = matmul_bf16(h, sb_w_in, 3 * d)
    o = sb_attention(qkv, batch=batch, seq=seq, heads=SB_HEADS, dh=d // SB_HEADS)
    xt, h = _moe(o, sb_w_out[0], xt, norm_ffn[0], moe_w_group[0], moe_b_group[0], moe_w_expert[0],
                 moe_b_expert[0], moe_w1, moe_w3, moe_w2, 0, norm_mix[1], final=False)

    dqk = (d // 2) // ML_HEADS
    dv = d // ML_HEADS
    n_main = 2 * ML_HEADS * dqk + 2 * ML_HEADS * dv
    qkvo = matmul_bf16(h, ml_w_in, n_main)
    gcol, grow = mlstm_gates(h, ml_w_in[0][:, n_main:], ml_b_i[0], ml_b_f[0])
    o = mlstm(qkvo, gcol, grow, ml_g_head[0], batch=batch, seq=seq, heads=ML_HEADS, dqk=dqk, dv=dv)
    out = _moe(o, ml_w_out[0], xt, norm_ffn[1], moe_w_group[1], moe_b_group[1], moe_w_expert[1],
               moe_b_expert[1], moe_w1, moe_w3, moe_w2, 1, norm_final, final=True)
    return out.reshape(batch, seq, d)
```

```python
import functools

import jax
import jax.numpy as jnp
from jax import lax
from jax.experimental import pallas as pl
from jax.experimental.pallas import tpu as pltpu

F32 = jnp.float32
BF16 = jnp.bfloat16
EPS = 1e-6

V7X_VMEM_BYTES = 64 * 1024 * 1024
VMEM_LIMIT_BYTES = V7X_VMEM_BYTES - 8 * 1024 * 1024
LANES = 128
SUBLANES = 8

SB_HEADS = 16
ML_HEADS = 8
N_GROUPS = 4
EXPERTS_PER_GROUP = 8
N_EXPERTS = N_GROUPS * EXPERTS_PER_GROUP
TOP_K = 2
ML_CHUNK = 128
SB_BLOCK = 128
ROUTER_ROWS = 128
EXP_F32_UNDERFLOW = -104.0
GATHER_SLOTS = 3
DMA_PRIORITIES = 2

NT_DIMS = (((1,), (1,)), ((), ()))


def _params(*sem):
    return pltpu.CompilerParams(dimension_semantics=sem, vmem_limit_bytes=VMEM_LIMIT_BYTES)


def _split2(x):
    hi = x.astype(BF16)
    lo = (x - hi.astype(F32)).astype(BF16)
    return hi, lo


def _split3(x):
    hi = x.astype(BF16)
    r = x - hi.astype(F32)
    mid = r.astype(BF16)
    lo = (r - mid.astype(F32)).astype(BF16)
    return hi, mid, lo


def _rms(x, g):
    return x * lax.rsqrt(jnp.mean(x * x, axis=-1, keepdims=True) + EPS) * g


def _slab_rows(d):
    return d // LANES


def _slab_pitch(d):
    return _slab_rows(d) + 1


def _slab_store(ref, x, n):
    d = x.shape[1]
    pitch = _slab_pitch(d)
    for c in range(_slab_rows(d)):
        ref[pl.ds(c, n, stride=pitch), :] = x[:, c * LANES:(c + 1) * LANES]
    ref[pl.ds(_slab_rows(d), n, stride=pitch), :] = jnp.zeros((n, LANES), x.dtype)


def _slab_load(ref, lead, n, d):
    pitch = _slab_pitch(d)
    return jnp.concatenate([ref[lead + (pl.ds(c, n, stride=pitch), slice(None))] for c in range(_slab_rows(d))],
                           axis=1)


def _rmsnorm_body(x_ref, g_ref, h_ref):
    h_ref[...] = _rms(x_ref[...], g_ref[...]).astype(h_ref.dtype)


def rmsnorm_bf16(x, g, *, tm=512):
    t, d = x.shape
    return pl.pallas_call(
        _rmsnorm_body,
        grid=(t // tm,),
        in_specs=[pl.BlockSpec((tm, d), lambda i: (i, 0)), pl.BlockSpec((1, d), lambda i: (0, 0))],
        out_specs=pl.BlockSpec((tm, d), lambda i: (i, 0)),
        out_shape=jax.ShapeDtypeStruct((t, d), BF16),
        compiler_params=_params("parallel"),
        name="rmsnorm",
    )(x, g.reshape(1, d))


def _mm_body(h_ref, w_ref, o_ref, wb_ref):
    @pl.when(pl.program_id(1) == 0)
    def _():
        wb_ref[...] = w_ref[...].astype(BF16)

    o_ref[...] = jnp.dot(h_ref[...], wb_ref[...], preferred_element_type=F32).astype(o_ref.dtype)


def matmul_bf16(h, w, n, *, tm=1024, tn=1024):
    t, k = h.shape
    return pl.pallas_call(
        _mm_body,
        grid=(n // tn, t // tm),
        in_specs=[pl.BlockSpec((tm, k), lambda j, i: (i, 0)), pl.BlockSpec((None, k, tn), lambda j, i: (0, 0, j))],
        out_specs=pl.BlockSpec((tm, tn), lambda j, i: (i, j)),
        out_shape=jax.ShapeDtypeStruct((t, n), BF16),
        scratch_shapes=[pltpu.VMEM((k, tn), BF16)],
        compiler_params=_params("parallel", "arbitrary"),
        name="proj_in",
    )(h, w)


def _sb_attn_body(q_ref, k_ref, v_ref, cm_ref, o_ref, *, scale, nblk, hg, kc):
    blk = SB_BLOCK
    dh = SB_BLOCK
    wid = kc * blk
    rel = lax.broadcasted_iota(jnp.int32, (blk, wid), 1) - lax.broadcasted_iota(jnp.int32, (blk, wid), 0)

    def chunk(qbs, q0, c, rs, accs, masked):
        k0 = pl.multiple_of(c * wid, wid)
        if masked:
            valid = rel < (q0 - k0)
        z = jnp.concatenate(
            [lax.dot_general(qbs[h], k_ref[pl.ds(k0, wid), h * dh:(h + 1) * dh], NT_DIMS,
                             preferred_element_type=F32) for h in range(hg)], axis=1)
        sp = jnp.maximum(z, 0.0) + jnp.log(1.0 + jnp.exp(-jnp.abs(z)))
        log_beta = z - sp
        if masked:
            valid_all = jnp.concatenate([valid] * hg, axis=1)
            sp = jnp.where(valid_all, sp, 0.0)
        hi, lo = _split2(sp)
        nb = hg * kc
        x = jnp.concatenate(
            [jnp.concatenate([hi[:, j * blk:(j + 1) * blk], lo[:, j * blk:(j + 1) * blk]], axis=1)
             for j in range(nb)], axis=0)
        cs = jnp.dot(x, cm_ref[...], preferred_element_type=F32)
        new_rs = []
        suffix = [None] * nb
        for h in range(hg):
            r = rs[h]
            for j in reversed(range(kc)):
                csj = cs[(h * kc + j) * blk:(h * kc + j + 1) * blk]
                suffix[h * kc + j] = csj[:, :blk] + r
                r = r + csj[:, blk:]
            new_rs.append(r)
        a = jnp.exp(log_beta + jnp.concatenate(suffix, axis=1))
        if masked:
            a = jnp.where(valid_all, a, 0.0)
        a = a.astype(BF16)
        new_accs = [accs[h] + jnp.dot(a[:, h * wid:(h + 1) * wid], v_ref[pl.ds(k0, wid), h * dh:(h + 1) * dh],
                                      preferred_element_type=F32) for h in range(hg)]
        return tuple(new_rs), tuple(new_accs)

    def q_step(qi, carry):
        q0 = pl.multiple_of(qi * blk, blk)
        qbs = [(q_ref[pl.ds(q0, blk), h * dh:(h + 1) * dh].astype(F32) * scale).astype(BF16) for h in range(hg)]
        zeros = tuple(jnp.zeros((blk, blk), F32) for _ in range(hg))
        top = qi // kc
        rs, accs = chunk(qbs, q0, top, zeros, zeros, True)

        def live(rs):
            return jnp.max(functools.reduce(jnp.maximum, rs)) > EXP_F32_UNDERFLOW

        def k_cond(c):
            return (c[0] < top) & c[1]

        def k_step(c):
            rs, accs = chunk(qbs, q0, top - 1 - c[0], c[2], c[3], False)
            return c[0] + 1, live(rs), rs, accs

        accs = lax.while_loop(k_cond, k_step, (jnp.int32(0), live(rs), rs, accs))[3]
        for h in range(hg):
            o_ref[pl.ds(q0, blk), h * dh:(h + 1) * dh] = accs[h].astype(o_ref.dtype)
        return carry

    lax.fori_loop(0, nblk, q_step, 0)


def _cumsum_matrix(blk):
    j = jnp.arange(2 * blk)[:, None] % blk
    c = jnp.arange(2 * blk)[None, :]
    return jnp.where((c >= blk) | (j > c), -1.0, 0.0).astype(BF16)


def sb_attention(qkv, *, batch, seq, heads, dh, hg=8, kc=2):
    assert dh == SB_BLOCK
    nblk = seq // SB_BLOCK
    assert nblk % kc == 0 and heads % hg == 0
    body = functools.partial(_sb_attn_body, scale=dh ** -0.5, nblk=nblk, hg=hg, kc=kc)
    cm = _cumsum_matrix(SB_BLOCK)
    ng = heads // hg
    return pl.pallas_call(
        body,
        grid=(batch, ng),
        in_specs=[
            pl.BlockSpec((seq, hg * dh), lambda b, h: (b, h)),
            pl.BlockSpec((seq, hg * dh), lambda b, h: (b, ng + h)),
            pl.BlockSpec((seq, hg * dh), lambda b, h: (b, 2 * ng + h)),
            pl.BlockSpec((2 * SB_BLOCK, 2 * SB_BLOCK), lambda b, h: (0, 0)),
        ],
        out_specs=pl.BlockSpec((seq, hg * dh), lambda b, h: (b, h)),
        out_shape=jax.ShapeDtypeStruct((batch * seq, heads * dh), BF16),
        compiler_params=_params("parallel", "parallel"),
        name="sb_attn",
    )(qkv, qkv, qkv, cm)


def _proj_route_body(o_ref, w_ref, x_ref, g_ref, wrh_ref, wrl_ref, br_ref, tri_ref,
                     x1_ref, h2_ref, ri_ref, rw_ref, cnt_ref, base_ref, *, tm):
    i = pl.program_id(0)

    @pl.when(i == 0)
    def _():
        base_ref[...] = jnp.zeros_like(base_ref)

    x1 = x_ref[...] + jnp.dot(o_ref[...], w_ref[...], preferred_element_type=F32)
    x1_ref[...] = x1
    h2 = _rms(x1, g_ref[...])
    _slab_store(h2_ref, h2, tm)

    hh, hl = _split2(h2)
    wrh = wrh_ref[...]
    lg = (lax.dot_general(wrh, hh, NT_DIMS, preferred_element_type=F32)
          + lax.dot_general(wrh, hl, NT_DIMS, preferred_element_type=F32)
          + lax.dot_general(wrl_ref[...], hh, NT_DIMS, preferred_element_type=F32)) + br_ref[...]

    r8 = lax.broadcasted_iota(jnp.int32, (SUBLANES, tm), 0)
    neg_inf = jnp.float32(-jnp.inf)
    gl = jnp.where(r8 < N_GROUPS, lg[N_EXPERTS:N_EXPERTS + SUBLANES, :], neg_inf)
    gmax = jnp.max(gl, axis=0, keepdims=True)
    gsel = jnp.min(jnp.where(gl == gmax, r8, SUBLANES), axis=0, keepdims=True)
    p_sel = 1.0 / jnp.sum(jnp.exp(gl - gmax), axis=0, keepdims=True)

    es = lg[0:EXPERTS_PER_GROUP, :]
    for g in range(1, N_GROUPS):
        es = jnp.where(gsel == g, lg[g * EXPERTS_PER_GROUP:(g + 1) * EXPERTS_PER_GROUP, :], es)
    m1 = jnp.max(es, axis=0, keepdims=True)
    i1 = jnp.min(jnp.where(es == m1, r8, SUBLANES), axis=0, keepdims=True)
    es2 = jnp.where(r8 == i1, neg_inf, es)
    m2 = jnp.max(es2, axis=0, keepdims=True)
    i2 = jnp.min(jnp.where(es2 == m2, r8, SUBLANES), axis=0, keepdims=True)
    t = jnp.exp(m2 - m1)
    w1 = p_sel / (1.0 + t)
    w2 = w1 * t
    e1 = gsel * EXPERTS_PER_GROUP + i1
    e2 = gsel * EXPERTS_PER_GROUP + i2

    r32 = lax.broadcasted_iota(jnp.int32, (N_EXPERTS, tm), 0)
    hit1 = r32 == e1
    hit2 = r32 == e2
    onehot = (jnp.where(hit1, 1.0, 0.0) + jnp.where(hit2, 1.0, 0.0)).astype(BF16)
    cs = jnp.dot(onehot, tri_ref[...], preferred_element_type=F32)
    base = base_ref[...]
    before = cs[:, :tm] + base
    rank1 = jnp.sum(jnp.where(hit1, before, 0.0), axis=0, keepdims=True)
    rank2 = jnp.sum(jnp.where(hit2, before, 0.0), axis=0, keepdims=True)
    tot = cs[:, tm:]
    new_base = base + jnp.concatenate([tot] * (tm // LANES), axis=1)
    base_ref[...] = new_base
    cnt_ref[...] = new_base[:, :LANES]

    zi = jnp.zeros((SUBLANES, tm), jnp.int32)
    ri = jnp.where(r8 == 0, e1, zi)
    ri = jnp.where(r8 == 1, e2, ri)
    ri = jnp.where(r8 == 2, rank1.astype(jnp.int32), ri)
    ri = jnp.where(r8 == 3, rank2.astype(jnp.int32), ri)
    ri_ref[...] = ri
    zf = jnp.zeros((SUBLANES, tm), F32)
    rw = jnp.where(r8 == 0, w1, zf)
    rw = jnp.where(r8 == 1, w2, rw)
    rw_ref[...] = rw


def proj_route(o, w_out, x, g, w_group, b_group, w_expert, b_expert, *, tm=256):
    t, d = x.shape
    k = o.shape[1]
    wr = jnp.zeros((ROUTER_ROWS, d), F32)
    wr = wr.at[:N_EXPERTS].set(w_expert.T).at[N_EXPERTS:N_EXPERTS + N_GROUPS].set(w_group.T)
    wrh, wrl = _split2(wr)
    br = jnp.zeros((ROUTER_ROWS,), F32).at[:N_EXPERTS].set(b_expert).at[N_EXPERTS:N_EXPERTS + N_GROUPS].set(b_group)
    br = jnp.broadcast_to(br[:, None], (ROUTER_ROWS, tm))
    tp = jnp.arange(tm)[:, None]
    tri = jnp.concatenate([(tp < jnp.arange(tm)[None, :]).astype(BF16), jnp.ones((tm, LANES), BF16)], axis=1)
    body = functools.partial(_proj_route_body, tm=tm)
    row = lambda i: (i, 0)
    const = lambda i: (0, 0)
    return pl.pallas_call(
        body,
        grid=(t // tm,),
        in_specs=[
            pl.BlockSpec((tm, k), row),
            pl.BlockSpec((k, d), const),
            pl.BlockSpec((tm, d), row),
            pl.BlockSpec((1, d), const),
            pl.BlockSpec((ROUTER_ROWS, d), const),
            pl.BlockSpec((ROUTER_ROWS, d), const),
            pl.BlockSpec((ROUTER_ROWS, tm), const),
            pl.BlockSpec((tm, tm + LANES), const),
        ],
        out_specs=[
            pl.BlockSpec((tm, d), row),
            pl.BlockSpec((tm * _slab_pitch(d), LANES), row),
            pl.BlockSpec((SUBLANES, tm), lambda i: (0, i)),
            pl.BlockSpec((SUBLANES, tm), lambda i: (0, i)),
            pl.BlockSpec((N_EXPERTS, LANES), const),
        ],
        out_shape=[
            jax.ShapeDtypeStruct((t, d), F32),
            jax.ShapeDtypeStruct((t * _slab_pitch(d), LANES), F32),
            jax.ShapeDtypeStruct((SUBLANES, t), jnp.int32),
            jax.ShapeDtypeStruct((SUBLANES, t), F32),
            jax.ShapeDtypeStruct((N_EXPERTS, LANES), F32),
        ],
        scratch_shapes=[pltpu.VMEM((N_EXPERTS, tm), F32)],
        compiler_params=_params("arbitrary"),
        name="proj_route",
    )(o, w_out, x, g.reshape(1, d), wrh, wrl, br, tri)


def _experts_body(te_ref, nt_ref, src_ref, nx1_ref, nx2_ref, h_hbm, w1_ref, w3_ref, w2_ref, y_ref,
                  xbuf, w1b, w3b, w2b, gsem, *, tm, n_tok, d):
    i = pl.program_id(0)
    n_tiles = nt_ref[0]
    slot = lax.rem(i, GATHER_SLOTS)
    rows, pitch = _slab_rows(d), _slab_pitch(d)

    def row_in(tok, r, s):
        return pltpu.make_async_copy(h_hbm.at[pl.ds(tok * pitch, rows)], xbuf.at[s, pl.ds(r * pitch, rows)],
                                     gsem.at[s])

    def gather_start(idx_ref, s):
        for r in range(tm):
            p = idx_ref[0, 0, r]
            row_in(jnp.where(p < 0, 0, p & (n_tok - 1)), r, s).start(priority=r % DMA_PRIORITIES)

    def gather_wait(s):
        for r in range(tm):
            row_in(0, r, s).wait()

    @pl.when(i == 0)
    def _():
        gather_start(src_ref, 0)
        gather_start(nx1_ref, 1)

    @pl.when(i < n_tiles + (GATHER_SLOTS - 1))
    def _():
        gather_wait(slot)

    @pl.when(i < n_tiles)
    def _():
        @pl.when((i == 0) | (te_ref[i] != te_ref[jnp.maximum(i - 1, 0)]))
        def _():
            w1b[...] = w1_ref[0].astype(BF16)
            w3b[...] = w3_ref[0].astype(BF16)
            w2b[...] = w2_ref[0].astype(BF16)

        xb = _slab_load(xbuf, (slot,), tm, d).astype(BF16)
        gather_start(nx2_ref, lax.rem(i + 2, GATHER_SLOTS))
        a = jnp.dot(xb, w1b[...], preferred_element_type=F32)
        b = jnp.dot(xb, w3b[...], preferred_element_type=F32)
        hid = (a * (1.0 / (1.0 + jnp.exp(-a))) * b).astype(BF16)
        _slab_store(y_ref, jnp.dot(hid, w2b[...], preferred_element_type=F32), tm)

    @pl.when(i >= n_tiles)
    def _():
        y_ref[...] = jnp.zeros_like(y_ref)


def _route_tables(ri, cnt, *, tm, n_tok):
    nt_max = TOP_K * n_tok // tm + N_EXPERTS - 1 + (GATHER_SLOTS - 1)
    counts = cnt[:, 0].astype(jnp.int32)
    padded = ((counts + tm - 1) // tm) * tm
    ends = jnp.cumsum(padded)
    offs = ends - padded
    eid = jnp.arange(N_EXPERTS, dtype=jnp.int32)
    sel = ri[0:TOP_K, :, None] == eid
    pos = jnp.sum(jnp.where(sel, offs, 0), axis=-1) + ri[TOP_K:2 * TOP_K]
    pair = jnp.arange(TOP_K * n_tok, dtype=jnp.int32)
    src = jnp.full((nt_max * tm,), -1, jnp.int32).at[pos.reshape(-1)].set(pair)
    n_tiles = (ends[-1] // tm).astype(jnp.int32)
    tile_start = jnp.arange(nt_max, dtype=jnp.int32) * tm
    te = jnp.sum((tile_start[:, None] >= ends[None, :]).astype(jnp.int32), axis=1)
    last = jnp.sum((tile_start[jnp.maximum(n_tiles - 1, 0)] >= ends).astype(jnp.int32))
    te = jnp.clip(jnp.minimum(te, last), 0, N_EXPERTS - 1)
    return pos, src.reshape(nt_max, 1, tm), te, n_tiles.reshape(1), nt_max


def routed_experts(h2, src3, te, n_tiles, w1, w3, w2, layer, *, tm, nt_max):
    d, f = w1.shape[-2:]
    pitch = _slab_pitch(d)
    t = h2.shape[0] // pitch
    body = functools.partial(_experts_body, tm=tm, n_tok=t, d=d)
    smem_tile = lambda ahead: pl.BlockSpec((1, 1, tm), lambda i, te, nt: (jnp.minimum(i + ahead, nt_max - 1), 0, 0),
                                           memory_space=pltpu.SMEM)
    grid_spec = pltpu.PrefetchScalarGridSpec(
        num_scalar_prefetch=2,
        grid=(nt_max,),
        in_specs=[
            smem_tile(0), smem_tile(1), smem_tile(2),
            pl.BlockSpec(memory_space=pl.ANY),
            pl.BlockSpec((None, 1, d, f), lambda i, te, nt: (layer, te[i], 0, 0)),
            pl.BlockSpec((None, 1, d, f), lambda i, te, nt: (layer, te[i], 0, 0)),
            pl.BlockSpec((None, 1, f, d), lambda i, te, nt: (layer, te[i], 0, 0)),
        ],
        out_specs=pl.BlockSpec((tm * pitch, LANES), lambda i, te, nt: (i, 0)),
        scratch_shapes=[
            pltpu.VMEM((GATHER_SLOTS, tm * pitch, LANES), F32),
            pltpu.VMEM((d, f), BF16),
            pltpu.VMEM((d, f), BF16),
            pltpu.VMEM((f, d), BF16),
            pltpu.SemaphoreType.DMA((GATHER_SLOTS,)),
        ],
    )
    return pl.pallas_call(
        body,
        grid_spec=grid_spec,
        out_shape=jax.ShapeDtypeStruct((nt_max * tm * pitch, LANES), F32),
        compiler_params=_params("arbitrary"),
        name="experts",
    )(te, n_tiles, src3, src3, src3, h2, w1, w3, w2)


def _combine_body(pos_ref, nxt_ref, x_ref, w_ref, g_ref, ys_hbm, *rest, tm, final):
    if final:
        h_ref, ybuf, sem = rest
        x2_ref = None
    else:
        x2_ref, h_ref, ybuf, sem = rest
    i = pl.program_id(0)
    n = pl.num_programs(0)
    slot = lax.rem(i, 2)
    d = x_ref.shape[1]
    rows, pitch = _slab_rows(d), _slab_pitch(d)

    def row_in(src, k, r, s):
        return pltpu.make_async_copy(ys_hbm.at[pl.ds(src * pitch, rows)], ybuf.at[s, k, pl.ds(r * pitch, rows)],
                                     sem.at[s])

    def gather_start(idx_ref, s):
        for k in range(TOP_K):
            for r in range(tm):
                row_in(idx_ref[0, 0, k * tm + r], k, r, s).start(priority=r % DMA_PRIORITIES)

    def gather_wait(s):
        for k in range(TOP_K):
            for r in range(tm):
                row_in(0, k, r, s).wait()

    @pl.when(i == 0)
    def _():
        gather_start(pos_ref, 0)

    gather_wait(slot)

    @pl.when(i + 1 < n)
    def _():
        gather_start(nxt_ref, 1 - slot)

    w = w_ref[...]
    x2 = x_ref[...] + w[:, 0:1] * _slab_load(ybuf, (slot, 0), tm, d) + w[:, 1:2] * _slab_load(ybuf, (slot, 1), tm, d)
    if x2_ref is not None:
        x2_ref[...] = x2
    h_ref[...] = _rms(x2, g_ref[...]).astype(h_ref.dtype)


def moe_combine(x1, ys, pos, rw, g, *, final, tm=128):
    t, d = x1.shape
    nt = t // tm
    w = rw[0:TOP_K].T
    pos3 = pos.reshape(TOP_K, nt, tm).transpose(1, 0, 2).reshape(nt, 1, TOP_K * tm)
    row = lambda i: (i, 0)
    smem_tile = lambda imap: pl.BlockSpec((1, 1, TOP_K * tm), imap, memory_space=pltpu.SMEM)
    in_specs = [
        smem_tile(lambda i: (i, 0, 0)),
        smem_tile(lambda i: (jnp.minimum(i + 1, nt - 1), 0, 0)),
        pl.BlockSpec((tm, d), row),
        pl.BlockSpec((tm, TOP_K), row),
        pl.BlockSpec((1, d), lambda i: (0, 0)),
        pl.BlockSpec(memory_space=pl.ANY),
    ]
    scratch = [pltpu.VMEM((2, TOP_K, tm * _slab_pitch(d), LANES), F32), pltpu.SemaphoreType.DMA((2,))]
    args = (pos3, pos3, x1, w, g.reshape(1, d), ys)
    body = functools.partial(_combine_body, tm=tm, final=final)
    if final:
        return pl.pallas_call(
            body, grid=(nt,), in_specs=in_specs,
            out_specs=pl.BlockSpec((tm, d), row),
            out_shape=jax.ShapeDtypeStruct((t, d), F32),
            scratch_shapes=scratch,
            compiler_params=_params("arbitrary"), name="combine_final",
        )(*args)
    return pl.pallas_call(
        body, grid=(nt,), in_specs=in_specs,
        out_specs=[pl.BlockSpec((tm, d), row), pl.BlockSpec((tm, d), row)],
        out_shape=[jax.ShapeDtypeStruct((t, d), F32), jax.ShapeDtypeStruct((t, d), BF16)],
        scratch_shapes=scratch,
        compiler_params=_params("arbitrary"), name="combine",
    )(*args)


def _log_sigmoid(x):
    return jnp.minimum(x, 0.0) - jnp.log(1.0 + jnp.exp(-jnp.abs(x)))


def _gates_body(h_ref, wch_ref, wcl_ref, bc_ref, gc_ref, gr_ref, *, nh):
    h = h_ref[...]
    col = (jnp.dot(h, wch_ref[...], preferred_element_type=F32)
           + jnp.dot(h, wcl_ref[...], preferred_element_type=F32)) + bc_ref[...]
    lane = lax.broadcasted_iota(jnp.int32, col.shape, 1)
    gates = jnp.where(lane < nh, col, _log_sigmoid(col))
    gc_ref[...] = gates
    gr_ref[...] = gates.T[:2 * nh, :]


def mlstm_gates(h, w_if, b_i, b_f, *, tm=512):
    t, d = h.shape
    nh = b_i.shape[0]
    wc = jnp.zeros((d, LANES), F32).at[:, :2 * nh].set(w_if)
    wch, wcl = _split2(wc)
    bc = jnp.zeros((1, LANES), F32).at[0, :2 * nh].set(jnp.concatenate([b_i, b_f]))
    const = lambda i: (0, 0)
    return pl.pallas_call(
        functools.partial(_gates_body, nh=nh),
        grid=(t // tm,),
        in_specs=[
            pl.BlockSpec((tm, d), lambda i: (i, 0)),
            pl.BlockSpec((d, LANES), const), pl.BlockSpec((d, LANES), const),
            pl.BlockSpec((1, LANES), const),
        ],
        out_specs=[pl.BlockSpec((tm, LANES), lambda i: (i, 0)), pl.BlockSpec((2 * nh, tm), lambda i: (0, i))],
        out_shape=[jax.ShapeDtypeStruct((t, LANES), F32), jax.ShapeDtypeStruct((2 * nh, t), F32)],
        compiler_params=_params("parallel"),
        name="mlstm_gates",
    )(h, wch, wcl, bc)


def _mlstm_body(q_ref, k_ref, v_ref, o_ref, gc_ref, gr_ref, gh_ref, tril_ref, triu_ref, out_ref,
                c_ref, n_ref, m_ref, *, scale, nchunk, dv):
    L = ML_CHUNK
    row = lax.broadcasted_iota(jnp.int32, (L, L), 0)
    col = lax.broadcasted_iota(jnp.int32, (L, L), 1)
    tmask = col <= row
    c_ref[...] = jnp.zeros_like(c_ref)
    n_ref[...] = jnp.zeros_like(n_ref)
    m_ref[...] = jnp.zeros_like(m_ref)

    def cumsum_exact(parts, tri, left):
        acc = None
        for p in parts:
            d = (jnp.dot(tri, p, preferred_element_type=F32) if left
                 else jnp.dot(p, tri, preferred_element_type=F32))
            acc = d if acc is None else acc + d
        return acc

    def step(c, carry):
        c0 = pl.multiple_of(c * L, L)
        qc = (q_ref[pl.ds(c0, L), :].astype(F32) * scale).astype(BF16)
        kc = k_ref[pl.ds(c0, L), :]
        vc = v_ref[pl.ds(c0, L), :]
        gc = gc_ref[0, pl.ds(c0, L), :]
        gr = gr_ref[0, :, pl.ds(c0, L)]
        i_col, f_col = gc[:, 0:1], gc[:, 1:2]
        i_row, f_row = gr[0:1, :], gr[1:2, :]
        m_prev = m_ref[...]

        b_col = cumsum_exact(_split3(jnp.broadcast_to(f_col, (L, L))), tril_ref[...], True)
        b_row = cumsum_exact(_split3(jnp.broadcast_to(f_row, (SUBLANES, L))), triu_ref[...], False)[0:1, :]
        bt = b_col[:, 0:1]
        b_last = b_col[L - 1:L, 0:1]

        d_mat = jnp.where(tmask, b_col + (i_row - b_row), -1e30)
        m_t = jnp.maximum(bt + m_prev, jnp.max(d_mat, axis=1, keepdims=True))
        e = jnp.where(tmask, jnp.exp(d_mat - m_t), 0.0)
        w = lax.dot_general(qc, kc, NT_DIMS, preferred_element_type=F32) * e
        inter = jnp.exp(bt + m_prev - m_t)
        c_prev = c_ref[...]
        n_prev = n_ref[...]
        num = inter * jnp.dot(qc, c_prev.astype(BF16), preferred_element_type=F32) \
            + jnp.dot(w.astype(BF16), vc, preferred_element_type=F32)
        qn = jnp.sum(qc.astype(F32) * n_prev, axis=1, keepdims=True)
        den = inter * qn + jnp.sum(w, axis=1, keepdims=True)
        h = num / jnp.maximum(jnp.abs(den), jnp.exp(-m_t))

        hs = h * lax.rsqrt(jnp.mean(h * h, axis=1, keepdims=True) + EPS) * gh_ref[...]
        og = o_ref[pl.ds(c0, L), :].astype(F32)
        out_ref[pl.ds(c0, L), :] = (hs / (1.0 + jnp.exp(-og))).astype(out_ref.dtype)

        key_log = b_last - bt + i_col
        m_new = jnp.maximum(b_last + m_prev, jnp.max(key_log, axis=0, keepdims=True))
        decay = jnp.exp(key_log - m_new)
        keep = jnp.exp(b_last + m_prev - m_new)
        kd = kc.astype(F32) * decay
        c_ref[...] = keep * c_prev + jnp.dot(kd.T.astype(BF16), vc, preferred_element_type=F32)
        n_ref[...] = keep * n_prev + jnp.sum(kd, axis=0, keepdims=True)
        m_ref[...] = m_new
        return carry

    lax.fori_loop(0, nchunk, step, 0)


def mlstm(qkvo, gcol, grow, g_head, *, batch, seq, heads, dqk, dv):
    t = batch * seq
    nh = heads
    gc3 = jnp.stack([gcol[:, :nh].T, gcol[:, nh:2 * nh].T], axis=-1)
    gr3 = jnp.stack([grow[:nh], grow[nh:2 * nh]], axis=1)
    L = ML_CHUNK
    idx = jnp.arange(L)
    tril = (idx[None, :] <= idx[:, None]).astype(BF16)
    triu = (idx[:, None] <= idx[None, :]).astype(BF16)
    kq = nh * dqk // dqk
    del kq
    body = functools.partial(_mlstm_body, scale=dqk ** -0.5, nchunk=seq // L, dv=dv)
    v_blk0 = 2 * nh * dqk // dv
    o_blk0 = v_blk0 + nh
    const = lambda b, h: (0, 0)
    return pl.pallas_call(
        body,
        grid=(batch, nh),
        in_specs=[
            pl.BlockSpec((seq, dqk), lambda b, h: (b, h)),
            pl.BlockSpec((seq, dqk), lambda b, h: (b, nh + h)),
            pl.BlockSpec((seq, dv), lambda b, h: (b, v_blk0 + h)),
            pl.BlockSpec((seq, dv), lambda b, h: (b, o_blk0 + h)),
            pl.BlockSpec((1, seq, 2), lambda b, h: (h, b, 0)),
            pl.BlockSpec((1, 2, seq), lambda b, h: (h, 0, b)),
            pl.BlockSpec((1, dv), lambda b, h: (0, h)),
            pl.BlockSpec((L, L), const),
            pl.BlockSpec((L, L), const),
        ],
        out_specs=pl.BlockSpec((seq, dv), lambda b, h: (b, h)),
        out_shape=jax.ShapeDtypeStruct((t, nh * dv), BF16),
        scratch_shapes=[pltpu.VMEM((dqk, dv), F32), pltpu.VMEM((1, dqk), F32), pltpu.VMEM((1, 1), F32)],
        compiler_params=_params("parallel", "parallel"),
        name="mlstm",
    )(qkvo, qkvo, qkvo, qkvo, gc3, gr3, g_head.reshape(1, nh * dv), tril, triu)


def _moe(o, w_out, x, norm_ffn, w_group, b_group, w_expert, b_expert, w1, w3, w2, layer, g_next, *, final,
         tm=256):
    x1, h2, ri, rw, cnt = proj_route(o, w_out.astype(BF16), x, norm_ffn, w_group, b_group, w_expert, b_expert)
    pos, src3, te, n_tiles, nt_max = _route_tables(ri, cnt, tm=tm, n_tok=x.shape[0])
    ys = routed_experts(h2, src3, te, n_tiles, w1, w3, w2, layer, tm=tm, nt_max=nt_max)
    return moe_combine(x1, ys, pos, rw, g_next, final=final)


def kernel(x, norm_mix, norm_ffn, norm_final, sb_w_in, sb_w_out, ml_w_in, ml_b_i, ml_b_f, ml_g_head, ml_w_out,
           moe_w_group, moe_b_group, moe_w_expert, moe_b_expert, moe_w1, moe_w3, moe_w2):
    batch, seq, d = x.shape
    xt = x.reshape(batch * seq, d)

    h = rmsnorm_bf16(xt, norm_mix[0])
    qkv = matmul_bf16(h, sb_w_in, 3 * d)
    o = sb_attention(qkv, batch=batch, seq=seq, heads=SB_HEADS, dh=d // SB_HEADS)
    xt, h = _moe(o, sb_w_out[0], xt, norm_ffn[0], moe_w_group[0], moe_b_group[0], moe_w_expert[0],
                 moe_b_expert[0], moe_w1, moe_w3, moe_w2, 0, norm_mix[1], final=False)

    dqk = (d // 2) // ML_HEADS
    dv = d // ML_HEADS
    n_main = 2 * ML_HEADS * dqk + 2 * ML_HEADS * dv
    qkvo = matmul_bf16(h, ml_w_in, n_main)
    gcol, grow = mlstm_gates(h, ml_w_in[0][:, n_main:], ml_b_i[0], ml_b_f[0])
    o = mlstm(qkvo, gcol, grow, ml_g_head[0], batch=batch, seq=seq, heads=ML_HEADS, dqk=dqk, dv=dv)
    out = _moe(o, ml_w_out[0], xt, norm_ffn[1], moe_w_group[1], moe_b_group[1], moe_w_expert[1],
               moe_b_expert[1], moe_w1, moe_w3, moe_w2, 1, norm_final, final=True)
    return out.reshape(batch, seq, d)
```

```python
import functools

import jax
import jax.numpy as jnp
from jax import lax
from jax.experimental import pallas as pl
from jax.experimental.pallas import tpu as pltpu

F32 = jnp.float32
BF16 = jnp.bfloat16
EPS = 1e-6

V7X_VMEM_BYTES = 64 * 1024 * 1024
VMEM_LIMIT_BYTES = V7X_VMEM_BYTES - 8 * 1024 * 1024
LANES = 128
SUBLANES = 8

SB_HEADS = 16
ML_HEADS = 8
N_GROUPS = 4
EXPERTS_PER_GROUP = 8
N_EXPERTS = N_GROUPS * EXPERTS_PER_GROUP
TOP_K = 2
ML_CHUNK = 128
SB_BLOCK = 128
ROUTER_ROWS = 128
EXP_F32_UNDERFLOW = -104.0
DMA_PRIORITIES = 2

NT_DIMS = (((1,), (1,)), ((), ()))


def _params(*sem):
    return pltpu.CompilerParams(dimension_semantics=sem, vmem_limit_bytes=VMEM_LIMIT_BYTES)


def _split2(x):
    hi = x.astype(BF16)
    lo = (x - hi.astype(F32)).astype(BF16)
    return hi, lo


def _split3(x):
    hi = x.astype(BF16)
    r = x - hi.astype(F32)
    mid = r.astype(BF16)
    lo = (r - mid.astype(F32)).astype(BF16)
    return hi, mid, lo


def _rms(x, g):
    return x * lax.rsqrt(jnp.mean(x * x, axis=-1, keepdims=True) + EPS) * g


def _slab_rows(d):
    return d // LANES


def _slab_pitch(d):
    return _slab_rows(d) + 1


def _slab_store(ref, x, n):
    d = x.shape[1]
    pitch = _slab_pitch(d)
    for c in range(_slab_rows(d)):
        ref[pl.ds(c, n, stride=pitch), :] = x[:, c * LANES:(c + 1) * LANES]
    ref[pl.ds(_slab_rows(d), n, stride=pitch), :] = jnp.zeros((n, LANES), x.dtype)


def _slab_load(ref, lead, n, d):
    pitch = _slab_pitch(d)
    return jnp.concatenate([ref[lead + (pl.ds(c, n, stride=pitch), slice(None))] for c in range(_slab_rows(d))],
                           axis=1)


def _rmsnorm_body(x_ref, g_ref, h_ref):
    h_ref[...] = _rms(x_ref[...], g_ref[...]).astype(h_ref.dtype)


def rmsnorm_bf16(x, g, *, tm=512):
    t, d = x.shape
    return pl.pallas_call(
        _rmsnorm_body,
        grid=(t // tm,),
        in_specs=[pl.BlockSpec((tm, d), lambda i: (i, 0)), pl.BlockSpec((1, d), lambda i: (0, 0))],
        out_specs=pl.BlockSpec((tm, d), lambda i: (i, 0)),
        out_shape=jax.ShapeDtypeStruct((t, d), BF16),
        compiler_params=_params("parallel"),
        name="rmsnorm",
    )(x, g.reshape(1, d))


def _mm_body(h_ref, w_ref, o_ref, wb_ref):
    @pl.when(pl.program_id(1) == 0)
    def _():
        wb_ref[...] = w_ref[...].astype(BF16)

    o_ref[...] = jnp.dot(h_ref[...], wb_ref[...], preferred_element_type=F32).astype(o_ref.dtype)


def matmul_bf16(h, w, n, *, tm=1024, tn=1024):
    t, k = h.shape
    return pl.pallas_call(
        _mm_body,
        grid=(n // tn, t // tm),
        in_specs=[pl.BlockSpec((tm, k), lambda j, i: (i, 0)), pl.BlockSpec((None, k, tn), lambda j, i: (0, 0, j))],
        out_specs=pl.BlockSpec((tm, tn), lambda j, i: (i, j)),
        out_shape=jax.ShapeDtypeStruct((t, n), BF16),
        scratch_shapes=[pltpu.VMEM((k, tn), BF16)],
        compiler_params=_params("parallel", "arbitrary"),
        name="proj_in",
    )(h, w)


def _sb_attn_body(q_ref, k_ref, v_ref, cm_ref, o_ref, *, scale, nblk, hg, kc):
    blk = SB_BLOCK
    dh = SB_BLOCK
    wid = kc * blk
    rel = lax.broadcasted_iota(jnp.int32, (blk, wid), 1) - lax.broadcasted_iota(jnp.int32, (blk, wid), 0)

    def chunk(qbs, q0, c, rs, accs, masked):
        k0 = pl.multiple_of(c * wid, wid)
        if masked:
            valid = rel < (q0 - k0)
        z = jnp.concatenate(
            [lax.dot_general(qbs[h], k_ref[pl.ds(k0, wid), h * dh:(h + 1) * dh], NT_DIMS,
                             preferred_element_type=F32) for h in range(hg)], axis=1)
        sp = jnp.maximum(z, 0.0) + jnp.log(1.0 + jnp.exp(-jnp.abs(z)))
        log_beta = z - sp
        if masked:
            valid_all = jnp.concatenate([valid] * hg, axis=1)
            sp = jnp.where(valid_all, sp, 0.0)
        hi, lo = _split2(sp)
        nb = hg * kc
        x = jnp.concatenate(
            [jnp.concatenate([hi[:, j * blk:(j + 1) * blk], lo[:, j * blk:(j + 1) * blk]], axis=1)
             for j in range(nb)], axis=0)
        cs = jnp.dot(x, cm_ref[...], preferred_element_type=F32)
        new_rs = []
        suffix = [None] * nb
        for h in range(hg):
            r = rs[h]
            for j in reversed(range(kc)):
                csj = cs[(h * kc + j) * blk:(h * kc + j + 1) * blk]
                suffix[h * kc + j] = csj[:, :blk] + r
                r = r + csj[:, blk:]
            new_rs.append(r)
        a = jnp.exp(log_beta + jnp.concatenate(suffix, axis=1))
        if masked:
            a = jnp.where(valid_all, a, 0.0)
        a = a.astype(BF16)
        new_accs = [accs[h] + jnp.dot(a[:, h * wid:(h + 1) * wid], v_ref[pl.ds(k0, wid), h * dh:(h + 1) * dh],
                                      preferred_element_type=F32) for h in range(hg)]
        return tuple(new_rs), tuple(new_accs)

    def q_step(qi, carry):
        q0 = pl.multiple_of(qi * blk, blk)
        qbs = [(q_ref[pl.ds(q0, blk), h * dh:(h + 1) * dh].astype(F32) * scale).astype(BF16) for h in range(hg)]
        zeros = tuple(jnp.zeros((blk, blk), F32) for _ in range(hg))
        top = qi // kc
        rs, accs = chunk(qbs, q0, top, zeros, zeros, True)

        def live(rs):
            return jnp.max(functools.reduce(jnp.maximum, rs)) > EXP_F32_UNDERFLOW

        def k_cond(c):
            return (c[0] < top) & c[1]

        def k_step(c):
            rs, accs = chunk(qbs, q0, top - 1 - c[0], c[2], c[3], False)
            return c[0] + 1, live(rs), rs, accs

        accs = lax.while_loop(k_cond, k_step, (jnp.int32(0), live(rs), rs, accs))[3]
        for h in range(hg):
            o_ref[pl.ds(q0, blk), h * dh:(h + 1) * dh] = accs[h].astype(o_ref.dtype)
        return carry

    lax.fori_loop(0, nblk, q_step, 0)


def _cumsum_matrix(blk):
    j = jnp.arange(2 * blk)[:, None] % blk
    c = jnp.arange(2 * blk)[None, :]
    return jnp.where((c >= blk) | (j > c), -1.0, 0.0).astype(BF16)


def sb_attention(qkv, *, batch, seq, heads, dh, hg=8, kc=2):
    assert dh == SB_BLOCK
    nblk = seq // SB_BLOCK
    assert nblk % kc == 0 and heads % hg == 0
    body = functools.partial(_sb_attn_body, scale=dh ** -0.5, nblk=nblk, hg=hg, kc=kc)
    cm = _cumsum_matrix(SB_BLOCK)
    ng = heads // hg
    return pl.pallas_call(
        body,
        grid=(batch, ng),
        in_specs=[
            pl.BlockSpec((seq, hg * dh), lambda b, h: (b, h)),
            pl.BlockSpec((seq, hg * dh), lambda b, h: (b, ng + h)),
            pl.BlockSpec((seq, hg * dh), lambda b, h: (b, 2 * ng + h)),
            pl.BlockSpec((2 * SB_BLOCK, 2 * SB_BLOCK), lambda b, h: (0, 0)),
        ],
        out_specs=pl.BlockSpec((seq, hg * dh), lambda b, h: (b, h)),
        out_shape=jax.ShapeDtypeStruct((batch * seq, heads * dh), BF16),
        compiler_params=_params("parallel", "parallel"),
        name="sb_attn",
    )(qkv, qkv, qkv, cm)


def _proj_route_body(o_ref, w_ref, x_ref, g_ref, wrh_ref, wrl_ref, br_ref, tri_ref,
                     x1_ref, ri_ref, rw_ref, cnt_ref, base_ref, *, tm):
    i = pl.program_id(0)

    @pl.when(i == 0)
    def _():
        base_ref[...] = jnp.zeros_like(base_ref)

    x1 = x_ref[...] + jnp.dot(o_ref[...], w_ref[...], preferred_element_type=F32)
    x1_ref[...] = x1
    h2 = _rms(x1, g_ref[...])

    hh, hl = _split2(h2)
    wrh = wrh_ref[...]
    lg = (lax.dot_general(wrh, hh, NT_DIMS, preferred_element_type=F32)
          + lax.dot_general(wrh, hl, NT_DIMS, preferred_element_type=F32)
          + lax.dot_general(wrl_ref[...], hh, NT_DIMS, preferred_element_type=F32)) + br_ref[...]

    r8 = lax.broadcasted_iota(jnp.int32, (SUBLANES, tm), 0)
    neg_inf = jnp.float32(-jnp.inf)
    gl = jnp.where(r8 < N_GROUPS, lg[N_EXPERTS:N_EXPERTS + SUBLANES, :], neg_inf)
    gmax = jnp.max(gl, axis=0, keepdims=True)
    gsel = jnp.min(jnp.where(gl == gmax, r8, SUBLANES), axis=0, keepdims=True)
    p_sel = 1.0 / jnp.sum(jnp.exp(gl - gmax), axis=0, keepdims=True)

    es = lg[0:EXPERTS_PER_GROUP, :]
    for g in range(1, N_GROUPS):
        es = jnp.where(gsel == g, lg[g * EXPERTS_PER_GROUP:(g + 1) * EXPERTS_PER_GROUP, :], es)
    m1 = jnp.max(es, axis=0, keepdims=True)
    i1 = jnp.min(jnp.where(es == m1, r8, SUBLANES), axis=0, keepdims=True)
    es2 = jnp.where(r8 == i1, neg_inf, es)
    m2 = jnp.max(es2, axis=0, keepdims=True)
    i2 = jnp.min(jnp.where(es2 == m2, r8, SUBLANES), axis=0, keepdims=True)
    t = jnp.exp(m2 - m1)
    w1 = p_sel / (1.0 + t)
    w2 = w1 * t
    e1 = gsel * EXPERTS_PER_GROUP + i1
    e2 = gsel * EXPERTS_PER_GROUP + i2

    r32 = lax.broadcasted_iota(jnp.int32, (N_EXPERTS, tm), 0)
    hit1 = r32 == e1
    hit2 = r32 == e2
    onehot = (jnp.where(hit1, 1.0, 0.0) + jnp.where(hit2, 1.0, 0.0)).astype(BF16)
    cs = jnp.dot(onehot, tri_ref[...], preferred_element_type=F32)
    base = base_ref[...]
    before = cs[:, :tm] + base
    rank1 = jnp.sum(jnp.where(hit1, before, 0.0), axis=0, keepdims=True)
    rank2 = jnp.sum(jnp.where(hit2, before, 0.0), axis=0, keepdims=True)
    tot = cs[:, tm:]
    new_base = base + jnp.concatenate([tot] * (tm // LANES), axis=1)
    base_ref[...] = new_base
    cnt_ref[...] = new_base[:, :LANES]

    zi = jnp.zeros((SUBLANES, tm), jnp.int32)
    ri = jnp.where(r8 == 0, e1, zi)
    ri = jnp.where(r8 == 1, e2, ri)
    ri = jnp.where(r8 == 2, rank1.astype(jnp.int32), ri)
    ri = jnp.where(r8 == 3, rank2.astype(jnp.int32), ri)
    ri_ref[...] = ri
    zf = jnp.zeros((SUBLANES, tm), F32)
    rw = jnp.where(r8 == 0, w1, zf)
    rw = jnp.where(r8 == 1, w2, rw)
    rw_ref[...] = rw


def proj_route(o, w_out, x, g, w_group, b_group, w_expert, b_expert, *, tm=256):
    t, d = x.shape
    k = o.shape[1]
    wr = jnp.zeros((ROUTER_ROWS, d), F32)
    wr = wr.at[:N_EXPERTS].set(w_expert.T).at[N_EXPERTS:N_EXPERTS + N_GROUPS].set(w_group.T)
    wrh, wrl = _split2(wr)
    br = jnp.zeros((ROUTER_ROWS,), F32).at[:N_EXPERTS].set(b_expert).at[N_EXPERTS:N_EXPERTS + N_GROUPS].set(b_group)
    br = jnp.broadcast_to(br[:, None], (ROUTER_ROWS, tm))
    tp = jnp.arange(tm)[:, None]
    tri = jnp.concatenate([(tp < jnp.arange(tm)[None, :]).astype(BF16), jnp.ones((tm, LANES), BF16)], axis=1)
    body = functools.partial(_proj_route_body, tm=tm)
    row = lambda i: (i, 0)
    const = lambda i: (0, 0)
    return pl.pallas_call(
        body,
        grid=(t // tm,),
        in_specs=[
            pl.BlockSpec((tm, k), row),
            pl.BlockSpec((k, d), const),
            pl.BlockSpec((tm, d), row),
            pl.BlockSpec((1, d), const),
            pl.BlockSpec((ROUTER_ROWS, d), const),
            pl.BlockSpec((ROUTER_ROWS, d), const),
            pl.BlockSpec((ROUTER_ROWS, tm), const),
            pl.BlockSpec((tm, tm + LANES), const),
        ],
        out_specs=[
            pl.BlockSpec((tm, d), row),
            pl.BlockSpec((SUBLANES, tm), lambda i: (0, i)),
            pl.BlockSpec((SUBLANES, tm), lambda i: (0, i)),
            pl.BlockSpec((N_EXPERTS, LANES), const),
        ],
        out_shape=[
            jax.ShapeDtypeStruct((t, d), F32),
            jax.ShapeDtypeStruct((SUBLANES, t), jnp.int32),
            jax.ShapeDtypeStruct((SUBLANES, t), F32),
            jax.ShapeDtypeStruct((N_EXPERTS, LANES), F32),
        ],
        scratch_shapes=[pltpu.VMEM((N_EXPERTS, tm), F32)],
        compiler_params=_params("arbitrary"),
        name="proj_route",
    )(o, w_out, x, g.reshape(1, d), wrh, wrl, br, tri)


def _dispatch_body(pos_ref, x_ref, g_ref, xs_hbm, hs, sem, *, tm, n_pairs):
    i = pl.program_id(0)
    n = pl.num_programs(0)
    slot = lax.rem(i, 2)
    d = x_ref.shape[1]
    pitch = _slab_pitch(d)

    def row_out(p, r, s):
        return pltpu.make_async_copy(hs.at[s, pl.ds(r * pitch, pitch)], xs_hbm.at[pl.ds(p * pitch, pitch)], sem.at[s])

    def wait_tile(s):
        for k in range(TOP_K):
            for r in range(tm):
                row_out(0, r, s).wait()

    def tail(s):
        return pltpu.make_async_copy(hs.at[s], xs_hbm.at[pl.ds(n_pairs * pitch, tm * pitch)], sem.at[s])

    @pl.when(i >= 2)
    def _():
        wait_tile(slot)

    _slab_store(hs.at[slot], _rms(x_ref[...], g_ref[...]), tm)
    for k in range(TOP_K):
        for r in range(tm):
            row_out(pos_ref[0, 0, k * tm + r], r, slot).start(priority=r % DMA_PRIORITIES)

    @pl.when(i == n - 1)
    def _():
        wait_tile(slot)

        @pl.when(i >= 1)
        def _():
            wait_tile(1 - slot)

        hs[slot] = jnp.zeros(hs.shape[1:], hs.dtype)
        tail(slot).start()
        tail(slot).wait()


def dispatch_rows(x1, g, pos, *, tm=256):
    t, d = x1.shape
    nt = t // tm
    pitch = _slab_pitch(d)
    n_pairs = TOP_K * t
    pos3 = pos.reshape(TOP_K, nt, tm).transpose(1, 0, 2).reshape(nt, 1, TOP_K * tm)
    return pl.pallas_call(
        functools.partial(_dispatch_body, tm=tm, n_pairs=n_pairs),
        grid=(nt,),
        in_specs=[
            pl.BlockSpec((1, 1, TOP_K * tm), lambda i: (i, 0, 0), memory_space=pltpu.SMEM),
            pl.BlockSpec((tm, d), lambda i: (i, 0)),
            pl.BlockSpec((1, d), lambda i: (0, 0)),
        ],
        out_specs=pl.BlockSpec(memory_space=pl.ANY),
        out_shape=jax.ShapeDtypeStruct(((n_pairs + tm) * pitch, LANES), F32),
        scratch_shapes=[pltpu.VMEM((2, tm * pitch, LANES), F32), pltpu.SemaphoreType.DMA((2,))],
        compiler_params=_params("arbitrary"),
        name="dispatch",
    )(pos3, x1, g.reshape(1, d))


def _experts_body(te_ref, ts_ref, nt_ref, xs_hbm, w1_ref, w3_ref, w2_ref, ys_hbm,
                  xbuf, ybuf, w1b, w3b, w2b, isem, osem, *, tm, d, n_pairs):
    i = pl.program_id(0)
    n_tiles = nt_ref[0]
    slot = lax.rem(i, 2)
    pitch = _slab_pitch(d)

    def tile_in(j, s):
        return pltpu.make_async_copy(xs_hbm.at[pl.ds(ts_ref[j] * pitch, tm * pitch)], xbuf.at[s], isem.at[s])

    def tile_out(j, s):
        return pltpu.make_async_copy(ybuf.at[s], ys_hbm.at[pl.ds(ts_ref[j] * pitch, tm * pitch)], osem.at[s])

    @pl.when(i == 0)
    def _():
        tile_in(0, 0).start()
        ybuf[1] = jnp.zeros(ybuf.shape[1:], ybuf.dtype)
        zero_tail = pltpu.make_async_copy(ybuf.at[1], ys_hbm.at[pl.ds(n_pairs * pitch, tm * pitch)], osem.at[1])
        zero_tail.start()
        zero_tail.wait()

    @pl.when(i < n_tiles)
    def _():
        tile_in(i, slot).wait()

        @pl.when(i + 1 < n_tiles)
        def _():
            tile_in(i + 1, 1 - slot).start()

        @pl.when((i == 0) | (te_ref[i] != te_ref[jnp.maximum(i - 1, 0)]))
        def _():
            w1b[...] = w1_ref[0].astype(BF16)
            w3b[...] = w3_ref[0].astype(BF16)
            w2b[...] = w2_ref[0].astype(BF16)

        xb = _slab_load(xbuf, (slot,), tm, d).astype(BF16)
        a = jnp.dot(xb, w1b[...], preferred_element_type=F32)
        b = jnp.dot(xb, w3b[...], preferred_element_type=F32)
        hid = (a * (1.0 / (1.0 + jnp.exp(-a))) * b).astype(BF16)
        _slab_store(ybuf.at[slot], jnp.dot(hid, w2b[...], preferred_element_type=F32), tm)

        @pl.when(i >= 1)
        def _():
            tile_out(i - 1, 1 - slot).wait()

        tile_out(i, slot).start()

        @pl.when(i == n_tiles - 1)
        def _():
            tile_out(i, slot).wait()


def _route_tables(ri, cnt, *, tm, n_tok):
    nt_max = TOP_K * n_tok // tm + N_EXPERTS - 1
    counts = cnt[:, 0].astype(jnp.int32)
    ends = jnp.cumsum(counts)
    offs = ends - counts
    eid = jnp.arange(N_EXPERTS, dtype=jnp.int32)
    sel = ri[0:TOP_K, :, None] == eid
    pos = jnp.sum(jnp.where(sel, offs, 0), axis=-1) + ri[TOP_K:2 * TOP_K]
    tiles = (counts + tm - 1) // tm
    tile_ends = jnp.cumsum(tiles)
    n_tiles = tile_ends[-1].astype(jnp.int32)
    step = jnp.minimum(jnp.arange(nt_max, dtype=jnp.int32), n_tiles - 1)
    te = jnp.clip(jnp.sum((step[:, None] >= tile_ends[None, :]).astype(jnp.int32), axis=1), 0, N_EXPERTS - 1)
    first = jnp.sum(jnp.where(te[:, None] == eid, tile_ends - tiles, 0), axis=-1)
    ts = jnp.sum(jnp.where(te[:, None] == eid, offs, 0), axis=-1) + (step - first) * tm
    return pos, te, ts.astype(jnp.int32), n_tiles.reshape(1), nt_max


def routed_experts(xs, te, ts, n_tiles, w1, w3, w2, layer, *, tm, nt_max, n_tok):
    d, f = w1.shape[-2:]
    pitch = _slab_pitch(d)
    body = functools.partial(_experts_body, tm=tm, d=d, n_pairs=TOP_K * n_tok)
    grid_spec = pltpu.PrefetchScalarGridSpec(
        num_scalar_prefetch=3,
        grid=(nt_max,),
        in_specs=[
            pl.BlockSpec(memory_space=pl.ANY),
            pl.BlockSpec((None, 1, d, f), lambda i, te, ts, nt: (layer, te[i], 0, 0)),
            pl.BlockSpec((None, 1, d, f), lambda i, te, ts, nt: (layer, te[i], 0, 0)),
            pl.BlockSpec((None, 1, f, d), lambda i, te, ts, nt: (layer, te[i], 0, 0)),
        ],
        out_specs=pl.BlockSpec(memory_space=pl.ANY),
        scratch_shapes=[
            pltpu.VMEM((2, tm * pitch, LANES), F32),
            pltpu.VMEM((2, tm * pitch, LANES), F32),
            pltpu.VMEM((d, f), BF16),
            pltpu.VMEM((d, f), BF16),
            pltpu.VMEM((f, d), BF16),
            pltpu.SemaphoreType.DMA((2,)),
            pltpu.SemaphoreType.DMA((2,)),
        ],
    )
    return pl.pallas_call(
        body,
        grid_spec=grid_spec,
        out_shape=jax.ShapeDtypeStruct(xs.shape, F32),
        compiler_params=_params("arbitrary"),
        name="experts",
    )(te, ts, n_tiles, xs, w1, w3, w2)


def _combine_body(pos_ref, nxt_ref, x_ref, w_ref, g_ref, ys_hbm, *rest, tm, final):
    if final:
        h_ref, ybuf, sem = rest
        x2_ref = None
    else:
        x2_ref, h_ref, ybuf, sem = rest
    i = pl.program_id(0)
    n = pl.num_programs(0)
    slot = lax.rem(i, 2)
    d = x_ref.shape[1]
    rows, pitch = _slab_rows(d), _slab_pitch(d)

    def row_in(src, k, r, s):
        return pltpu.make_async_copy(ys_hbm.at[pl.ds(src * pitch, rows)], ybuf.at[s, k, pl.ds(r * pitch, rows)],
                                     sem.at[s])

    def gather_start(idx_ref, s):
        for k in range(TOP_K):
            for r in range(tm):
                row_in(idx_ref[0, 0, k * tm + r], k, r, s).start(priority=r % DMA_PRIORITIES)

    def gather_wait(s):
        for k in range(TOP_K):
            for r in range(tm):
                row_in(0, k, r, s).wait()

    @pl.when(i == 0)
    def _():
        gather_start(pos_ref, 0)

    gather_wait(slot)

    @pl.when(i + 1 < n)
    def _():
        gather_start(nxt_ref, 1 - slot)

    w = w_ref[...]
    x2 = x_ref[...] + w[:, 0:1] * _slab_load(ybuf, (slot, 0), tm, d) + w[:, 1:2] * _slab_load(ybuf, (slot, 1), tm, d)
    if x2_ref is not None:
        x2_ref[...] = x2
    h_ref[...] = _rms(x2, g_ref[...]).astype(h_ref.dtype)


def moe_combine(x1, ys, pos, rw, g, *, final, tm=128):
    t, d = x1.shape
    nt = t // tm
    w = rw[0:TOP_K].T
    pos3 = pos.reshape(TOP_K, nt, tm).transpose(1, 0, 2).reshape(nt, 1, TOP_K * tm)
    row = lambda i: (i, 0)
    smem_tile = lambda imap: pl.BlockSpec((1, 1, TOP_K * tm), imap, memory_space=pltpu.SMEM)
    in_specs = [
        smem_tile(lambda i: (i, 0, 0)),
        smem_tile(lambda i: (jnp.minimum(i + 1, nt - 1), 0, 0)),
        pl.BlockSpec((tm, d), row),
        pl.BlockSpec((tm, TOP_K), row),
        pl.BlockSpec((1, d), lambda i: (0, 0)),
        pl.BlockSpec(memory_space=pl.ANY),
    ]
    scratch = [pltpu.VMEM((2, TOP_K, tm * _slab_pitch(d), LANES), F32), pltpu.SemaphoreType.DMA((2,))]
    args = (pos3, pos3, x1, w, g.reshape(1, d), ys)
    body = functools.partial(_combine_body, tm=tm, final=final)
    if final:
        return pl.pallas_call(
            body, grid=(nt,), in_specs=in_specs,
            out_specs=pl.BlockSpec((tm, d), row),
            out_shape=jax.ShapeDtypeStruct((t, d), F32),
            scratch_shapes=scratch,
            compiler_params=_params("arbitrary"), name="combine_final",
        )(*args)
    return pl.pallas_call(
        body, grid=(nt,), in_specs=in_specs,
        out_specs=[pl.BlockSpec((tm, d), row), pl.BlockSpec((tm, d), row)],
        out_shape=[jax.ShapeDtypeStruct((t, d), F32), jax.ShapeDtypeStruct((t, d), BF16)],
        scratch_shapes=scratch,
        compiler_params=_params("arbitrary"), name="combine",
    )(*args)


def _log_sigmoid(x):
    return jnp.minimum(x, 0.0) - jnp.log(1.0 + jnp.exp(-jnp.abs(x)))


def _gates_body(h_ref, wch_ref, wcl_ref, bc_ref, gc_ref, gr_ref, *, nh):
    h = h_ref[...]
    col = (jnp.dot(h, wch_ref[...], preferred_element_type=F32)
           + jnp.dot(h, wcl_ref[...], preferred_element_type=F32)) + bc_ref[...]
    lane = lax.broadcasted_iota(jnp.int32, col.shape, 1)
    gates = jnp.where(lane < nh, col, _log_sigmoid(col))
    gc_ref[...] = gates
    gr_ref[...] = gates.T[:2 * nh, :]


def mlstm_gates(h, w_if, b_i, b_f, *, tm=512):
    t, d = h.shape
    nh = b_i.shape[0]
    wc = jnp.zeros((d, LANES), F32).at[:, :2 * nh].set(w_if)
    wch, wcl = _split2(wc)
    bc = jnp.zeros((1, LANES), F32).at[0, :2 * nh].set(jnp.concatenate([b_i, b_f]))
    const = lambda i: (0, 0)
    return pl.pallas_call(
        functools.partial(_gates_body, nh=nh),
        grid=(t // tm,),
        in_specs=[
            pl.BlockSpec((tm, d), lambda i: (i, 0)),
            pl.BlockSpec((d, LANES), const), pl.BlockSpec((d, LANES), const),
            pl.BlockSpec((1, LANES), const),
        ],
        out_specs=[pl.BlockSpec((tm, LANES), lambda i: (i, 0)), pl.BlockSpec((2 * nh, tm), lambda i: (0, i))],
        out_shape=[jax.ShapeDtypeStruct((t, LANES), F32), jax.ShapeDtypeStruct((2 * nh, t), F32)],
        compiler_params=_params("parallel"),
        name="mlstm_gates",
    )(h, wch, wcl, bc)


def _mlstm_body(q_ref, k_ref, v_ref, o_ref, gc_ref, gr_ref, gh_ref, tril_ref, triu_ref, out_ref,
                c_ref, n_ref, m_ref, *, scale, nchunk):
    L = ML_CHUNK
    row = lax.broadcasted_iota(jnp.int32, (L, L), 0)
    col = lax.broadcasted_iota(jnp.int32, (L, L), 1)
    tmask = col <= row
    c_ref[...] = jnp.zeros_like(c_ref)
    n_ref[...] = jnp.zeros_like(n_ref)
    m_ref[...] = jnp.zeros_like(m_ref)

    def cumsum_exact(parts, tri, left):
        acc = None
        for p in parts:
            d = (jnp.dot(tri, p, preferred_element_type=F32) if left
                 else jnp.dot(p, tri, preferred_element_type=F32))
            acc = d if acc is None else acc + d
        return acc

    def step(c, carry):
        c0 = pl.multiple_of(c * L, L)
        qc = (q_ref[pl.ds(c0, L), :].astype(F32) * scale).astype(BF16)
        kc = k_ref[pl.ds(c0, L), :]
        vc = v_ref[pl.ds(c0, L), :]
        gc = gc_ref[0, pl.ds(c0, L), :]
        gr = gr_ref[0, :, pl.ds(c0, L)]
        i_col, f_col = gc[:, 0:1], gc[:, 1:2]
        i_row, f_row = gr[0:1, :], gr[1:2, :]
        m_prev = m_ref[...]

        b_col = cumsum_exact(_split3(jnp.broadcast_to(f_col, (L, L))), tril_ref[...], True)
        b_row = cumsum_exact(_split3(jnp.broadcast_to(f_row, (SUBLANES, L))), triu_ref[...], False)[0:1, :]
        bt = b_col[:, 0:1]
        b_last = b_col[L - 1:L, 0:1]

        d_mat = jnp.where(tmask, b_col + (i_row - b_row), -1e30)
        m_t = jnp.maximum(bt + m_prev, jnp.max(d_mat, axis=1, keepdims=True))
        e = jnp.where(tmask, jnp.exp(d_mat - m_t), 0.0)
        w = lax.dot_general(qc, kc, NT_DIMS, preferred_element_type=F32) * e
        inter = jnp.exp(bt + m_prev - m_t)
        c_prev = c_ref[...]
        n_prev = n_ref[...]
        num = inter * jnp.dot(qc, c_prev.astype(BF16), preferred_element_type=F32) \
            + jnp.dot(w.astype(BF16), vc, preferred_element_type=F32)
        qn = jnp.sum(qc.astype(F32) * n_prev, axis=1, keepdims=True)
        den = inter * qn + jnp.sum(w, axis=1, keepdims=True)
        h = num / jnp.maximum(jnp.abs(den), jnp.exp(-m_t))

        hs = h * lax.rsqrt(jnp.mean(h * h, axis=1, keepdims=True) + EPS) * gh_ref[...]
        og = o_ref[pl.ds(c0, L), :].astype(F32)
        out_ref[pl.ds(c0, L), :] = (hs / (1.0 + jnp.exp(-og))).astype(out_ref.dtype)

        key_log = b_last - bt + i_col
        m_new = jnp.maximum(b_last + m_prev, jnp.max(key_log, axis=0, keepdims=True))
        decay = jnp.exp(key_log - m_new)
        keep = jnp.exp(b_last + m_prev - m_new)
        kd = kc.astype(F32) * decay
        c_ref[...] = keep * c_prev + jnp.dot(kd.T.astype(BF16), vc, preferred_element_type=F32)
        n_ref[...] = keep * n_prev + jnp.sum(kd, axis=0, keepdims=True)
        m_ref[...] = m_new
        return carry

    lax.fori_loop(0, nchunk, step, 0)


def mlstm(qkvo, gcol, grow, g_head, *, batch, seq, heads, dqk, dv):
    t = batch * seq
    nh = heads
    gc3 = jnp.stack([gcol[:, :nh].T, gcol[:, nh:2 * nh].T], axis=-1)
    gr3 = jnp.stack([grow[:nh], grow[nh:2 * nh]], axis=1)
    L = ML_CHUNK
    nchunk = seq // L
    idx = jnp.arange(L)
    tril = (idx[None, :] <= idx[:, None]).astype(BF16)
    triu = (idx[:, None] <= idx[None, :]).astype(BF16)
    body = functools.partial(_mlstm_body, scale=dqk ** -0.5, nchunk=nchunk)
    v_blk0 = 2 * nh * dqk // dv
    o_blk0 = v_blk0 + nh
    const = lambda b, h: (0, 0)
    return pl.pallas_call(
        body,
        grid=(batch, nh),
        in_specs=[
            pl.BlockSpec((seq, dqk), lambda b, h: (b, h)),
            pl.BlockSpec((seq, dqk), lambda b, h: (b, nh + h)),
            pl.BlockSpec((seq, dv), lambda b, h: (b, v_blk0 + h)),
            pl.BlockSpec((seq, dv), lambda b, h: (b, o_blk0 + h)),
            pl.BlockSpec((1, seq, 2), lambda b, h: (h, b, 0)),
            pl.BlockSpec((1, 2, seq), lambda b, h: (h, 0, b)),
            pl.BlockSpec((1, dv), lambda b, h: (0, h)),
            pl.BlockSpec((L, L), const),
            pl.BlockSpec((L, L), const),
        ],
        out_specs=pl.BlockSpec((seq, dv), lambda b, h: (b, h)),
        out_shape=jax.ShapeDtypeStruct((t, nh * dv), BF16),
        scratch_shapes=[pltpu.VMEM((dqk, dv), F32), pltpu.VMEM((1, dqk), F32), pltpu.VMEM((1, 1), F32)],
        compiler_params=_params("parallel", "parallel"),
        name="mlstm",
    )(qkvo, qkvo, qkvo, qkvo, gc3, gr3, g_head.reshape(1, nh * dv), tril, triu)


def _moe(o, w_out, x, norm_ffn, w_group, b_group, w_expert, b_expert, w1, w3, w2, layer, g_next, *, final,
         tm=256):
    n_tok = x.shape[0]
    x1, ri, rw, cnt = proj_route(o, w_out.astype(BF16), x, norm_ffn, w_group, b_group, w_expert, b_expert)
    pos, te, ts, n_tiles, nt_max = _route_tables(ri, cnt, tm=tm, n_tok=n_tok)
    xs = dispatch_rows(x1, norm_ffn, pos)
    ys = routed_experts(xs, te, ts, n_tiles, w1, w3, w2, layer, tm=tm, nt_max=nt_max, n_tok=n_tok)
    return moe_combine(x1, ys, pos, rw, g_next, final=final)


def kernel(x, norm_mix, norm_ffn, norm_final, sb_w_in, sb_w_out, ml_w_in, ml_b_i, ml_b_f, ml_g_head, ml_w_out,
           moe_w_group, moe_b_group, moe_w_expert, moe_b_expert, moe_w1, moe_w3, moe_w2):
    batch, seq, d = x.shape
    xt = x.reshape(batch * seq, d)

    h = rmsnorm_bf16(xt, norm_mix[0])
    qkv = matmul_bf16(h, sb_w_in, 3 * d)
    o = sb_attention(qkv, batch=batch, seq=seq, heads=SB_HEADS, dh=d // SB_HEADS)
    xt, h = _moe(o, sb_w_out[0], xt, norm_ffn[0], moe_w_group[0], moe_b_group[0], moe_w_expert[0],
                 moe_b_expert[0], moe_w1, moe_w3, moe_w2, 0, norm_mix[1], final=False)

    dqk = (d // 2) // ML_HEADS
    dv = d // ML_HEADS
    n_main = 2 * ML_HEADS * dqk + 2 * ML_HEADS * dv
    qkvo = matmul_bf16(h, ml_w_in, n_main)
    gcol, grow = mlstm_gates(h, ml_w_in[0][:, n_main:], ml_b_i[0], ml_b_f[0])
    o = mlstm(qkvo, gcol, grow, ml_g_head[0], batch=batch, seq=seq, heads=ML_HEADS, dqk=dqk, dv=dv)
    out = _moe(o, ml_w_out[0], xt, norm_ffn[1], moe_w_group[1], moe_b_group[1], moe_w_expert[1],
               moe_b_expert[1], moe_w1, moe_w3, moe_w2, 1, norm_final, final=True)
    return out.reshape(batch, seq, d)
```

```python
import functools

import jax
import jax.numpy as jnp
from jax import lax
from jax.experimental import pallas as pl
from jax.experimental.pallas import tpu as pltpu

F32 = jnp.float32
BF16 = jnp.bfloat16
EPS = 1e-6

V7X_VMEM_BYTES = 64 * 1024 * 1024
VMEM_LIMIT_BYTES = V7X_VMEM_BYTES - 8 * 1024 * 1024
LANES = 128
SUBLANES = 8

SB_HEADS = 16
ML_HEADS = 8
N_GROUPS = 4
EXPERTS_PER_GROUP = 8
N_EXPERTS = N_GROUPS * EXPERTS_PER_GROUP
TOP_K = 2
ML_CHUNK = 128
SB_BLOCK = 128
ROUTER_ROWS = 128
LOG2_E = 1.4426950408889634
EXP2_F32_UNDERFLOW = -150.5
DMA_PRIORITIES = 2

NT_DIMS = (((1,), (1,)), ((), ()))


def _params(*sem):
    return pltpu.CompilerParams(dimension_semantics=sem, vmem_limit_bytes=VMEM_LIMIT_BYTES)


def _split2(x):
    hi = x.astype(BF16)
    lo = (x - hi.astype(F32)).astype(BF16)
    return hi, lo


def _split3(x):
    hi = x.astype(BF16)
    r = x - hi.astype(F32)
    mid = r.astype(BF16)
    lo = (r - mid.astype(F32)).astype(BF16)
    return hi, mid, lo


def _rms(x, g):
    return x * lax.rsqrt(jnp.mean(x * x, axis=-1, keepdims=True) + EPS) * g


def _slab_rows(d):
    return d // LANES


def _slab_pitch(d):
    return _slab_rows(d) + 1


def _slab_store(ref, x, n):
    d = x.shape[1]
    pitch = _slab_pitch(d)
    for c in range(_slab_rows(d)):
        ref[pl.ds(c, n, stride=pitch), :] = x[:, c * LANES:(c + 1) * LANES]
    ref[pl.ds(_slab_rows(d), n, stride=pitch), :] = jnp.zeros((n, LANES), x.dtype)


def _slab_load(ref, lead, n, d):
    pitch = _slab_pitch(d)
    return jnp.concatenate([ref[lead + (pl.ds(c, n, stride=pitch), slice(None))] for c in range(_slab_rows(d))],
                           axis=1)


def _rmsnorm_body(x_ref, g_ref, h_ref):
    h_ref[...] = _rms(x_ref[...], g_ref[...]).astype(h_ref.dtype)


def rmsnorm_bf16(x, g, *, tm=512):
    t, d = x.shape
    return pl.pallas_call(
        _rmsnorm_body,
        grid=(t // tm,),
        in_specs=[pl.BlockSpec((tm, d), lambda i: (i, 0)), pl.BlockSpec((1, d), lambda i: (0, 0))],
        out_specs=pl.BlockSpec((tm, d), lambda i: (i, 0)),
        out_shape=jax.ShapeDtypeStruct((t, d), BF16),
        compiler_params=_params("parallel"),
        name="rmsnorm",
    )(x, g.reshape(1, d))


def _mm_body(h_ref, w_ref, o_ref, wb_ref, *, w_rows_are_outputs):
    @pl.when(pl.program_id(1) == 0)
    def _():
        wb_ref[...] = w_ref[...].astype(BF16)

    if w_rows_are_outputs:
        acc = lax.dot_general(h_ref[...], wb_ref[...], NT_DIMS, preferred_element_type=F32)
    else:
        acc = jnp.dot(h_ref[...], wb_ref[...], preferred_element_type=F32)
    o_ref[...] = acc.astype(o_ref.dtype)


def matmul_bf16(h, w, n, *, w_rows_are_outputs=False, tm=1024, tn=1024):
    t, k = h.shape
    if w_rows_are_outputs:
        w_spec, w_block = pl.BlockSpec((None, tn, k), lambda j, i: (0, j, 0)), (tn, k)
    else:
        w_spec, w_block = pl.BlockSpec((None, k, tn), lambda j, i: (0, 0, j)), (k, tn)
    return pl.pallas_call(
        functools.partial(_mm_body, w_rows_are_outputs=w_rows_are_outputs),
        grid=(n // tn, t // tm),
        in_specs=[pl.BlockSpec((tm, k), lambda j, i: (i, 0)), w_spec],
        out_specs=pl.BlockSpec((tm, tn), lambda j, i: (i, j)),
        out_shape=jax.ShapeDtypeStruct((t, n), BF16),
        scratch_shapes=[pltpu.VMEM(w_block, BF16)],
        compiler_params=_params("parallel", "arbitrary"),
        name="proj_in",
    )(h, w)


def _sb_attn_body(q_ref, k_ref, v_ref, cm_ref, o_ref, *, scale, nblk, hg, kc):
    blk = SB_BLOCK
    dh = SB_BLOCK
    wid = kc * blk
    rel = lax.broadcasted_iota(jnp.int32, (blk, wid), 1) - lax.broadcasted_iota(jnp.int32, (blk, wid), 0)

    def chunk(qbs, q0, c, rs, accs, masked):
        k0 = pl.multiple_of(c * wid, wid)
        if masked:
            valid = rel < (q0 - k0)
        z = jnp.concatenate(
            [lax.dot_general(qbs[h], k_ref[pl.ds(k0, wid), h * dh:(h + 1) * dh], NT_DIMS,
                             preferred_element_type=F32) for h in range(hg)], axis=1)
        sp = jnp.maximum(z, 0.0) + jnp.log(1.0 + jnp.exp2(-jnp.abs(z))) * LOG2_E
        log_beta = z - sp
        if masked:
            valid_all = jnp.concatenate([valid] * hg, axis=1)
            sp = jnp.where(valid_all, sp, 0.0)
        nb = hg * kc
        x = jnp.concatenate([sp[:, j * blk:(j + 1) * blk] for j in range(nb)], axis=0).astype(BF16)
        cs = jnp.dot(x, cm_ref[...], preferred_element_type=F32)
        new_rs = []
        suffix = [None] * nb
        for h in range(hg):
            r = rs[h]
            for j in reversed(range(kc)):
                csj = cs[(h * kc + j) * blk:(h * kc + j + 1) * blk]
                suffix[h * kc + j] = csj[:, :blk] + r
                r = r + csj[:, blk:]
            new_rs.append(r)
        a = jnp.exp2(log_beta + jnp.concatenate(suffix, axis=1))
        if masked:
            a = jnp.where(valid_all, a, 0.0)
        a = a.astype(BF16)
        new_accs = [accs[h] + jnp.dot(a[:, h * wid:(h + 1) * wid], v_ref[pl.ds(k0, wid), h * dh:(h + 1) * dh],
                                      preferred_element_type=F32) for h in range(hg)]
        return tuple(new_rs), tuple(new_accs)

    def q_step(qi, carry):
        q0 = pl.multiple_of(qi * blk, blk)
        qbs = [(q_ref[pl.ds(q0, blk), h * dh:(h + 1) * dh].astype(F32) * scale).astype(BF16) for h in range(hg)]
        zeros = tuple(jnp.zeros((blk, blk), F32) for _ in range(hg))
        top = qi // kc
        rs, accs = chunk(qbs, q0, top, zeros, zeros, True)

        def live(rs):
            return jnp.max(functools.reduce(jnp.maximum, rs)) > EXP2_F32_UNDERFLOW

        def k_cond(c):
            return (c[0] < top) & c[1]

        def k_step(c):
            rs, accs = chunk(qbs, q0, top - 1 - c[0], c[2], c[3], False)
            return c[0] + 1, live(rs), rs, accs

        accs = lax.while_loop(k_cond, k_step, (jnp.int32(0), live(rs), rs, accs))[3]
        for h in range(hg):
            o_ref[pl.ds(q0, blk), h * dh:(h + 1) * dh] = accs[h].astype(o_ref.dtype)
        return carry

    lax.fori_loop(0, nblk, q_step, 0)


def _cumsum_matrix(blk):
    j = jnp.arange(blk)[:, None]
    c = jnp.arange(2 * blk)[None, :]
    return jnp.where((c >= blk) | (j > c), -1.0, 0.0).astype(BF16)


def sb_attention(qkv, *, batch, seq, heads, dh, hg=8, kc=2):
    assert dh == SB_BLOCK
    nblk = seq // SB_BLOCK
    assert nblk % kc == 0 and heads % hg == 0
    body = functools.partial(_sb_attn_body, scale=dh ** -0.5 * LOG2_E, nblk=nblk, hg=hg, kc=kc)
    cm = _cumsum_matrix(SB_BLOCK)
    ng = heads // hg
    return pl.pallas_call(
        body,
        grid=(batch, ng),
        in_specs=[
            pl.BlockSpec((seq, hg * dh), lambda b, h: (b, h)),
            pl.BlockSpec((seq, hg * dh), lambda b, h: (b, ng + h)),
            pl.BlockSpec((seq, hg * dh), lambda b, h: (b, 2 * ng + h)),
            pl.BlockSpec((SB_BLOCK, 2 * SB_BLOCK), lambda b, h: (0, 0)),
        ],
        out_specs=pl.BlockSpec((seq, hg * dh), lambda b, h: (b, h)),
        out_shape=jax.ShapeDtypeStruct((batch * seq, heads * dh), BF16),
        compiler_params=_params("parallel", "parallel"),
        name="sb_attn",
    )(qkv, qkv, qkv, cm)


def _proj_route_body(o_ref, w_ref, x_ref, g_ref, wrh_ref, wrl_ref, br_ref, tri_ref,
                     x1_ref, ri_ref, rw_ref, cnt_ref, base_ref, xprev_ref, *, tm):
    i = pl.program_id(0)
    n = pl.num_programs(0) - 1

    @pl.when(i == 0)
    def _():
        base_ref[...] = jnp.zeros_like(base_ref)

    @pl.when(i > 0)
    def _():
        _route_tile(xprev_ref[...], g_ref, wrh_ref, wrl_ref, br_ref, tri_ref, ri_ref, rw_ref, cnt_ref, base_ref, tm=tm)

    @pl.when(i < n)
    def _():
        x1 = x_ref[...] + jnp.dot(o_ref[...], w_ref[...], preferred_element_type=F32)
        x1_ref[...] = x1
        xprev_ref[...] = x1


def _route_tile(x1, g_ref, wrh_ref, wrl_ref, br_ref, tri_ref, ri_ref, rw_ref, cnt_ref, base_ref, *, tm):
    h2 = _rms(x1, g_ref[...])

    hh, hl = _split2(h2)
    wrh = wrh_ref[...]
    lg = (lax.dot_general(wrh, hh, NT_DIMS, preferred_element_type=F32)
          + lax.dot_general(wrh, hl, NT_DIMS, preferred_element_type=F32)
          + lax.dot_general(wrl_ref[...], hh, NT_DIMS, preferred_element_type=F32)) + br_ref[...]

    r8 = lax.broadcasted_iota(jnp.int32, (SUBLANES, tm), 0)
    neg_inf = jnp.float32(-jnp.inf)
    gl = jnp.where(r8 < N_GROUPS, lg[N_EXPERTS:N_EXPERTS + SUBLANES, :], neg_inf)
    gmax = jnp.max(gl, axis=0, keepdims=True)
    gsel = jnp.min(jnp.where(gl == gmax, r8, SUBLANES), axis=0, keepdims=True)
    p_sel = 1.0 / jnp.sum(jnp.exp(gl - gmax), axis=0, keepdims=True)

    es = lg[0:EXPERTS_PER_GROUP, :]
    for g in range(1, N_GROUPS):
        es = jnp.where(gsel == g, lg[g * EXPERTS_PER_GROUP:(g + 1) * EXPERTS_PER_GROUP, :], es)
    m1 = jnp.max(es, axis=0, keepdims=True)
    i1 = jnp.min(jnp.where(es == m1, r8, SUBLANES), axis=0, keepdims=True)
    es2 = jnp.where(r8 == i1, neg_inf, es)
    m2 = jnp.max(es2, axis=0, keepdims=True)
    i2 = jnp.min(jnp.where(es2 == m2, r8, SUBLANES), axis=0, keepdims=True)
    t = jnp.exp(m2 - m1)
    w1 = p_sel / (1.0 + t)
    w2 = w1 * t
    e1 = gsel * EXPERTS_PER_GROUP + i1
    e2 = gsel * EXPERTS_PER_GROUP + i2

    r32 = lax.broadcasted_iota(jnp.int32, (N_EXPERTS, tm), 0)
    hit1 = r32 == e1
    hit2 = r32 == e2
    onehot = (jnp.where(hit1, 1.0, 0.0) + jnp.where(hit2, 1.0, 0.0)).astype(BF16)
    cs = jnp.dot(onehot, tri_ref[...], preferred_element_type=F32)
    base = base_ref[...]
    before = cs[:, :tm] + base
    rank1 = jnp.sum(jnp.where(hit1, before, 0.0), axis=0, keepdims=True)
    rank2 = jnp.sum(jnp.where(hit2, before, 0.0), axis=0, keepdims=True)
    tot = cs[:, tm:]
    new_base = base + jnp.concatenate([tot] * (tm // LANES), axis=1)
    base_ref[...] = new_base
    cnt_ref[...] = new_base[:, :LANES]

    zi = jnp.zeros((SUBLANES, tm), jnp.int32)
    ri = jnp.where(r8 == 0, e1, zi)
    ri = jnp.where(r8 == 1, e2, ri)
    ri = jnp.where(r8 == 2, rank1.astype(jnp.int32), ri)
    ri = jnp.where(r8 == 3, rank2.astype(jnp.int32), ri)
    ri_ref[...] = ri
    zf = jnp.zeros((SUBLANES, tm), F32)
    rw = jnp.where(r8 == 0, w1, zf)
    rw = jnp.where(r8 == 1, w2, rw)
    rw_ref[...] = rw


def proj_route(o, w_out, x, g, w_group, b_group, w_expert, b_expert, *, tm=256):
    t, d = x.shape
    k = o.shape[1]
    wr = jnp.zeros((ROUTER_ROWS, d), F32)
    wr = wr.at[:N_EXPERTS].set(w_expert.T).at[N_EXPERTS:N_EXPERTS + N_GROUPS].set(w_group.T)
    wrh, wrl = _split2(wr)
    br = jnp.zeros((ROUTER_ROWS,), F32).at[:N_EXPERTS].set(b_expert).at[N_EXPERTS:N_EXPERTS + N_GROUPS].set(b_group)
    br = jnp.broadcast_to(br[:, None], (ROUTER_ROWS, tm))
    tp = jnp.arange(tm)[:, None]
    tri = jnp.concatenate([(tp < jnp.arange(tm)[None, :]).astype(BF16), jnp.ones((tm, LANES), BF16)], axis=1)
    body = functools.partial(_proj_route_body, tm=tm)
    n = t // tm
    row = lambda i: (jnp.minimum(i, n - 1), 0)
    routed = lambda i: (0, jnp.maximum(i - 1, 0))
    const = lambda i: (0, 0)
    return pl.pallas_call(
        body,
        grid=(n + 1,),
        in_specs=[
            pl.BlockSpec((tm, k), row),
            pl.BlockSpec((k, d), const),
            pl.BlockSpec((tm, d), row),
            pl.BlockSpec((1, d), const),
            pl.BlockSpec((ROUTER_ROWS, d), const),
            pl.BlockSpec((ROUTER_ROWS, d), const),
            pl.BlockSpec((ROUTER_ROWS, tm), const),
            pl.BlockSpec((tm, tm + LANES), const),
        ],
        out_specs=[
            pl.BlockSpec((tm, d), row),
            pl.BlockSpec((SUBLANES, tm), routed),
            pl.BlockSpec((SUBLANES, tm), routed),
            pl.BlockSpec((N_EXPERTS, LANES), const),
        ],
        out_shape=[
            jax.ShapeDtypeStruct((t, d), F32),
            jax.ShapeDtypeStruct((SUBLANES, t), jnp.int32),
            jax.ShapeDtypeStruct((SUBLANES, t), F32),
            jax.ShapeDtypeStruct((N_EXPERTS, LANES), F32),
        ],
        scratch_shapes=[pltpu.VMEM((N_EXPERTS, tm), F32), pltpu.VMEM((tm, d), F32)],
        compiler_params=_params("arbitrary"),
        name="proj_route",
    )(o, w_out, x, g.reshape(1, d), wrh, wrl, br, tri)


def _dispatch_body(pos_ref, x_ref, g_ref, xs_hbm, hs, sem, *, tm, n_pairs):
    i = pl.program_id(0)
    n = pl.num_programs(0)
    slot = lax.rem(i, 2)
    d = x_ref.shape[1]
    pitch = _slab_pitch(d)

    def row_out(p, r, s):
        return pltpu.make_async_copy(hs.at[s, pl.ds(r * pitch, pitch)], xs_hbm.at[pl.ds(p * pitch, pitch)], sem.at[s])

    def wait_tile(s):
        for k in range(TOP_K):
            for r in range(tm):
                row_out(0, r, s).wait()

    def tail(s):
        return pltpu.make_async_copy(hs.at[s], xs_hbm.at[pl.ds(n_pairs * pitch, tm * pitch)], sem.at[s])

    @pl.when(i >= 2)
    def _():
        wait_tile(slot)

    _slab_store(hs.at[slot], _rms(x_ref[...], g_ref[...]), tm)
    for k in range(TOP_K):
        for r in range(tm):
            row_out(pos_ref[0, 0, k * tm + r], r, slot).start(priority=r % DMA_PRIORITIES)

    @pl.when(i == n - 1)
    def _():
        wait_tile(slot)

        @pl.when(i >= 1)
        def _():
            wait_tile(1 - slot)

        hs[slot] = jnp.zeros(hs.shape[1:], hs.dtype)
        tail(slot).start()
        tail(slot).wait()


def dispatch_rows(x1, g, pos, *, tm=256):
    t, d = x1.shape
    nt = t // tm
    pitch = _slab_pitch(d)
    n_pairs = TOP_K * t
    pos3 = pos.reshape(TOP_K, nt, tm).transpose(1, 0, 2).reshape(nt, 1, TOP_K * tm)
    return pl.pallas_call(
        functools.partial(_dispatch_body, tm=tm, n_pairs=n_pairs),
        grid=(nt,),
        in_specs=[
            pl.BlockSpec((1, 1, TOP_K * tm), lambda i: (i, 0, 0), memory_space=pltpu.SMEM),
            pl.BlockSpec((tm, d), lambda i: (i, 0)),
            pl.BlockSpec((1, d), lambda i: (0, 0)),
        ],
        out_specs=pl.BlockSpec(memory_space=pl.ANY),
        out_shape=jax.ShapeDtypeStruct(((n_pairs + tm) * pitch, LANES), F32),
        scratch_shapes=[pltpu.VMEM((2, tm * pitch, LANES), F32), pltpu.SemaphoreType.DMA((2,))],
        compiler_params=_params("arbitrary"),
        name="dispatch",
    )(pos3, x1, g.reshape(1, d))


def _experts_body(te_ref, ts_ref, nt_ref, xs_hbm, w1_ref, w3_ref, w2_ref, ys_hbm,
                  xbuf, ybuf, w1b, w3b, w2b, isem, osem, *, tm, d, n_pairs):
    i = pl.program_id(0)
    n_tiles = nt_ref[0]
    slot = lax.rem(i, 2)
    pitch = _slab_pitch(d)

    def tile_in(j, s):
        return pltpu.make_async_copy(xs_hbm.at[pl.ds(ts_ref[j] * pitch, tm * pitch)], xbuf.at[s], isem.at[s])

    def tile_out(j, s):
        return pltpu.make_async_copy(ybuf.at[s], ys_hbm.at[pl.ds(ts_ref[j] * pitch, tm * pitch)], osem.at[s])

    @pl.when(i == 0)
    def _():
        tile_in(0, 0).start()
        ybuf[1] = jnp.zeros(ybuf.shape[1:], ybuf.dtype)
        zero_tail = pltpu.make_async_copy(ybuf.at[1], ys_hbm.at[pl.ds(n_pairs * pitch, tm * pitch)], osem.at[1])
        zero_tail.start()
        zero_tail.wait()

    @pl.when(i < n_tiles)
    def _():
        tile_in(i, slot).wait()

        @pl.when(i + 1 < n_tiles)
        def _():
            tile_in(i + 1, 1 - slot).start()

        @pl.when((i == 0) | (te_ref[i] != te_ref[jnp.maximum(i - 1, 0)]))
        def _():
            w1b[...] = w1_ref[0].astype(BF16)
            w3b[...] = w3_ref[0].astype(BF16)
            w2b[...] = w2_ref[0].astype(BF16)

        xb = _slab_load(xbuf, (slot,), tm, d).astype(BF16)
        a = jnp.dot(xb, w1b[...], preferred_element_type=F32)
        b = jnp.dot(xb, w3b[...], preferred_element_type=F32)
        hid = (a * (1.0 / (1.0 + jnp.exp(-a))) * b).astype(BF16)
        _slab_store(ybuf.at[slot], jnp.dot(hid, w2b[...], preferred_element_type=F32), tm)

        @pl.when(i >= 1)
        def _():
            tile_out(i - 1, 1 - slot).wait()

        tile_out(i, slot).start()

        @pl.when(i == n_tiles - 1)
        def _():
            tile_out(i, slot).wait()


def _route_tables(ri, cnt, *, tm, n_tok):
    nt_max = TOP_K * n_tok // tm + N_EXPERTS - 1
    counts = cnt[:, 0].astype(jnp.int32)
    ends = jnp.cumsum(counts)
    offs = ends - counts
    eid = jnp.arange(N_EXPERTS, dtype=jnp.int32)
    sel = ri[0:TOP_K, :, None] == eid
    pos = jnp.sum(jnp.where(sel, offs, 0), axis=-1) + ri[TOP_K:2 * TOP_K]
    tiles = (counts + tm - 1) // tm
    tile_ends = jnp.cumsum(tiles)
    n_tiles = tile_ends[-1].astype(jnp.int32)
    step = jnp.minimum(jnp.arange(nt_max, dtype=jnp.int32), n_tiles - 1)
    te = jnp.clip(jnp.sum((step[:, None] >= tile_ends[None, :]).astype(jnp.int32), axis=1), 0, N_EXPERTS - 1)
    first = jnp.sum(jnp.where(te[:, None] == eid, tile_ends - tiles, 0), axis=-1)
    ts = jnp.sum(jnp.where(te[:, None] == eid, offs, 0), axis=-1) + (step - first) * tm
    return pos, te, ts.astype(jnp.int32), n_tiles.reshape(1), nt_max


def routed_experts(xs, te, ts, n_tiles, w1, w3, w2, layer, *, tm, nt_max, n_tok):
    d, f = w1.shape[-2:]
    pitch = _slab_pitch(d)
    body = functools.partial(_experts_body, tm=tm, d=d, n_pairs=TOP_K * n_tok)
    grid_spec = pltpu.PrefetchScalarGridSpec(
        num_scalar_prefetch=3,
        grid=(nt_max,),
        in_specs=[
            pl.BlockSpec(memory_space=pl.ANY),
            pl.BlockSpec((None, 1, d, f), lambda i, te, ts, nt: (layer, te[i], 0, 0)),
            pl.BlockSpec((None, 1, d, f), lambda i, te, ts, nt: (layer, te[i], 0, 0)),
            pl.BlockSpec((None, 1, f, d), lambda i, te, ts, nt: (layer, te[i], 0, 0)),
        ],
        out_specs=pl.BlockSpec(memory_space=pl.ANY),
        scratch_shapes=[
            pltpu.VMEM((2, tm * pitch, LANES), F32),
            pltpu.VMEM((2, tm * pitch, LANES), F32),
            pltpu.VMEM((d, f), BF16),
            pltpu.VMEM((d, f), BF16),
            pltpu.VMEM((f, d), BF16),
            pltpu.SemaphoreType.DMA((2,)),
            pltpu.SemaphoreType.DMA((2,)),
        ],
    )
    return pl.pallas_call(
        body,
        grid_spec=grid_spec,
        out_shape=jax.ShapeDtypeStruct(xs.shape, F32),
        compiler_params=_params("arbitrary"),
        name="experts",
    )(te, ts, n_tiles, xs, w1, w3, w2)


def _combine_body(pos_ref, nxt_ref, x_ref, w_ref, g_ref, ys_hbm, *rest, tm, final):
    if final:
        h_ref, ybuf, sem = rest
        x2_ref = None
    else:
        x2_ref, h_ref, ybuf, sem = rest
    i = pl.program_id(0)
    n = pl.num_programs(0)
    slot = lax.rem(i, 2)
    d = x_ref.shape[1]
    rows, pitch = _slab_rows(d), _slab_pitch(d)

    def row_in(src, k, r, s):
        return pltpu.make_async_copy(ys_hbm.at[pl.ds(src * pitch, rows)], ybuf.at[s, k, pl.ds(r * pitch, rows)],
                                     sem.at[s])

    def gather_start(idx_ref, s):
        for k in range(TOP_K):
            for r in range(tm):
                row_in(idx_ref[0, 0, k * tm + r], k, r, s).start(priority=r % DMA_PRIORITIES)

    def gather_wait(s):
        for k in range(TOP_K):
            for r in range(tm):
                row_in(0, k, r, s).wait()

    @pl.when(i == 0)
    def _():
        gather_start(pos_ref, 0)

    gather_wait(slot)

    @pl.when(i + 1 < n)
    def _():
        gather_start(nxt_ref, 1 - slot)

    w = w_ref[...]
    x2 = x_ref[...] + w[:, 0:1] * _slab_load(ybuf, (slot, 0), tm, d) + w[:, 1:2] * _slab_load(ybuf, (slot, 1), tm, d)
    if x2_ref is not None:
        x2_ref[...] = x2
    h_ref[...] = _rms(x2, g_ref[...]).astype(h_ref.dtype)


def moe_combine(x1, ys, pos, rw, g, *, final, tm=128):
    t, d = x1.shape
    nt = t // tm
    w = rw[0:TOP_K].T
    pos3 = pos.reshape(TOP_K, nt, tm).transpose(1, 0, 2).reshape(nt, 1, TOP_K * tm)
    row = lambda i: (i, 0)
    smem_tile = lambda imap: pl.BlockSpec((1, 1, TOP_K * tm), imap, memory_space=pltpu.SMEM)
    in_specs = [
        smem_tile(lambda i: (i, 0, 0)),
        smem_tile(lambda i: (jnp.minimum(i + 1, nt - 1), 0, 0)),
        pl.BlockSpec((tm, d), row),
        pl.BlockSpec((tm, TOP_K), row),
        pl.BlockSpec((1, d), lambda i: (0, 0)),
        pl.BlockSpec(memory_space=pl.ANY),
    ]
    scratch = [pltpu.VMEM((2, TOP_K, tm * _slab_pitch(d), LANES), F32), pltpu.SemaphoreType.DMA((2,))]
    args = (pos3, pos3, x1, w, g.reshape(1, d), ys)
    body = functools.partial(_combine_body, tm=tm, final=final)
    if final:
        return pl.pallas_call(
            body, grid=(nt,), in_specs=in_specs,
            out_specs=pl.BlockSpec((tm, d), row),
            out_shape=jax.ShapeDtypeStruct((t, d), F32),
            scratch_shapes=scratch,
            compiler_params=_params("arbitrary"), name="combine_final",
        )(*args)
    return pl.pallas_call(
        body, grid=(nt,), in_specs=in_specs,
        out_specs=[pl.BlockSpec((tm, d), row), pl.BlockSpec((tm, d), row)],
        out_shape=[jax.ShapeDtypeStruct((t, d), F32), jax.ShapeDtypeStruct((t, d), BF16)],
        scratch_shapes=scratch,
        compiler_params=_params("arbitrary"), name="combine",
    )(*args)


def _log_sigmoid(x):
    return jnp.minimum(x, 0.0) - jnp.log(1.0 + jnp.exp(-jnp.abs(x)))


def _gates_body(h_ref, wch_ref, wcl_ref, bc_ref, gc_ref, gr_ref, *, nh):
    h = h_ref[...]
    col = (lax.dot_general(h, wch_ref[...], NT_DIMS, preferred_element_type=F32)
           + lax.dot_general(h, wcl_ref[...], NT_DIMS, preferred_element_type=F32)) + bc_ref[...]
    lane = lax.broadcasted_iota(jnp.int32, col.shape, 1)
    gates = jnp.where(lane < nh, col, _log_sigmoid(col))
    gc_ref[...] = gates
    gr_ref[...] = gates.T[:2 * nh, :]


def mlstm_gates(h, w_if_t, b_i, b_f, *, tm=512):
    t, d = h.shape
    nh = b_i.shape[0]
    wc = jnp.zeros((LANES, d), F32).at[:2 * nh].set(w_if_t)
    wch, wcl = _split2(wc)
    bc = jnp.zeros((1, LANES), F32).at[0, :2 * nh].set(jnp.concatenate([b_i, b_f]))
    const = lambda i: (0, 0)
    return pl.pallas_call(
        functools.partial(_gates_body, nh=nh),
        grid=(t // tm,),
        in_specs=[
            pl.BlockSpec((tm, d), lambda i: (i, 0)),
            pl.BlockSpec((LANES, d), const), pl.BlockSpec((LANES, d), const),
            pl.BlockSpec((1, LANES), const),
        ],
        out_specs=[pl.BlockSpec((tm, LANES), lambda i: (i, 0)), pl.BlockSpec((2 * nh, tm), lambda i: (0, i))],
        out_shape=[jax.ShapeDtypeStruct((t, LANES), F32), jax.ShapeDtypeStruct((2 * nh, t), F32)],
        compiler_params=_params("parallel"),
        name="mlstm_gates",
    )(h, wch, wcl, bc)


def _mlstm_body(q_ref, k_ref, v_ref, o_ref, gc_ref, gr_ref, gh_ref, tril_ref, triu_ref, out_ref,
                c_ref, n_ref, m_ref, *, scale, nchunk):
    L = ML_CHUNK
    row = lax.broadcasted_iota(jnp.int32, (L, L), 0)
    col = lax.broadcasted_iota(jnp.int32, (L, L), 1)
    tmask = col <= row
    c_ref[...] = jnp.zeros_like(c_ref)
    n_ref[...] = jnp.zeros_like(n_ref)
    m_ref[...] = jnp.zeros_like(m_ref)

    def cumsum_exact(parts, tri, left):
        acc = None
        for p in parts:
            d = (jnp.dot(tri, p, preferred_element_type=F32) if left
                 else jnp.dot(p, tri, preferred_element_type=F32))
            acc = d if acc is None else acc + d
        return acc

    def step(c, carry):
        c0 = pl.multiple_of(c * L, L)
        qc = (q_ref[pl.ds(c0, L), :].astype(F32) * scale).astype(BF16)
        kc = k_ref[pl.ds(c0, L), :]
        vc = v_ref[pl.ds(c0, L), :]
        gc = gc_ref[0, pl.ds(c0, L), :]
        gr = gr_ref[0, :, pl.ds(c0, L)]
        i_col, f_col = gc[:, 0:1], gc[:, 1:2]
        i_row, f_row = gr[0:1, :], gr[1:2, :]
        m_prev = m_ref[...]

        b_col = cumsum_exact(_split3(jnp.broadcast_to(f_col, (L, L))), tril_ref[...], True)
        b_row = cumsum_exact(_split3(jnp.broadcast_to(f_row, (SUBLANES, L))), triu_ref[...], False)[0:1, :]
        bt = b_col[:, 0:1]
        b_last = b_col[L - 1:L, 0:1]

        d_mat = jnp.where(tmask, b_col + (i_row - b_row), -1e30)
        m_t = jnp.maximum(bt + m_prev, jnp.max(d_mat, axis=1, keepdims=True))
        e = jnp.where(tmask, jnp.exp(d_mat - m_t), 0.0)
        w = lax.dot_general(qc, kc, NT_DIMS, preferred_element_type=F32) * e
        inter = jnp.exp(bt + m_prev - m_t)
        c_prev = c_ref[...]
        n_prev = n_ref[...]
        num = inter * jnp.dot(qc, c_prev.astype(BF16), preferred_element_type=F32) \
            + jnp.dot(w.astype(BF16), vc, preferred_element_type=F32)
        qn = jnp.sum(qc.astype(F32) * n_prev, axis=1, keepdims=True)
        den = inter * qn + jnp.sum(w, axis=1, keepdims=True)
        h = num / jnp.maximum(jnp.abs(den), jnp.exp(-m_t))

        hs = h * lax.rsqrt(jnp.mean(h * h, axis=1, keepdims=True) + EPS) * gh_ref[...]
        og = o_ref[pl.ds(c0, L), :].astype(F32)
        out_ref[pl.ds(c0, L), :] = (hs / (1.0 + jnp.exp(-og))).astype(out_ref.dtype)

        key_log = b_last - bt + i_col
        m_new = jnp.maximum(b_last + m_prev, jnp.max(key_log, axis=0, keepdims=True))
        decay = jnp.exp(key_log - m_new)
        keep = jnp.exp(b_last + m_prev - m_new)
        kd = kc.astype(F32) * decay
        c_ref[...] = keep * c_prev + jnp.dot(kd.T.astype(BF16), vc, preferred_element_type=F32)
        n_ref[...] = keep * n_prev + jnp.sum(kd, axis=0, keepdims=True)
        m_ref[...] = m_new
        return carry

    lax.fori_loop(0, nchunk, step, 0)


def mlstm(qkvo, gcol, grow, g_head, *, batch, seq, heads, dqk, dv):
    t = batch * seq
    nh = heads
    gc3 = jnp.stack([gcol[:, :nh].T, gcol[:, nh:2 * nh].T], axis=-1)
    gr3 = jnp.stack([grow[:nh], grow[nh:2 * nh]], axis=1)
    L = ML_CHUNK
    nchunk = seq // L
    idx = jnp.arange(L)
    tril = (idx[None, :] <= idx[:, None]).astype(BF16)
    triu = (idx[:, None] <= idx[None, :]).astype(BF16)
    body = functools.partial(_mlstm_body, scale=dqk ** -0.5, nchunk=nchunk)
    v_blk0 = 2 * nh * dqk // dv
    o_blk0 = v_blk0 + nh
    const = lambda b, h: (0, 0)
    return pl.pallas_call(
        body,
        grid=(batch, nh),
        in_specs=[
            pl.BlockSpec((seq, dqk), lambda b, h: (b, h)),
            pl.BlockSpec((seq, dqk), lambda b, h: (b, nh + h)),
            pl.BlockSpec((seq, dv), lambda b, h: (b, v_blk0 + h)),
            pl.BlockSpec((seq, dv), lambda b, h: (b, o_blk0 + h)),
            pl.BlockSpec((1, seq, 2), lambda b, h: (h, b, 0)),
            pl.BlockSpec((1, 2, seq), lambda b, h: (h, 0, b)),
            pl.BlockSpec((1, dv), lambda b, h: (0, h)),
            pl.BlockSpec((L, L), const),
            pl.BlockSpec((L, L), const),
        ],
        out_specs=pl.BlockSpec((seq, dv), lambda b, h: (b, h)),
        out_shape=jax.ShapeDtypeStruct((t, nh * dv), BF16),
        scratch_shapes=[pltpu.VMEM((dqk, dv), F32), pltpu.VMEM((1, dqk), F32), pltpu.VMEM((1, 1), F32)],
        compiler_params=_params("parallel", "parallel"),
        name="mlstm",
    )(qkvo, qkvo, qkvo, qkvo, gc3, gr3, g_head.reshape(1, nh * dv), tril, triu)


def _moe(o, w_out, x, norm_ffn, w_group, b_group, w_expert, b_expert, w1, w3, w2, layer, g_next, *, final,
         tm=256):
    n_tok = x.shape[0]
    x1, ri, rw, cnt = proj_route(o, w_out.astype(BF16), x, norm_ffn, w_group, b_group, w_expert, b_expert)
    pos, te, ts, n_tiles, nt_max = _route_tables(ri, cnt, tm=tm, n_tok=n_tok)
    xs = dispatch_rows(x1, norm_ffn, pos)
    ys = routed_experts(xs, te, ts, n_tiles, w1, w3, w2, layer, tm=tm, nt_max=nt_max, n_tok=n_tok)
    return moe_combine(x1, ys, pos, rw, g_next, final=final)


def kernel(x, norm_mix, norm_ffn, norm_final, sb_w_in, sb_w_out, ml_w_in, ml_b_i, ml_b_f, ml_g_head, ml_w_out,
           moe_w_group, moe_b_group, moe_w_expert, moe_b_expert, moe_w1, moe_w3, moe_w2):
    batch, seq, d = x.shape
    xt = x.reshape(batch * seq, d)

    h = rmsnorm_bf16(xt, norm_mix[0])
    qkv = matmul_bf16(h, sb_w_in, 3 * d)
    o = sb_attention(qkv, batch=batch, seq=seq, heads=SB_HEADS, dh=d // SB_HEADS)
    xt, h = _moe(o, sb_w_out[0], xt, norm_ffn[0], moe_w_group[0], moe_b_group[0], moe_w_expert[0],
                 moe_b_expert[0], moe_w1, moe_w3, moe_w2, 0, norm_mix[1], final=False)

    dqk = (d // 2) // ML_HEADS
    dv = d // ML_HEADS
    n_main = 2 * ML_HEADS * dqk + 2 * ML_HEADS * dv
    w_in_t = jnp.swapaxes(ml_w_in, 1, 2)
    qkvo = matmul_bf16(h, w_in_t, n_main, w_rows_are_outputs=True)
    gcol, grow = mlstm_gates(h, w_in_t[0, n_main:], ml_b_i[0], ml_b_f[0])
    o = mlstm(qkvo, gcol, grow, ml_g_head[0], batch=batch, seq=seq, heads=ML_HEADS, dqk=dqk, dv=dv)
    out = _moe(o, ml_w_out[0], xt, norm_ffn[1], moe_w_group[1], moe_b_group[1], moe_w_expert[1],
               moe_b_expert[1], moe_w1, moe_w3, moe_w2, 1, norm_final, final=True)
    return out.reshape(batch, seq, d)
```

```python
import functools

import jax
import jax.numpy as jnp
from jax import lax
from jax.experimental import pallas as pl
from jax.experimental.pallas import tpu as pltpu

F32 = jnp.float32
BF16 = jnp.bfloat16
EPS = 1e-6

V7X_VMEM_BYTES = 64 * 1024 * 1024
VMEM_LIMIT_BYTES = V7X_VMEM_BYTES - 8 * 1024 * 1024
LANES = 128
SUBLANES = 8

SB_HEADS = 16
ML_HEADS = 8
N_GROUPS = 4
EXPERTS_PER_GROUP = 8
N_EXPERTS = N_GROUPS * EXPERTS_PER_GROUP
TOP_K = 2
ML_CHUNK = 128
SB_BLOCK = 128
ROUTER_ROWS = 128
LOG2_E = 1.4426950408889634
EXP2_F32_UNDERFLOW = -150.5
DMA_PRIORITIES = 2

NT_DIMS = (((1,), (1,)), ((), ()))


def _params(*sem):
    return pltpu.CompilerParams(dimension_semantics=sem, vmem_limit_bytes=VMEM_LIMIT_BYTES)


def _split2(x):
    hi = x.astype(BF16)
    lo = (x - hi.astype(F32)).astype(BF16)
    return hi, lo


def _split3(x):
    hi = x.astype(BF16)
    r = x - hi.astype(F32)
    mid = r.astype(BF16)
    lo = (r - mid.astype(F32)).astype(BF16)
    return hi, mid, lo


def _rms(x, g):
    return x * lax.rsqrt(jnp.mean(x * x, axis=-1, keepdims=True) + EPS) * g


def _slab_rows(d):
    return d // LANES


def _slab_pitch(d):
    return _slab_rows(d) + 1


def _slab_store(ref, x, n):
    d = x.shape[1]
    pitch = _slab_pitch(d)
    for c in range(_slab_rows(d)):
        ref[pl.ds(c, n, stride=pitch), :] = x[:, c * LANES:(c + 1) * LANES]
    ref[pl.ds(_slab_rows(d), n, stride=pitch), :] = jnp.zeros((n, LANES), x.dtype)


def _slab_load(ref, lead, n, d):
    pitch = _slab_pitch(d)
    return jnp.concatenate([ref[lead + (pl.ds(c, n, stride=pitch), slice(None))] for c in range(_slab_rows(d))],
                           axis=1)


def _rmsnorm_body(x_ref, g_ref, h_ref):
    h_ref[...] = _rms(x_ref[...], g_ref[...]).astype(h_ref.dtype)


def rmsnorm_bf16(x, g, *, tm=512):
    t, d = x.shape
    return pl.pallas_call(
        _rmsnorm_body,
        grid=(t // tm,),
        in_specs=[pl.BlockSpec((tm, d), lambda i: (i, 0)), pl.BlockSpec((1, d), lambda i: (0, 0))],
        out_specs=pl.BlockSpec((tm, d), lambda i: (i, 0)),
        out_shape=jax.ShapeDtypeStruct((t, d), BF16),
        compiler_params=_params("parallel"),
        name="rmsnorm",
    )(x, g.reshape(1, d))


def _mm_body(h_ref, w_ref, o_ref, wb_ref, *, w_rows_are_outputs):
    @pl.when(pl.program_id(1) == 0)
    def _():
        wb_ref[...] = w_ref[...].astype(BF16)

    if w_rows_are_outputs:
        acc = lax.dot_general(h_ref[...], wb_ref[...], NT_DIMS, preferred_element_type=F32)
    else:
        acc = jnp.dot(h_ref[...], wb_ref[...], preferred_element_type=F32)
    o_ref[...] = acc.astype(o_ref.dtype)


def matmul_bf16(h, w, n, *, w_rows_are_outputs=False, tm=1024, tn=1024):
    t, k = h.shape
    if w_rows_are_outputs:
        w_spec, w_block = pl.BlockSpec((None, tn, k), lambda j, i: (0, j, 0)), (tn, k)
    else:
        w_spec, w_block = pl.BlockSpec((None, k, tn), lambda j, i: (0, 0, j)), (k, tn)
    return pl.pallas_call(
        functools.partial(_mm_body, w_rows_are_outputs=w_rows_are_outputs),
        grid=(n // tn, t // tm),
        in_specs=[pl.BlockSpec((tm, k), lambda j, i: (i, 0)), w_spec],
        out_specs=pl.BlockSpec((tm, tn), lambda j, i: (i, j)),
        out_shape=jax.ShapeDtypeStruct((t, n), BF16),
        scratch_shapes=[pltpu.VMEM(w_block, BF16)],
        compiler_params=_params("parallel", "arbitrary"),
        name="proj_in",
    )(h, w)


def _sb_attn_body(q_ref, k_ref, v_ref, cm_ref, o_ref, *, scale, nblk, hg, kc):
    blk = SB_BLOCK
    dh = SB_BLOCK
    wid = kc * blk
    rel = lax.broadcasted_iota(jnp.int32, (blk, wid), 1) - lax.broadcasted_iota(jnp.int32, (blk, wid), 0)

    def chunk(qbs, q0, c, rs, accs, masked):
        k0 = pl.multiple_of(c * wid, wid)
        if masked:
            valid = rel < (q0 - k0)
        z = jnp.concatenate(
            [lax.dot_general(qbs[h], k_ref[pl.ds(k0, wid), h * dh:(h + 1) * dh], NT_DIMS,
                             preferred_element_type=F32) for h in range(hg)], axis=1)
        sp = jnp.maximum(z, 0.0) + jnp.log(1.0 + jnp.exp2(-jnp.abs(z))) * LOG2_E
        log_beta = z - sp
        if masked:
            valid_all = jnp.concatenate([valid] * hg, axis=1)
            sp = jnp.where(valid_all, sp, 0.0)
        nb = hg * kc
        x = jnp.concatenate([sp[:, j * blk:(j + 1) * blk] for j in range(nb)], axis=0).astype(BF16)
        cs = jnp.dot(x, cm_ref[...], preferred_element_type=F32)
        new_rs = []
        suffix = [None] * nb
        for h in range(hg):
            r = rs[h]
            for j in reversed(range(kc)):
                csj = cs[(h * kc + j) * blk:(h * kc + j + 1) * blk]
                suffix[h * kc + j] = csj[:, :blk] + r
                r = r + csj[:, blk:]
            new_rs.append(r)
        a = jnp.exp2(log_beta + jnp.concatenate(suffix, axis=1))
        if masked:
            a = jnp.where(valid_all, a, 0.0)
        a = a.astype(BF16)
        new_accs = [accs[h] + jnp.dot(a[:, h * wid:(h + 1) * wid], v_ref[pl.ds(k0, wid), h * dh:(h + 1) * dh],
                                      preferred_element_type=F32) for h in range(hg)]
        return tuple(new_rs), tuple(new_accs)

    def q_step(qi, carry):
        q0 = pl.multiple_of(qi * blk, blk)
        qbs = [(q_ref[pl.ds(q0, blk), h * dh:(h + 1) * dh].astype(F32) * scale).astype(BF16) for h in range(hg)]
        zeros = tuple(jnp.zeros((blk, blk), F32) for _ in range(hg))
        top = qi // kc
        rs, accs = chunk(qbs, q0, top, zeros, zeros, True)

        def live(rs):
            return jnp.max(functools.reduce(jnp.maximum, rs)) > EXP2_F32_UNDERFLOW

        def k_cond(c):
            return (c[0] < top) & c[1]

        def k_step(c):
            rs, accs = chunk(qbs, q0, top - 1 - c[0], c[2], c[3], False)
            return c[0] + 1, live(rs), rs, accs

        accs = lax.while_loop(k_cond, k_step, (jnp.int32(0), live(rs), rs, accs))[3]
        for h in range(hg):
            o_ref[pl.ds(q0, blk), h * dh:(h + 1) * dh] = accs[h].astype(o_ref.dtype)
        return carry

    lax.fori_loop(0, nblk, q_step, 0)


def _cumsum_matrix(blk):
    j = jnp.arange(blk)[:, None]
    c = jnp.arange(2 * blk)[None, :]
    return jnp.where((c >= blk) | (j > c), -1.0, 0.0).astype(BF16)


def sb_attention(qkv, *, batch, seq, heads, dh, hg=8, kc=2):
    assert dh == SB_BLOCK
    nblk = seq // SB_BLOCK
    assert nblk % kc == 0 and heads % hg == 0
    body = functools.partial(_sb_attn_body, scale=dh ** -0.5 * LOG2_E, nblk=nblk, hg=hg, kc=kc)
    cm = _cumsum_matrix(SB_BLOCK)
    ng = heads // hg
    return pl.pallas_call(
        body,
        grid=(batch, ng),
        in_specs=[
            pl.BlockSpec((seq, hg * dh), lambda b, h: (b, h)),
            pl.BlockSpec((seq, hg * dh), lambda b, h: (b, ng + h)),
            pl.BlockSpec((seq, hg * dh), lambda b, h: (b, 2 * ng + h)),
            pl.BlockSpec((SB_BLOCK, 2 * SB_BLOCK), lambda b, h: (0, 0)),
        ],
        out_specs=pl.BlockSpec((seq, hg * dh), lambda b, h: (b, h)),
        out_shape=jax.ShapeDtypeStruct((batch * seq, heads * dh), BF16),
        compiler_params=_params("parallel", "parallel"),
        name="sb_attn",
    )(qkv, qkv, qkv, cm)


def _proj_route_body(o_ref, w_ref, x_ref, g_ref, wrh_ref, wrl_ref, br_ref, tri_ref,
                     x1_ref, ri_ref, rw_ref, cnt_ref, base_ref, xprev_ref, *, tm):
    i = pl.program_id(0)
    n = pl.num_programs(0) - 1

    @pl.when(i == 0)
    def _():
        base_ref[...] = jnp.zeros_like(base_ref)

    @pl.when(i > 0)
    def _():
        _route_tile(xprev_ref[...], g_ref, wrh_ref, wrl_ref, br_ref, tri_ref, ri_ref, rw_ref, cnt_ref, base_ref, tm=tm)

    @pl.when(i < n)
    def _():
        x1 = x_ref[...] + jnp.dot(o_ref[...], w_ref[...], preferred_element_type=F32)
        x1_ref[...] = x1
        xprev_ref[...] = x1


def _route_tile(x1, g_ref, wrh_ref, wrl_ref, br_ref, tri_ref, ri_ref, rw_ref, cnt_ref, base_ref, *, tm):
    h2 = _rms(x1, g_ref[...])

    hh, hl = _split2(h2)
    wrh = wrh_ref[...]
    lg = (lax.dot_general(wrh, hh, NT_DIMS, preferred_element_type=F32)
          + lax.dot_general(wrh, hl, NT_DIMS, preferred_element_type=F32)
          + lax.dot_general(wrl_ref[...], hh, NT_DIMS, preferred_element_type=F32)) + br_ref[...]

    r8 = lax.broadcasted_iota(jnp.int32, (SUBLANES, tm), 0)
    neg_inf = jnp.float32(-jnp.inf)
    gl = jnp.where(r8 < N_GROUPS, lg[N_EXPERTS:N_EXPERTS + SUBLANES, :], neg_inf)
    gmax = jnp.max(gl, axis=0, keepdims=True)
    gsel = jnp.min(jnp.where(gl == gmax, r8, SUBLANES), axis=0, keepdims=True)
    p_sel = 1.0 / jnp.sum(jnp.exp(gl - gmax), axis=0, keepdims=True)

    es = lg[0:EXPERTS_PER_GROUP, :]
    for g in range(1, N_GROUPS):
        es = jnp.where(gsel == g, lg[g * EXPERTS_PER_GROUP:(g + 1) * EXPERTS_PER_GROUP, :], es)
    m1 = jnp.max(es, axis=0, keepdims=True)
    i1 = jnp.min(jnp.where(es == m1, r8, SUBLANES), axis=0, keepdims=True)
    es2 = jnp.where(r8 == i1, neg_inf, es)
    m2 = jnp.max(es2, axis=0, keepdims=True)
    i2 = jnp.min(jnp.where(es2 == m2, r8, SUBLANES), axis=0, keepdims=True)
    t = jnp.exp(m2 - m1)
    w1 = p_sel / (1.0 + t)
    w2 = w1 * t
    e1 = gsel * EXPERTS_PER_GROUP + i1
    e2 = gsel * EXPERTS_PER_GROUP + i2

    r32 = lax.broadcasted_iota(jnp.int32, (N_EXPERTS, tm), 0)
    hit1 = r32 == e1
    hit2 = r32 == e2
    onehot = (jnp.where(hit1, 1.0, 0.0) + jnp.where(hit2, 1.0, 0.0)).astype(BF16)
    cs = jnp.dot(onehot, tri_ref[...], preferred_element_type=F32)
    base = base_ref[...]
    before = cs[:, :tm] + base
    rank1 = jnp.sum(jnp.where(hit1, before, 0.0), axis=0, keepdims=True)
    rank2 = jnp.sum(jnp.where(hit2, before, 0.0), axis=0, keepdims=True)
    tot = cs[:, tm:]
    new_base = base + jnp.concatenate([tot] * (tm // LANES), axis=1)
    base_ref[...] = new_base
    cnt_ref[...] = new_base[:, :LANES]

    zi = jnp.zeros((SUBLANES, tm), jnp.int32)
    ri = jnp.where(r8 == 0, e1, zi)
    ri = jnp.where(r8 == 1, e2, ri)
    ri = jnp.where(r8 == 2, rank1.astype(jnp.int32), ri)
    ri = jnp.where(r8 == 3, rank2.astype(jnp.int32), ri)
    ri_ref[...] = ri
    zf = jnp.zeros((SUBLANES, tm), F32)
    rw = jnp.where(r8 == 0, w1, zf)
    rw = jnp.where(r8 == 1, w2, rw)
    rw_ref[...] = rw


def proj_route(o, w_out, x, g, w_group, b_group, w_expert, b_expert, *, tm=256):
    t, d = x.shape
    k = o.shape[1]
    wr = jnp.zeros((ROUTER_ROWS, d), F32)
    wr = wr.at[:N_EXPERTS].set(w_expert.T).at[N_EXPERTS:N_EXPERTS + N_GROUPS].set(w_group.T)
    wrh, wrl = _split2(wr)
    br = jnp.zeros((ROUTER_ROWS,), F32).at[:N_EXPERTS].set(b_expert).at[N_EXPERTS:N_EXPERTS + N_GROUPS].set(b_group)
    br = jnp.broadcast_to(br[:, None], (ROUTER_ROWS, tm))
    tp = jnp.arange(tm)[:, None]
    tri = jnp.concatenate([(tp < jnp.arange(tm)[None, :]).astype(BF16), jnp.ones((tm, LANES), BF16)], axis=1)
    body = functools.partial(_proj_route_body, tm=tm)
    n = t // tm
    row = lambda i: (jnp.minimum(i, n - 1), 0)
    routed = lambda i: (0, jnp.maximum(i - 1, 0))
    const = lambda i: (0, 0)
    return pl.pallas_call(
        body,
        grid=(n + 1,),
        in_specs=[
            pl.BlockSpec((tm, k), row),
            pl.BlockSpec((k, d), const),
            pl.BlockSpec((tm, d), row),
            pl.BlockSpec((1, d), const),
            pl.BlockSpec((ROUTER_ROWS, d), const),
            pl.BlockSpec((ROUTER_ROWS, d), const),
            pl.BlockSpec((ROUTER_ROWS, tm), const),
            pl.BlockSpec((tm, tm + LANES), const),
        ],
        out_specs=[
            pl.BlockSpec((tm, d), row),
            pl.BlockSpec((SUBLANES, tm), routed),
            pl.BlockSpec((SUBLANES, tm), routed),
            pl.BlockSpec((N_EXPERTS, LANES), const),
        ],
        out_shape=[
            jax.ShapeDtypeStruct((t, d), F32),
            jax.ShapeDtypeStruct((SUBLANES, t), jnp.int32),
            jax.ShapeDtypeStruct((SUBLANES, t), F32),
            jax.ShapeDtypeStruct((N_EXPERTS, LANES), F32),
        ],
        scratch_shapes=[pltpu.VMEM((N_EXPERTS, tm), F32), pltpu.VMEM((tm, d), F32)],
        compiler_params=_params("arbitrary"),
        name="proj_route",
    )(o, w_out, x, g.reshape(1, d), wrh, wrl, br, tri)


def _dispatch_body(pos_ref, x_ref, g_ref, xs_hbm, hs, sem, *, tm, n_pairs):
    i = pl.program_id(0)
    n = pl.num_programs(0)
    slot = lax.rem(i, 2)
    d = x_ref.shape[1]
    pitch = _slab_pitch(d)

    def row_out(p, r, s):
        return pltpu.make_async_copy(hs.at[s, pl.ds(r * pitch, pitch)], xs_hbm.at[pl.ds(p * pitch, pitch)], sem.at[s])

    def wait_tile(s):
        for k in range(TOP_K):
            for r in range(tm):
                row_out(0, r, s).wait()

    def tail(s):
        return pltpu.make_async_copy(hs.at[s], xs_hbm.at[pl.ds(n_pairs * pitch, tm * pitch)], sem.at[s])

    @pl.when(i >= 2)
    def _():
        wait_tile(slot)

    _slab_store(hs.at[slot], _rms(x_ref[...], g_ref[...]), tm)
    for k in range(TOP_K):
        for r in range(tm):
            row_out(pos_ref[0, 0, k * tm + r], r, slot).start(priority=r % DMA_PRIORITIES)

    @pl.when(i == n - 1)
    def _():
        wait_tile(slot)

        @pl.when(i >= 1)
        def _():
            wait_tile(1 - slot)

        hs[slot] = jnp.zeros(hs.shape[1:], hs.dtype)
        tail(slot).start()
        tail(slot).wait()


def dispatch_rows(x1, g, pos, *, tm=256):
    t, d = x1.shape
    nt = t // tm
    pitch = _slab_pitch(d)
    n_pairs = TOP_K * t
    pos3 = pos.reshape(TOP_K, nt, tm).transpose(1, 0, 2).reshape(nt, 1, TOP_K * tm)
    return pl.pallas_call(
        functools.partial(_dispatch_body, tm=tm, n_pairs=n_pairs),
        grid=(nt,),
        in_specs=[
            pl.BlockSpec((1, 1, TOP_K * tm), lambda i: (i, 0, 0), memory_space=pltpu.SMEM),
            pl.BlockSpec((tm, d), lambda i: (i, 0)),
            pl.BlockSpec((1, d), lambda i: (0, 0)),
        ],
        out_specs=pl.BlockSpec(memory_space=pl.ANY),
        out_shape=jax.ShapeDtypeStruct(((n_pairs + tm) * pitch, LANES), F32),
        scratch_shapes=[pltpu.VMEM((2, tm * pitch, LANES), F32), pltpu.SemaphoreType.DMA((2,))],
        compiler_params=_params("arbitrary"),
        name="dispatch",
    )(pos3, x1, g.reshape(1, d))


def _experts_body(te_ref, ts_ref, nt_ref, xs_hbm, w1_ref, w3_ref, w2_ref, ys_hbm,
                  xbuf, ybuf, w1b, w3b, w2b, isem, osem, *, tm, d, n_pairs):
    i = pl.program_id(0)
    n_tiles = nt_ref[0]
    slot = lax.rem(i, 2)
    pitch = _slab_pitch(d)

    def tile_in(j, s):
        return pltpu.make_async_copy(xs_hbm.at[pl.ds(ts_ref[j] * pitch, tm * pitch)], xbuf.at[s], isem.at[s])

    def tile_out(j, s):
        return pltpu.make_async_copy(ybuf.at[s], ys_hbm.at[pl.ds(ts_ref[j] * pitch, tm * pitch)], osem.at[s])

    @pl.when(i == 0)
    def _():
        tile_in(0, 0).start()
        ybuf[1] = jnp.zeros(ybuf.shape[1:], ybuf.dtype)
        zero_tail = pltpu.make_async_copy(ybuf.at[1], ys_hbm.at[pl.ds(n_pairs * pitch, tm * pitch)], osem.at[1])
        zero_tail.start()
        zero_tail.wait()

    @pl.when(i < n_tiles)
    def _():
        tile_in(i, slot).wait()

        @pl.when(i + 1 < n_tiles)
        def _():
            tile_in(i + 1, 1 - slot).start()

        @pl.when((i == 0) | (te_ref[i] != te_ref[jnp.maximum(i - 1, 0)]))
        def _():
            w1b[...] = w1_ref[0].astype(BF16)
            w3b[...] = w3_ref[0].astype(BF16)
            w2b[...] = w2_ref[0].astype(BF16)

        xb = _slab_load(xbuf, (slot,), tm, d).astype(BF16)
        a = jnp.dot(xb, w1b[...], preferred_element_type=F32)
        b = jnp.dot(xb, w3b[...], preferred_element_type=F32)
        hid = (a * (1.0 / (1.0 + jnp.exp(-a))) * b).astype(BF16)
        _slab_store(ybuf.at[slot], jnp.dot(hid, w2b[...], preferred_element_type=F32), tm)

        @pl.when(i >= 1)
        def _():
            tile_out(i - 1, 1 - slot).wait()

        tile_out(i, slot).start()

        @pl.when(i == n_tiles - 1)
        def _():
            tile_out(i, slot).wait()


def _route_tables(ri, cnt, *, tm, n_tok):
    nt_max = TOP_K * n_tok // tm + N_EXPERTS - 1
    counts = cnt[:, 0].astype(jnp.int32)
    ends = jnp.cumsum(counts)
    offs = ends - counts
    eid = jnp.arange(N_EXPERTS, dtype=jnp.int32)
    sel = ri[0:TOP_K, :, None] == eid
    pos = jnp.sum(jnp.where(sel, offs, 0), axis=-1) + ri[TOP_K:2 * TOP_K]
    tiles = (counts + tm - 1) // tm
    tile_ends = jnp.cumsum(tiles)
    n_tiles = tile_ends[-1].astype(jnp.int32)
    step = jnp.minimum(jnp.arange(nt_max, dtype=jnp.int32), n_tiles - 1)
    te = jnp.clip(jnp.sum((step[:, None] >= tile_ends[None, :]).astype(jnp.int32), axis=1), 0, N_EXPERTS - 1)
    first = jnp.sum(jnp.where(te[:, None] == eid, tile_ends - tiles, 0), axis=-1)
    ts = jnp.sum(jnp.where(te[:, None] == eid, offs, 0), axis=-1) + (step - first) * tm
    return pos, te, ts.astype(jnp.int32), n_tiles.reshape(1), nt_max


def routed_experts(xs, te, ts, n_tiles, w1, w3, w2, layer, *, tm, nt_max, n_tok):
    d, f = w1.shape[-2:]
    pitch = _slab_pitch(d)
    body = functools.partial(_experts_body, tm=tm, d=d, n_pairs=TOP_K * n_tok)
    grid_spec = pltpu.PrefetchScalarGridSpec(
        num_scalar_prefetch=3,
        grid=(nt_max,),
        in_specs=[
            pl.BlockSpec(memory_space=pl.ANY),
            pl.BlockSpec((None, 1, d, f), lambda i, te, ts, nt: (layer, te[i], 0, 0)),
            pl.BlockSpec((None, 1, d, f), lambda i, te, ts, nt: (layer, te[i], 0, 0)),
            pl.BlockSpec((None, 1, f, d), lambda i, te, ts, nt: (layer, te[i], 0, 0)),
        ],
        out_specs=pl.BlockSpec(memory_space=pl.ANY),
        scratch_shapes=[
            pltpu.VMEM((2, tm * pitch, LANES), F32),
            pltpu.VMEM((2, tm * pitch, LANES), F32),
            pltpu.VMEM((d, f), BF16),
            pltpu.VMEM((d, f), BF16),
            pltpu.VMEM((f, d), BF16),
            pltpu.SemaphoreType.DMA((2,)),
            pltpu.SemaphoreType.DMA((2,)),
        ],
    )
    return pl.pallas_call(
        body,
        grid_spec=grid_spec,
        out_shape=jax.ShapeDtypeStruct(xs.shape, F32),
        compiler_params=_params("arbitrary"),
        name="experts",
    )(te, ts, n_tiles, xs, w1, w3, w2)


def _combine_body(pos_ref, nxt_ref, x_ref, w_ref, g_ref, ys_hbm, *rest, tm, final):
    if final:
        h_ref, ybuf, sem = rest
        x2_ref = None
    else:
        x2_ref, h_ref, ybuf, sem = rest
    i = pl.program_id(0)
    n = pl.num_programs(0)
    slot = lax.rem(i, 2)
    d = x_ref.shape[1]
    rows, pitch = _slab_rows(d), _slab_pitch(d)

    def row_in(src, k, r, s):
        return pltpu.make_async_copy(ys_hbm.at[pl.ds(src * pitch, rows)], ybuf.at[s, k, pl.ds(r * pitch, rows)],
                                     sem.at[s])

    def gather_start(idx_ref, s):
        for k in range(TOP_K):
            for r in range(tm):
                row_in(idx_ref[0, 0, k * tm + r], k, r, s).start(priority=r % DMA_PRIORITIES)

    def gather_wait(s):
        for k in range(TOP_K):
            for r in range(tm):
                row_in(0, k, r, s).wait()

    @pl.when(i == 0)
    def _():
        gather_start(pos_ref, 0)

    gather_wait(slot)

    @pl.when(i + 1 < n)
    def _():
        gather_start(nxt_ref, 1 - slot)

    w = w_ref[...]
    x2 = x_ref[...] + w[:, 0:1] * _slab_load(ybuf, (slot, 0), tm, d) + w[:, 1:2] * _slab_load(ybuf, (slot, 1), tm, d)
    if x2_ref is not None:
        x2_ref[...] = x2
    h_ref[...] = _rms(x2, g_ref[...]).astype(h_ref.dtype)


def moe_combine(x1, ys, pos, rw, g, *, final, tm=128):
    t, d = x1.shape
    nt = t // tm
    w = rw[0:TOP_K].T
    pos3 = pos.reshape(TOP_K, nt, tm).transpose(1, 0, 2).reshape(nt, 1, TOP_K * tm)
    row = lambda i: (i, 0)
    smem_tile = lambda imap: pl.BlockSpec((1, 1, TOP_K * tm), imap, memory_space=pltpu.SMEM)
    in_specs = [
        smem_tile(lambda i: (i, 0, 0)),
        smem_tile(lambda i: (jnp.minimum(i + 1, nt - 1), 0, 0)),
        pl.BlockSpec((tm, d), row),
        pl.BlockSpec((tm, TOP_K), row),
        pl.BlockSpec((1, d), lambda i: (0, 0)),
        pl.BlockSpec(memory_space=pl.ANY),
    ]
    scratch = [pltpu.VMEM((2, TOP_K, tm * _slab_pitch(d), LANES), F32), pltpu.SemaphoreType.DMA((2,))]
    args = (pos3, pos3, x1, w, g.reshape(1, d), ys)
    body = functools.partial(_combine_body, tm=tm, final=final)
    if final:
        return pl.pallas_call(
            body, grid=(nt,), in_specs=in_specs,
            out_specs=pl.BlockSpec((tm, d), row),
            out_shape=jax.ShapeDtypeStruct((t, d), F32),
            scratch_shapes=scratch,
            compiler_params=_params("arbitrary"), name="combine_final",
        )(*args)
    return pl.pallas_call(
        body, grid=(nt,), in_specs=in_specs,
        out_specs=[pl.BlockSpec((tm, d), row), pl.BlockSpec((tm, d), row)],
        out_shape=[jax.ShapeDtypeStruct((t, d), F32), jax.ShapeDtypeStruct((t, d), BF16)],
        scratch_shapes=scratch,
        compiler_params=_params("arbitrary"), name="combine",
    )(*args)


def _log_sigmoid(x):
    return jnp.minimum(x, 0.0) - jnp.log(1.0 + jnp.exp(-jnp.abs(x)))


def _gates_body(h_ref, wch_ref, wcl_ref, bc_ref, gc_ref, gr_ref, *, nh):
    h = h_ref[...]
    col = (lax.dot_general(h, wch_ref[...], NT_DIMS, preferred_element_type=F32)
           + lax.dot_general(h, wcl_ref[...], NT_DIMS, preferred_element_type=F32)) + bc_ref[...]
    lane = lax.broadcasted_iota(jnp.int32, col.shape, 1)
    gates = jnp.where(lane < nh, col, _log_sigmoid(col))
    gc_ref[...] = gates
    gr_ref[...] = gates.T[:2 * nh, :]


def mlstm_gates(h, w_if_t, b_i, b_f, *, tm=512):
    t, d = h.shape
    nh = b_i.shape[0]
    wc = jnp.zeros((LANES, d), F32).at[:2 * nh].set(w_if_t)
    wch, wcl = _split2(wc)
    bc = jnp.zeros((1, LANES), F32).at[0, :2 * nh].set(jnp.concatenate([b_i, b_f]))
    const = lambda i: (0, 0)
    return pl.pallas_call(
        functools.partial(_gates_body, nh=nh),
        grid=(t // tm,),
        in_specs=[
            pl.BlockSpec((tm, d), lambda i: (i, 0)),
            pl.BlockSpec((LANES, d), const), pl.BlockSpec((LANES, d), const),
            pl.BlockSpec((1, LANES), const),
        ],
        out_specs=[pl.BlockSpec((tm, LANES), lambda i: (i, 0)), pl.BlockSpec((2 * nh, tm), lambda i: (0, i))],
        out_shape=[jax.ShapeDtypeStruct((t, LANES), F32), jax.ShapeDtypeStruct((2 * nh, t), F32)],
        compiler_params=_params("parallel"),
        name="mlstm_gates",
    )(h, wch, wcl, bc)


def _mlstm_body(q_ref, k_ref, v_ref, o_ref, gc_ref, gr_ref, gh_ref, tril_ref, triu_ref, out_ref,
                c_ref, n_ref, m_ref, *, scale, nchunk, hg, dqk, dv):
    L = ML_CHUNK
    row = lax.broadcasted_iota(jnp.int32, (L, L), 0)
    col = lax.broadcasted_iota(jnp.int32, (L, L), 1)
    heads = range(hg)
    stack = lambda parts: jnp.concatenate(parts, axis=0)
    rows_of = lambda x, h: x[h * L:(h + 1) * L]
    per_row = lambda scalars: stack([jnp.broadcast_to(s, (L, 1)) for s in scalars])
    tmask = stack([col <= row] * hg)
    c_ref[...] = jnp.zeros_like(c_ref)
    n_ref[...] = jnp.zeros_like(n_ref)
    m_ref[...] = jnp.zeros_like(m_ref)

    def cumsum_exact(parts, tri, left):
        acc = None
        for p in parts:
            d = (jnp.dot(tri, p, preferred_element_type=F32) if left
                 else jnp.dot(p, tri, preferred_element_type=F32))
            acc = d if acc is None else acc + d
        return acc

    def step(c, carry):
        c0 = pl.multiple_of(c * L, L)
        qc = [(q_ref[pl.ds(c0, L), h * dqk:(h + 1) * dqk].astype(F32) * scale).astype(BF16) for h in heads]
        kc = [k_ref[pl.ds(c0, L), h * dqk:(h + 1) * dqk] for h in heads]
        vc = [v_ref[pl.ds(c0, L), h * dv:(h + 1) * dv] for h in heads]
        gc = stack([gc_ref[h, pl.ds(c0, L), :] for h in heads])
        i_col, f_col = gc[:, 0:1], gc[:, 1:2]
        m_col = m_ref[...]
        m_prev = [m_col[h * L:h * L + 1] for h in heads]

        f_parts = _split3(jnp.broadcast_to(f_col, (hg * L, L)))
        b_col = stack([cumsum_exact([rows_of(p, h) for p in f_parts], tril_ref[...], True) for h in heads])
        u_rows = []
        for h in heads:
            gr = gr_ref[h, :, pl.ds(c0, L)]
            b_row = cumsum_exact(_split3(jnp.broadcast_to(gr[1:2, :], (SUBLANES, L))), triu_ref[...], False)[0:1, :]
            u_rows.append(jnp.broadcast_to(gr[0:1, :] - b_row, (L, L)))
        bt = b_col[:, 0:1]
        b_last = [b_col[h * L + L - 1:(h + 1) * L, 0:1] for h in heads]

        d_mat = jnp.where(tmask, b_col + stack(u_rows), -1e30)
        m_t = jnp.maximum(bt + m_col, jnp.max(d_mat, axis=1, keepdims=True))
        e = jnp.where(tmask, jnp.exp(d_mat - m_t), 0.0)
        w = stack([lax.dot_general(qc[h], kc[h], NT_DIMS, preferred_element_type=F32) for h in heads]) * e
        wb = w.astype(BF16)
        inter = jnp.exp(bt + m_col - m_t)
        c_prev = [c_ref[h] for h in heads]
        n_prev = [n_ref[h] for h in heads]
        num = inter * stack([jnp.dot(qc[h], c_prev[h].astype(BF16), preferred_element_type=F32) for h in heads]) \
            + stack([jnp.dot(rows_of(wb, h), vc[h], preferred_element_type=F32) for h in heads])
        qn = jnp.sum(stack([qc[h].astype(F32) * n_prev[h] for h in heads]), axis=1, keepdims=True)
        den = inter * qn + jnp.sum(w, axis=1, keepdims=True)
        hh = num / jnp.maximum(jnp.abs(den), jnp.exp(-m_t))

        gains = stack([jnp.broadcast_to(gh_ref[:, h * dv:(h + 1) * dv], (L, dv)) for h in heads])
        hs = hh * lax.rsqrt(jnp.mean(hh * hh, axis=1, keepdims=True) + EPS) * gains
        og = stack([o_ref[pl.ds(c0, L), h * dv:(h + 1) * dv] for h in heads]).astype(F32)
        gated = (hs / (1.0 + jnp.exp(-og))).astype(out_ref.dtype)
        for h in heads:
            out_ref[pl.ds(c0, L), h * dv:(h + 1) * dv] = rows_of(gated, h)

        key_log = per_row(b_last) - bt + i_col
        m_new = [jnp.maximum(b_last[h] + m_prev[h], jnp.max(rows_of(key_log, h), axis=0, keepdims=True))
                 for h in heads]
        m_new_col = per_row(m_new)
        kd = stack([kc[h].astype(F32) for h in heads]) * jnp.exp(key_log - m_new_col)
        for h in heads:
            keep = jnp.exp(b_last[h] + m_prev[h] - m_new[h])
            kd_h = rows_of(kd, h)
            c_ref[h] = keep * c_prev[h] + jnp.dot(kd_h.T.astype(BF16), vc[h], preferred_element_type=F32)
            n_ref[h] = keep * n_prev[h] + jnp.sum(kd_h, axis=0, keepdims=True)
        m_ref[...] = m_new_col
        return carry

    lax.fori_loop(0, nchunk, step, 0)


def mlstm(qkvo, gcol, grow, g_head, *, batch, seq, heads, dqk, dv, hg=4):
    t = batch * seq
    nh = heads
    gc3 = jnp.stack([gcol[:, :nh].T, gcol[:, nh:2 * nh].T], axis=-1)
    gr3 = jnp.stack([grow[:nh], grow[nh:2 * nh]], axis=1)
    L = ML_CHUNK
    nchunk = seq // L
    idx = jnp.arange(L)
    tril = (idx[None, :] <= idx[:, None]).astype(BF16)
    triu = (idx[:, None] <= idx[None, :]).astype(BF16)
    assert nh % hg == 0
    ng = nh // hg
    body = functools.partial(_mlstm_body, scale=dqk ** -0.5, nchunk=nchunk, hg=hg, dqk=dqk, dv=dv)
    v_blk0 = 2 * nh * dqk // (hg * dv)
    o_blk0 = v_blk0 + ng
    const = lambda b, g: (0, 0)
    return pl.pallas_call(
        body,
        grid=(batch, ng),
        in_specs=[
            pl.BlockSpec((seq, hg * dqk), lambda b, g: (b, g)),
            pl.BlockSpec((seq, hg * dqk), lambda b, g: (b, ng + g)),
            pl.BlockSpec((seq, hg * dv), lambda b, g: (b, v_blk0 + g)),
            pl.BlockSpec((seq, hg * dv), lambda b, g: (b, o_blk0 + g)),
            pl.BlockSpec((hg, seq, 2), lambda b, g: (g, b, 0)),
            pl.BlockSpec((hg, 2, seq), lambda b, g: (g, 0, b)),
            pl.BlockSpec((1, hg * dv), lambda b, g: (0, g)),
            pl.BlockSpec((L, L), const),
            pl.BlockSpec((L, L), const),
        ],
        out_specs=pl.BlockSpec((seq, hg * dv), lambda b, g: (b, g)),
        out_shape=jax.ShapeDtypeStruct((t, nh * dv), BF16),
        scratch_shapes=[pltpu.VMEM((hg, dqk, dv), F32), pltpu.VMEM((hg, 1, dqk), F32), pltpu.VMEM((hg * L, 1), F32)],
        compiler_params=_params("parallel", "parallel"),
        name="mlstm",
    )(qkvo, qkvo, qkvo, qkvo, gc3, gr3, g_head.reshape(1, nh * dv), tril, triu)


def _moe(o, w_out, x, norm_ffn, w_group, b_group, w_expert, b_expert, w1, w3, w2, layer, g_next, *, final,
         tm=256):
    n_tok = x.shape[0]
    x1, ri, rw, cnt = proj_route(o, w_out.astype(BF16), x, norm_ffn, w_group, b_group, w_expert, b_expert)
    pos, te, ts, n_tiles, nt_max = _route_tables(ri, cnt, tm=tm, n_tok=n_tok)
    xs = dispatch_rows(x1, norm_ffn, pos)
    ys = routed_experts(xs, te, ts, n_tiles, w1, w3, w2, layer, tm=tm, nt_max=nt_max, n_tok=n_tok)
    return moe_combine(x1, ys, pos, rw, g_next, final=final)


def kernel(x, norm_mix, norm_ffn, norm_final, sb_w_in, sb_w_out, ml_w_in, ml_b_i, ml_b_f, ml_g_head, ml_w_out,
           moe_w_group, moe_b_group, moe_w_expert, moe_b_expert, moe_w1, moe_w3, moe_w2):
    batch, seq, d = x.shape
    xt = x.reshape(batch * seq, d)

    h = rmsnorm_bf16(xt, norm_mix[0])
    qkv = matmul_bf16(h, sb_w_in, 3 * d)
    o = sb_attention(qkv, batch=batch, seq=seq, heads=SB_HEADS, dh=d // SB_HEADS)
    xt, h = _moe(o, sb_w_out[0], xt, norm_ffn[0], moe_w_group[0], moe_b_group[0], moe_w_expert[0],
                 moe_b_expert[0], moe_w1, moe_w3, moe_w2, 0, norm_mix[1], final=False)

    dqk = (d // 2) // ML_HEADS
    dv = d // ML_HEADS
    n_main = 2 * ML_HEADS * dqk + 2 * ML_HEADS * dv
    w_in_t = jnp.swapaxes(ml_w_in, 1, 2)
    qkvo = matmul_bf16(h, w_in_t, n_main, w_rows_are_outputs=True)
    gcol, grow = mlstm_gates(h, w_in_t[0, n_main:], ml_b_i[0], ml_b_f[0])
    o = mlstm(qkvo, gcol, grow, ml_g_head[0], batch=batch, seq=seq, heads=ML_HEADS, dqk=dqk, dv=dv)
    out = _moe(o, ml_w_out[0], xt, norm_ffn[1], moe_w_group[1], moe_b_group[1], moe_w_expert[1],
               moe_b_expert[1], moe_w1, moe_w3, moe_w2, 1, norm_final, final=True)
    return out.reshape(batch, seq, d)
```

```python
import functools

import jax
import jax.numpy as jnp
from jax import lax
from jax.experimental import pallas as pl
from jax.experimental.pallas import tpu as pltpu

F32 = jnp.float32
BF16 = jnp.bfloat16
EPS = 1e-6

V7X_VMEM_BYTES = 64 * 1024 * 1024
VMEM_LIMIT_BYTES = V7X_VMEM_BYTES - 8 * 1024 * 1024
LANES = 128
SUBLANES = 8

SB_HEADS = 16
ML_HEADS = 8
N_GROUPS = 4
EXPERTS_PER_GROUP = 8
N_EXPERTS = N_GROUPS * EXPERTS_PER_GROUP
TOP_K = 2
ML_CHUNK = 128
ML_HEAD_GROUP = 4
SB_BLOCK = 128
ROUTER_ROWS = 128
LOG2_E = 1.4426950408889634
EXP2_F32_UNDERFLOW = -150.5
DMA_PRIORITIES = 2
SLAB_SPARE_ROWS = 1
COMBINE_SLOTS = 3

NT_DIMS = (((1,), (1,)), ((), ()))


def _params(*sem):
    return pltpu.CompilerParams(dimension_semantics=sem, vmem_limit_bytes=VMEM_LIMIT_BYTES)


def _split2(x):
    hi = x.astype(BF16)
    lo = (x - hi.astype(F32)).astype(BF16)
    return hi, lo


def _split3(x):
    hi = x.astype(BF16)
    r = x - hi.astype(F32)
    mid = r.astype(BF16)
    lo = (r - mid.astype(F32)).astype(BF16)
    return hi, mid, lo


def _rms(x, g):
    return x * lax.rsqrt(jnp.mean(x * x, axis=-1, keepdims=True) + EPS) * g


def _slab_rows(d):
    return d // LANES


def _slab_pitch(d):
    return _slab_rows(d) + SLAB_SPARE_ROWS


def _slab_store(ref, x, n):
    d = x.shape[1]
    pitch = _slab_pitch(d)
    for c in range(_slab_rows(d)):
        ref[pl.ds(c, n, stride=pitch), :] = x[:, c * LANES:(c + 1) * LANES]
    for c in range(_slab_rows(d), pitch):
        ref[pl.ds(c, n, stride=pitch), :] = jnp.zeros((n, LANES), x.dtype)


def _slab_load(ref, lead, n, d):
    pitch = _slab_pitch(d)
    return jnp.concatenate([ref[lead + (pl.ds(c, n, stride=pitch), slice(None))] for c in range(_slab_rows(d))],
                           axis=1)


def _rmsnorm_body(x_ref, g_ref, h_ref):
    h_ref[...] = _rms(x_ref[...], g_ref[...]).astype(h_ref.dtype)


def rmsnorm_bf16(x, g, *, tm=512):
    t, d = x.shape
    return pl.pallas_call(
        _rmsnorm_body,
        grid=(t // tm,),
        in_specs=[pl.BlockSpec((tm, d), lambda i: (i, 0)), pl.BlockSpec((1, d), lambda i: (0, 0))],
        out_specs=pl.BlockSpec((tm, d), lambda i: (i, 0)),
        out_shape=jax.ShapeDtypeStruct((t, d), BF16),
        compiler_params=_params("parallel"),
        name="rmsnorm",
    )(x, g.reshape(1, d))


def _mm_body(h_ref, w_ref, o_ref, wb_ref, *, w_rows_are_outputs):
    @pl.when(pl.program_id(1) == 0)
    def _():
        wb_ref[...] = w_ref[...].astype(BF16)

    if w_rows_are_outputs:
        acc = lax.dot_general(h_ref[...], wb_ref[...], NT_DIMS, preferred_element_type=F32)
    else:
        acc = jnp.dot(h_ref[...], wb_ref[...], preferred_element_type=F32)
    o_ref[...] = acc.astype(o_ref.dtype)


def matmul_bf16(h, w, n, *, w_rows_are_outputs=False, tm=2048, tn=1024):
    t, k = h.shape
    if w_rows_are_outputs:
        w_spec, w_block = pl.BlockSpec((None, tn, k), lambda j, i: (0, j, 0)), (tn, k)
    else:
        w_spec, w_block = pl.BlockSpec((None, k, tn), lambda j, i: (0, 0, j)), (k, tn)
    return pl.pallas_call(
        functools.partial(_mm_body, w_rows_are_outputs=w_rows_are_outputs),
        grid=(n // tn, t // tm),
        in_specs=[pl.BlockSpec((tm, k), lambda j, i: (i, 0)), w_spec],
        out_specs=pl.BlockSpec((tm, tn), lambda j, i: (i, j)),
        out_shape=jax.ShapeDtypeStruct((t, n), BF16),
        scratch_shapes=[pltpu.VMEM(w_block, BF16)],
        compiler_params=_params("parallel", "arbitrary"),
        name="proj_in",
    )(h, w)


def _sb_attn_body(q_ref, k_ref, v_ref, cm_ref, o_ref, *, scale, nblk, hg, kc):
    blk = SB_BLOCK
    dh = SB_BLOCK
    wid = kc * blk
    rel = lax.broadcasted_iota(jnp.int32, (blk, wid), 1) - lax.broadcasted_iota(jnp.int32, (blk, wid), 0)

    def chunk(qbs, q0, c, rs, accs, masked):
        k0 = pl.multiple_of(c * wid, wid)
        if masked:
            valid = rel < (q0 - k0)
        z = jnp.concatenate(
            [lax.dot_general(qbs[h], k_ref[pl.ds(k0, wid), h * dh:(h + 1) * dh], NT_DIMS,
                             preferred_element_type=F32) for h in range(hg)], axis=1)
        sp = jnp.maximum(z, 0.0) + jnp.log(1.0 + jnp.exp2(-jnp.abs(z))) * LOG2_E
        log_beta = z - sp
        if masked:
            valid_all = jnp.concatenate([valid] * hg, axis=1)
            sp = jnp.where(valid_all, sp, 0.0)
        nb = hg * kc
        x = jnp.concatenate([sp[:, j * blk:(j + 1) * blk] for j in range(nb)], axis=0).astype(BF16)
        cs = jnp.dot(x, cm_ref[...], preferred_element_type=F32)
        new_rs = []
        suffix = [None] * nb
        for h in range(hg):
            r = rs[h]
            for j in reversed(range(kc)):
                csj = cs[(h * kc + j) * blk:(h * kc + j + 1) * blk]
                suffix[h * kc + j] = csj[:, :blk] + r
                r = r + csj[:, blk:]
            new_rs.append(r)
        a = jnp.exp2(log_beta + jnp.concatenate(suffix, axis=1))
        if masked:
            a = jnp.where(valid_all, a, 0.0)
        a = a.astype(BF16)
        new_accs = [accs[h] + jnp.dot(a[:, h * wid:(h + 1) * wid], v_ref[pl.ds(k0, wid), h * dh:(h + 1) * dh],
                                      preferred_element_type=F32) for h in range(hg)]
        return tuple(new_rs), tuple(new_accs)

    def q_step(qi, carry):
        q0 = pl.multiple_of(qi * blk, blk)
        qbs = [(q_ref[pl.ds(q0, blk), h * dh:(h + 1) * dh].astype(F32) * scale).astype(BF16) for h in range(hg)]
        zeros = tuple(jnp.zeros((blk, blk), F32) for _ in range(hg))
        top = qi // kc
        rs, accs = chunk(qbs, q0, top, zeros, zeros, True)

        def live(rs):
            return jnp.max(functools.reduce(jnp.maximum, rs)) > EXP2_F32_UNDERFLOW

        def k_cond(c):
            return (c[0] < top) & c[1]

        def k_step(c):
            rs, accs = chunk(qbs, q0, top - 1 - c[0], c[2], c[3], False)
            return c[0] + 1, live(rs), rs, accs

        accs = lax.while_loop(k_cond, k_step, (jnp.int32(0), live(rs), rs, accs))[3]
        for h in range(hg):
            o_ref[pl.ds(q0, blk), h * dh:(h + 1) * dh] = accs[h].astype(o_ref.dtype)
        return carry

    lax.fori_loop(0, nblk, q_step, 0)


def _cumsum_matrix(blk):
    j = jnp.arange(blk)[:, None]
    c = jnp.arange(2 * blk)[None, :]
    return jnp.where((c >= blk) | (j > c), -1.0, 0.0).astype(BF16)


def sb_attention(qkv, *, batch, seq, heads, dh, hg=8, kc=2):
    assert dh == SB_BLOCK
    nblk = seq // SB_BLOCK
    assert nblk % kc == 0 and heads % hg == 0
    body = functools.partial(_sb_attn_body, scale=dh ** -0.5 * LOG2_E, nblk=nblk, hg=hg, kc=kc)
    cm = _cumsum_matrix(SB_BLOCK)
    ng = heads // hg
    return pl.pallas_call(
        body,
        grid=(batch, ng),
        in_specs=[
            pl.BlockSpec((seq, hg * dh), lambda b, h: (b, h)),
            pl.BlockSpec((seq, hg * dh), lambda b, h: (b, ng + h)),
            pl.BlockSpec((seq, hg * dh), lambda b, h: (b, 2 * ng + h)),
            pl.BlockSpec((SB_BLOCK, 2 * SB_BLOCK), lambda b, h: (0, 0)),
        ],
        out_specs=pl.BlockSpec((seq, hg * dh), lambda b, h: (b, h)),
        out_shape=jax.ShapeDtypeStruct((batch * seq, heads * dh), BF16),
        compiler_params=_params("parallel", "parallel"),
        name="sb_attn",
    )(qkv, qkv, qkv, cm)


def _proj_route_body(o_ref, w_ref, x_ref, g_ref, wrh_ref, wrl_ref, br_ref, tri_ref,
                     x1_ref, ri_ref, rw_ref, cnt_ref, base_ref, xprev_ref, wb_ref, *, tm):
    i = pl.program_id(0)
    n = pl.num_programs(0) - 1

    @pl.when(i == 0)
    def _():
        base_ref[...] = jnp.zeros_like(base_ref)
        wb_ref[...] = w_ref[...].astype(BF16)

    @pl.when(i > 0)
    def _():
        _route_tile(xprev_ref[...], g_ref, wrh_ref, wrl_ref, br_ref, tri_ref, ri_ref, rw_ref, cnt_ref, base_ref, tm=tm)

    @pl.when(i < n)
    def _():
        x1 = x_ref[...] + jnp.dot(o_ref[...], wb_ref[...], preferred_element_type=F32)
        x1_ref[...] = x1
        xprev_ref[...] = x1


def _route_tile(x1, g_ref, wrh_ref, wrl_ref, br_ref, tri_ref, ri_ref, rw_ref, cnt_ref, base_ref, *, tm):
    h2 = _rms(x1, g_ref[...])

    hh, hl = _split2(h2)
    wrh = wrh_ref[...]
    lg = (lax.dot_general(wrh, hh, NT_DIMS, preferred_element_type=F32)
          + lax.dot_general(wrh, hl, NT_DIMS, preferred_element_type=F32)
          + lax.dot_general(wrl_ref[...], hh, NT_DIMS, preferred_element_type=F32)) + br_ref[...]

    r8 = lax.broadcasted_iota(jnp.int32, (SUBLANES, tm), 0)
    neg_inf = jnp.float32(-jnp.inf)
    gl = jnp.where(r8 < N_GROUPS, lg[N_EXPERTS:N_EXPERTS + SUBLANES, :], neg_inf)
    gmax = jnp.max(gl, axis=0, keepdims=True)
    gsel = jnp.min(jnp.where(gl == gmax, r8, SUBLANES), axis=0, keepdims=True)
    p_sel = 1.0 / jnp.sum(jnp.exp(gl - gmax), axis=0, keepdims=True)

    es = lg[0:EXPERTS_PER_GROUP, :]
    for g in range(1, N_GROUPS):
        es = jnp.where(gsel == g, lg[g * EXPERTS_PER_GROUP:(g + 1) * EXPERTS_PER_GROUP, :], es)
    m1 = jnp.max(es, axis=0, keepdims=True)
    i1 = jnp.min(jnp.where(es == m1, r8, SUBLANES), axis=0, keepdims=True)
    es2 = jnp.where(r8 == i1, neg_inf, es)
    m2 = jnp.max(es2, axis=0, keepdims=True)
    i2 = jnp.min(jnp.where(es2 == m2, r8, SUBLANES), axis=0, keepdims=True)
    t = jnp.exp(m2 - m1)
    w1 = p_sel / (1.0 + t)
    w2 = w1 * t
    e1 = gsel * EXPERTS_PER_GROUP + i1
    e2 = gsel * EXPERTS_PER_GROUP + i2

    r32 = lax.broadcasted_iota(jnp.int32, (N_EXPERTS, tm), 0)
    hit1 = r32 == e1
    hit2 = r32 == e2
    onehot = (jnp.where(hit1, 1.0, 0.0) + jnp.where(hit2, 1.0, 0.0)).astype(BF16)
    cs = jnp.dot(onehot, tri_ref[...], preferred_element_type=F32)
    base = base_ref[...]
    before = cs[:, :tm] + base
    rank1 = jnp.sum(jnp.where(hit1, before, 0.0), axis=0, keepdims=True)
    rank2 = jnp.sum(jnp.where(hit2, before, 0.0), axis=0, keepdims=True)
    tot = cs[:, tm:]
    new_base = base + jnp.concatenate([tot] * (tm // LANES), axis=1)
    base_ref[...] = new_base
    cnt_ref[...] = new_base[:, :LANES]

    zi = jnp.zeros((SUBLANES, tm), jnp.int32)
    ri = jnp.where(r8 == 0, e1, zi)
    ri = jnp.where(r8 == 1, e2, ri)
    ri = jnp.where(r8 == 2, rank1.astype(jnp.int32), ri)
    ri = jnp.where(r8 == 3, rank2.astype(jnp.int32), ri)
    ri_ref[...] = ri
    zf = jnp.zeros((SUBLANES, tm), F32)
    rw = jnp.where(r8 == 0, w1, zf)
    rw = jnp.where(r8 == 1, w2, rw)
    rw_ref[...] = rw


def proj_route(o, w_out, x, g, w_group, b_group, w_expert, b_expert, *, tm=256):
    t, d = x.shape
    k = o.shape[1]
    wr = jnp.zeros((ROUTER_ROWS, d), F32)
    wr = wr.at[:N_EXPERTS].set(w_expert.T).at[N_EXPERTS:N_EXPERTS + N_GROUPS].set(w_group.T)
    wrh, wrl = _split2(wr)
    br = jnp.zeros((ROUTER_ROWS,), F32).at[:N_EXPERTS].set(b_expert).at[N_EXPERTS:N_EXPERTS + N_GROUPS].set(b_group)
    br = jnp.broadcast_to(br[:, None], (ROUTER_ROWS, tm))
    tp = jnp.arange(tm)[:, None]
    tri = jnp.concatenate([(tp < jnp.arange(tm)[None, :]).astype(BF16), jnp.ones((tm, LANES), BF16)], axis=1)
    body = functools.partial(_proj_route_body, tm=tm)
    n = t // tm
    row = lambda i: (jnp.minimum(i, n - 1), 0)
    routed = lambda i: (0, jnp.maximum(i - 1, 0))
    const = lambda i: (0, 0)
    return pl.pallas_call(
        body,
        grid=(n + 1,),
        in_specs=[
            pl.BlockSpec((tm, k), row),
            pl.BlockSpec((None, k, d), lambda i: (0, 0, 0), pipeline_mode=pl.Buffered(1)),
            pl.BlockSpec((tm, d), row),
            pl.BlockSpec((1, d), const),
            pl.BlockSpec((ROUTER_ROWS, d), const),
            pl.BlockSpec((ROUTER_ROWS, d), const),
            pl.BlockSpec((ROUTER_ROWS, tm), const),
            pl.BlockSpec((tm, tm + LANES), const),
        ],
        out_specs=[
            pl.BlockSpec((tm, d), row),
            pl.BlockSpec((SUBLANES, tm), routed),
            pl.BlockSpec((SUBLANES, tm), routed),
            pl.BlockSpec((N_EXPERTS, LANES), const),
        ],
        out_shape=[
            jax.ShapeDtypeStruct((t, d), F32),
            jax.ShapeDtypeStruct((SUBLANES, t), jnp.int32),
            jax.ShapeDtypeStruct((SUBLANES, t), F32),
            jax.ShapeDtypeStruct((N_EXPERTS, LANES), F32),
        ],
        scratch_shapes=[pltpu.VMEM((N_EXPERTS, tm), F32), pltpu.VMEM((tm, d), F32), pltpu.VMEM((k, d), BF16)],
        compiler_params=_params("arbitrary"),
        name="proj_route",
    )(o, w_out, x, g.reshape(1, d), wrh, wrl, br, tri)


def _dispatch_body(pos_ref, x_ref, g_ref, xs_hbm, hs, sem, *, tm, n_pairs):
    i = pl.program_id(0)
    n = pl.num_programs(0)
    slot = lax.rem(i, 2)
    d = x_ref.shape[1]
    pitch = _slab_pitch(d)

    def row_out(p, r, s):
        return pltpu.make_async_copy(hs.at[s, pl.ds(r * pitch, pitch)], xs_hbm.at[pl.ds(p * pitch, pitch)], sem.at[s])

    def wait_tile(s):
        for k in range(TOP_K):
            for r in range(tm):
                row_out(0, r, s).wait()

    def tail(s):
        return pltpu.make_async_copy(hs.at[s], xs_hbm.at[pl.ds(n_pairs * pitch, tm * pitch)], sem.at[s])

    @pl.when(i >= 2)
    def _():
        wait_tile(slot)

    _slab_store(hs.at[slot], _rms(x_ref[...], g_ref[...]), tm)
    for k in range(TOP_K):
        for r in range(tm):
            row_out(pos_ref[0, 0, k * tm + r], r, slot).start(priority=r % DMA_PRIORITIES)

    @pl.when(i == n - 1)
    def _():
        wait_tile(slot)

        @pl.when(i >= 1)
        def _():
            wait_tile(1 - slot)

        hs[slot] = jnp.zeros(hs.shape[1:], hs.dtype)
        tail(slot).start()
        tail(slot).wait()


def dispatch_rows(x1, g, pos, *, tm=256):
    t, d = x1.shape
    nt = t // tm
    pitch = _slab_pitch(d)
    n_pairs = TOP_K * t
    pos3 = pos.reshape(TOP_K, nt, tm).transpose(1, 0, 2).reshape(nt, 1, TOP_K * tm)
    return pl.pallas_call(
        functools.partial(_dispatch_body, tm=tm, n_pairs=n_pairs),
        grid=(nt,),
        in_specs=[
            pl.BlockSpec((1, 1, TOP_K * tm), lambda i: (i, 0, 0), memory_space=pltpu.SMEM),
            pl.BlockSpec((tm, d), lambda i: (i, 0)),
            pl.BlockSpec((1, d), lambda i: (0, 0)),
        ],
        out_specs=pl.BlockSpec(memory_space=pl.ANY),
        out_shape=jax.ShapeDtypeStruct(((n_pairs + tm) * pitch, LANES), F32),
        scratch_shapes=[pltpu.VMEM((2, tm * pitch, LANES), F32), pltpu.SemaphoreType.DMA((2,))],
        compiler_params=_params("arbitrary"),
        name="dispatch",
    )(pos3, x1, g.reshape(1, d))


def _experts_body(te_ref, ts_ref, nt_ref, xs_hbm, w1_ref, w3_ref, w2_ref, ys_hbm,
                  xbuf, ybuf, w1b, w3b, w2b, isem, osem, *, tm, d, n_pairs):
    i = pl.program_id(0)
    n_tiles = nt_ref[0]
    slot = lax.rem(i, 2)
    pitch = _slab_pitch(d)

    def tile_in(j, s):
        return pltpu.make_async_copy(xs_hbm.at[pl.ds(ts_ref[j] * pitch, tm * pitch)], xbuf.at[s], isem.at[s])

    def tile_out(j, s):
        return pltpu.make_async_copy(ybuf.at[s], ys_hbm.at[pl.ds(ts_ref[j] * pitch, tm * pitch)], osem.at[s])

    @pl.when(i == 0)
    def _():
        tile_in(0, 0).start()
        ybuf[1] = jnp.zeros(ybuf.shape[1:], ybuf.dtype)
        zero_tail = pltpu.make_async_copy(ybuf.at[1], ys_hbm.at[pl.ds(n_pairs * pitch, tm * pitch)], osem.at[1])
        zero_tail.start()
        zero_tail.wait()

    @pl.when(i < n_tiles)
    def _():
        tile_in(i, slot).wait()

        @pl.when(i + 1 < n_tiles)
        def _():
            tile_in(i + 1, 1 - slot).start(priority=DMA_PRIORITIES - 1)

        @pl.when((i == 0) | (te_ref[i] != te_ref[jnp.maximum(i - 1, 0)]))
        def _():
            w1b[...] = w1_ref[0].astype(BF16)
            w3b[...] = w3_ref[0].astype(BF16)
            w2b[...] = w2_ref[0].astype(BF16)

        xb = _slab_load(xbuf, (slot,), tm, d).astype(BF16)
        a = jnp.dot(xb, w1b[...], preferred_element_type=F32)
        b = jnp.dot(xb, w3b[...], preferred_element_type=F32)
        hid = (a * (1.0 / (1.0 + jnp.exp(-a))) * b).astype(BF16)
        _slab_store(ybuf.at[slot], jnp.dot(hid, w2b[...], preferred_element_type=F32), tm)

        @pl.when(i >= 1)
        def _():
            tile_out(i - 1, 1 - slot).wait()

        tile_out(i, slot).start()

        @pl.when(i == n_tiles - 1)
        def _():
            tile_out(i, slot).wait()


def _route_tables(ri, cnt, *, tm, n_tok):
    nt_max = TOP_K * n_tok // tm + N_EXPERTS - 1
    counts = cnt[:, 0].astype(jnp.int32)
    ends = jnp.cumsum(counts)
    offs = ends - counts
    eid = jnp.arange(N_EXPERTS, dtype=jnp.int32)
    sel = ri[0:TOP_K, :, None] == eid
    pos = jnp.sum(jnp.where(sel, offs, 0), axis=-1) + ri[TOP_K:2 * TOP_K]
    tiles = (counts + tm - 1) // tm
    tile_ends = jnp.cumsum(tiles)
    n_tiles = tile_ends[-1].astype(jnp.int32)
    step = jnp.minimum(jnp.arange(nt_max, dtype=jnp.int32), n_tiles - 1)
    te = jnp.clip(jnp.sum((step[:, None] >= tile_ends[None, :]).astype(jnp.int32), axis=1), 0, N_EXPERTS - 1)
    first = jnp.sum(jnp.where(te[:, None] == eid, tile_ends - tiles, 0), axis=-1)
    ts = jnp.sum(jnp.where(te[:, None] == eid, offs, 0), axis=-1) + (step - first) * tm
    return pos, te, ts.astype(jnp.int32), n_tiles.reshape(1), nt_max


def routed_experts(xs, te, ts, n_tiles, w1, w3, w2, layer, *, tm, nt_max, n_tok):
    d, f = w1.shape[-2:]
    pitch = _slab_pitch(d)
    body = functools.partial(_experts_body, tm=tm, d=d, n_pairs=TOP_K * n_tok)
    grid_spec = pltpu.PrefetchScalarGridSpec(
        num_scalar_prefetch=3,
        grid=(nt_max,),
        in_specs=[
            pl.BlockSpec(memory_space=pl.ANY),
            pl.BlockSpec((None, 1, d, f), lambda i, te, ts, nt: (layer, te[i], 0, 0)),
            pl.BlockSpec((None, 1, d, f), lambda i, te, ts, nt: (layer, te[i], 0, 0)),
            pl.BlockSpec((None, 1, f, d), lambda i, te, ts, nt: (layer, te[i], 0, 0)),
        ],
        out_specs=pl.BlockSpec(memory_space=pl.ANY),
        scratch_shapes=[
            pltpu.VMEM((2, tm * pitch, LANES), F32),
            pltpu.VMEM((2, tm * pitch, LANES), F32),
            pltpu.VMEM((d, f), BF16),
            pltpu.VMEM((d, f), BF16),
            pltpu.VMEM((f, d), BF16),
            pltpu.SemaphoreType.DMA((2,)),
            pltpu.SemaphoreType.DMA((2,)),
        ],
    )
    return pl.pallas_call(
        body,
        grid_spec=grid_spec,
        out_shape=jax.ShapeDtypeStruct(xs.shape, F32),
        compiler_params=_params("arbitrary"),
        name="experts",
    )(te, ts, n_tiles, xs, w1, w3, w2)


def _combine_body(pos_ref, nx1_ref, nx2_ref, x_ref, w_ref, g_ref, ys_hbm, *rest, tm, final):
    if final:
        h_ref, ybuf, sem = rest
        x2_ref = None
    else:
        x2_ref, h_ref, ybuf, sem = rest
    i = pl.program_id(0)
    n = pl.num_programs(0)
    slot = lax.rem(i, COMBINE_SLOTS)
    d = x_ref.shape[1]
    rows, pitch = _slab_rows(d), _slab_pitch(d)

    def row_in(src, k, r, s):
        return pltpu.make_async_copy(ys_hbm.at[pl.ds(src * pitch, rows)], ybuf.at[s, k, pl.ds(r * pitch, rows)],
                                     sem.at[s])

    def gather_start(idx_ref, s):
        for k in range(TOP_K):
            for r in range(tm):
                row_in(idx_ref[0, 0, k * tm + r], k, r, s).start(priority=r % DMA_PRIORITIES)

    def gather_wait(s):
        for k in range(TOP_K):
            for r in range(tm):
                row_in(0, k, r, s).wait()

    @pl.when(i == 0)
    def _():
        gather_start(pos_ref, 0)

        @pl.when(n > 1)
        def _():
            gather_start(nx1_ref, 1)

    gather_wait(slot)

    @pl.when(i + 2 < n)
    def _():
        gather_start(nx2_ref, lax.rem(i + 2, COMBINE_SLOTS))

    w = w_ref[...]
    x2 = x_ref[...] + w[:, 0:1] * _slab_load(ybuf, (slot, 0), tm, d) + w[:, 1:2] * _slab_load(ybuf, (slot, 1), tm, d)
    if x2_ref is not None:
        x2_ref[...] = x2
    h_ref[...] = _rms(x2, g_ref[...]).astype(h_ref.dtype)


def moe_combine(x1, ys, pos, rw, g, *, final, tm=128):
    t, d = x1.shape
    nt = t // tm
    w = rw[0:TOP_K].T
    pos3 = pos.reshape(TOP_K, nt, tm).transpose(1, 0, 2).reshape(nt, 1, TOP_K * tm)
    row = lambda i: (i, 0)
    smem_tile = lambda ahead: pl.BlockSpec((1, 1, TOP_K * tm), lambda i: (jnp.minimum(i + ahead, nt - 1), 0, 0),
                                           memory_space=pltpu.SMEM)
    in_specs = [
        smem_tile(0), smem_tile(1), smem_tile(2),
        pl.BlockSpec((tm, d), row),
        pl.BlockSpec((tm, TOP_K), row),
        pl.BlockSpec((1, d), lambda i: (0, 0)),
        pl.BlockSpec(memory_space=pl.ANY),
    ]
    scratch = [pltpu.VMEM((COMBINE_SLOTS, TOP_K, tm * _slab_pitch(d), LANES), F32),
               pltpu.SemaphoreType.DMA((COMBINE_SLOTS,))]
    args = (pos3, pos3, pos3, x1, w, g.reshape(1, d), ys)
    body = functools.partial(_combine_body, tm=tm, final=final)
    if final:
        return pl.pallas_call(
            body, grid=(nt,), in_specs=in_specs,
            out_specs=pl.BlockSpec((tm, d), row),
            out_shape=jax.ShapeDtypeStruct((t, d), F32),
            scratch_shapes=scratch,
            compiler_params=_params("arbitrary"), name="combine_final",
        )(*args)
    return pl.pallas_call(
        body, grid=(nt,), in_specs=in_specs,
        out_specs=[pl.BlockSpec((tm, d), row), pl.BlockSpec((tm, d), row)],
        out_shape=[jax.ShapeDtypeStruct((t, d), F32), jax.ShapeDtypeStruct((t, d), BF16)],
        scratch_shapes=scratch,
        compiler_params=_params("arbitrary"), name="combine",
    )(*args)


def _log_sigmoid(x):
    return jnp.minimum(x, 0.0) - jnp.log(1.0 + jnp.exp(-jnp.abs(x)))


def _gates_body(h_ref, wch_ref, wcl_ref, bc_ref, gc_ref, gr_ref, *, hg, ng):
    h = h_ref[...]
    col = (lax.dot_general(h, wch_ref[...], NT_DIMS, preferred_element_type=F32)
           + lax.dot_general(h, wcl_ref[...], NT_DIMS, preferred_element_type=F32)) + bc_ref[...]
    lane = lax.broadcasted_iota(jnp.int32, col.shape, 1) % LANES
    gates = jnp.where(lane < hg, col, _log_sigmoid(col))
    gc_ref[...] = gates
    gt = gates.T
    gr_ref[...] = jnp.concatenate([gt[g * LANES:g * LANES + 2 * hg] for g in range(ng)], axis=0)


def mlstm_gates(h, w_if_t, b_i, b_f, *, hg, tm=512):
    t, d = h.shape
    nh = b_i.shape[0]
    ng = nh // hg

    def grouped(x_i, x_f):
        pad = jnp.zeros((ng, LANES - 2 * hg) + x_i.shape[1:], F32)
        blocks = jnp.concatenate([x_i.reshape((ng, hg) + x_i.shape[1:]), x_f.reshape((ng, hg) + x_f.shape[1:]), pad],
                                 axis=1)
        return blocks.reshape((ng * LANES,) + x_i.shape[1:])

    wch, wcl = _split2(grouped(w_if_t[:nh], w_if_t[nh:]))
    bc = grouped(b_i, b_f).reshape(1, ng * LANES)
    const = lambda i: (0, 0)
    return pl.pallas_call(
        functools.partial(_gates_body, hg=hg, ng=ng),
        grid=(t // tm,),
        in_specs=[
            pl.BlockSpec((tm, d), lambda i: (i, 0)),
            pl.BlockSpec((ng * LANES, d), const), pl.BlockSpec((ng * LANES, d), const),
            pl.BlockSpec((1, ng * LANES), const),
        ],
        out_specs=[pl.BlockSpec((tm, ng * LANES), lambda i: (i, 0)),
                   pl.BlockSpec((ng * 2 * hg, tm), lambda i: (0, i))],
        out_shape=[jax.ShapeDtypeStruct((t, ng * LANES), F32), jax.ShapeDtypeStruct((ng * 2 * hg, t), F32)],
        compiler_params=_params("parallel"),
        name="mlstm_gates",
    )(h, wch, wcl, bc)


def _mlstm_body(q_ref, k_ref, v_ref, o_ref, gc_ref, gr_ref, gh_ref, tril_ref, triu_ref, out_ref,
                c_ref, n_ref, m_ref, *, scale, nchunk, hg, dqk, dv):
    L = ML_CHUNK
    row = lax.broadcasted_iota(jnp.int32, (L, L), 0)
    col = lax.broadcasted_iota(jnp.int32, (L, L), 1)
    heads = range(hg)
    stack = lambda parts: jnp.concatenate(parts, axis=0)
    rows_of = lambda x, h: x[h * L:(h + 1) * L]
    per_row = lambda scalars: stack([jnp.broadcast_to(s, (L, 1)) for s in scalars])
    tmask = stack([col <= row] * hg)
    c_ref[...] = jnp.zeros_like(c_ref)
    n_ref[...] = jnp.zeros_like(n_ref)
    m_ref[...] = jnp.zeros_like(m_ref)

    def cumsum_exact(parts, tri, left):
        acc = None
        for p in parts:
            d = (jnp.dot(tri, p, preferred_element_type=F32) if left
                 else jnp.dot(p, tri, preferred_element_type=F32))
            acc = d if acc is None else acc + d
        return acc

    def step(c, carry):
        c0 = pl.multiple_of(c * L, L)
        qc = [(q_ref[pl.ds(c0, L), h * dqk:(h + 1) * dqk].astype(F32) * scale).astype(BF16) for h in heads]
        kc = [k_ref[pl.ds(c0, L), h * dqk:(h + 1) * dqk] for h in heads]
        vc = [v_ref[pl.ds(c0, L), h * dv:(h + 1) * dv] for h in heads]
        gc = gc_ref[pl.ds(c0, L), :]
        gr = gr_ref[:, pl.ds(c0, L)]
        i_col = stack([gc[:, h:h + 1] for h in heads])
        f_col = stack([gc[:, hg + h:hg + h + 1] for h in heads])
        m_col = m_ref[...]
        m_prev = [m_col[h * L:h * L + 1] for h in heads]

        f_parts = _split3(jnp.broadcast_to(f_col, (hg * L, L)))
        b_col = stack([cumsum_exact([rows_of(p, h) for p in f_parts], tril_ref[...], True) for h in heads])
        u_rows = []
        for h in heads:
            f_row = gr[hg + h:hg + h + 1, :]
            b_row = cumsum_exact(_split3(jnp.broadcast_to(f_row, (SUBLANES, L))), triu_ref[...], False)[0:1, :]
            u_rows.append(jnp.broadcast_to(gr[h:h + 1, :] - b_row, (L, L)))
        bt = b_col[:, 0:1]
        b_last = [b_col[h * L + L - 1:(h + 1) * L, 0:1] for h in heads]

        d_mat = jnp.where(tmask, b_col + stack(u_rows), -1e30)
        m_t = jnp.maximum(bt + m_col, jnp.max(d_mat, axis=1, keepdims=True))
        e = jnp.where(tmask, jnp.exp(d_mat - m_t), 0.0)
        w = stack([lax.dot_general(qc[h], kc[h], NT_DIMS, preferred_element_type=F32) for h in heads]) * e
        wb = w.astype(BF16)
        inter = jnp.exp(bt + m_col - m_t)
        c_prev = [c_ref[h] for h in heads]
        n_prev = [n_ref[h] for h in heads]
        num = inter * stack([jnp.dot(qc[h], c_prev[h].astype(BF16), preferred_element_type=F32) for h in heads]) \
            + stack([jnp.dot(rows_of(wb, h), vc[h], preferred_element_type=F32) for h in heads])
        qn = jnp.sum(stack([qc[h].astype(F32) * n_prev[h] for h in heads]), axis=1, keepdims=True)
        den = inter * qn + jnp.sum(w, axis=1, keepdims=True)
        hh = num / jnp.maximum(jnp.abs(den), jnp.exp(-m_t))

        gains = stack([jnp.broadcast_to(gh_ref[:, h * dv:(h + 1) * dv], (L, dv)) for h in heads])
        hs = hh * lax.rsqrt(jnp.mean(hh * hh, axis=1, keepdims=True) + EPS) * gains
        og = stack([o_ref[pl.ds(c0, L), h * dv:(h + 1) * dv] for h in heads]).astype(F32)
        gated = (hs / (1.0 + jnp.exp(-og))).astype(out_ref.dtype)
        for h in heads:
            out_ref[pl.ds(c0, L), h * dv:(h + 1) * dv] = rows_of(gated, h)

        key_log = per_row(b_last) - bt + i_col
        m_new = [jnp.maximum(b_last[h] + m_prev[h], jnp.max(rows_of(key_log, h), axis=0, keepdims=True))
                 for h in heads]
        m_new_col = per_row(m_new)
        kd = stack([kc[h].astype(F32) for h in heads]) * jnp.exp(key_log - m_new_col)
        for h in heads:
            keep = jnp.exp(b_last[h] + m_prev[h] - m_new[h])
            kd_h = rows_of(kd, h)
            c_ref[h] = keep * c_prev[h] + jnp.dot(kd_h.T.astype(BF16), vc[h], preferred_element_type=F32)
            n_ref[h] = keep * n_prev[h] + jnp.sum(kd_h, axis=0, keepdims=True)
        m_ref[...] = m_new_col
        return carry

    lax.fori_loop(0, nchunk, step, 0)


def mlstm(qkvo, gcol, grow, g_head, *, batch, seq, heads, dqk, dv, hg):
    t = batch * seq
    nh = heads
    L = ML_CHUNK
    nchunk = seq // L
    idx = jnp.arange(L)
    tril = (idx[None, :] <= idx[:, None]).astype(BF16)
    triu = (idx[:, None] <= idx[None, :]).astype(BF16)
    assert nh % hg == 0
    ng = nh // hg
    body = functools.partial(_mlstm_body, scale=dqk ** -0.5, nchunk=nchunk, hg=hg, dqk=dqk, dv=dv)
    v_blk0 = 2 * nh * dqk // (hg * dv)
    o_blk0 = v_blk0 + ng
    const = lambda b, g: (0, 0)
    return pl.pallas_call(
        body,
        grid=(batch, ng),
        in_specs=[
            pl.BlockSpec((seq, hg * dqk), lambda b, g: (b, g)),
            pl.BlockSpec((seq, hg * dqk), lambda b, g: (b, ng + g)),
            pl.BlockSpec((seq, hg * dv), lambda b, g: (b, v_blk0 + g)),
            pl.BlockSpec((seq, hg * dv), lambda b, g: (b, o_blk0 + g)),
            pl.BlockSpec((seq, LANES), lambda b, g: (b, g)),
            pl.BlockSpec((2 * hg, seq), lambda b, g: (g, b)),
            pl.BlockSpec((1, hg * dv), lambda b, g: (0, g)),
            pl.BlockSpec((L, L), const),
            pl.BlockSpec((L, L), const),
        ],
        out_specs=pl.BlockSpec((seq, hg * dv), lambda b, g: (b, g)),
        out_shape=jax.ShapeDtypeStruct((t, nh * dv), BF16),
        scratch_shapes=[pltpu.VMEM((hg, dqk, dv), F32), pltpu.VMEM((hg, 1, dqk), F32), pltpu.VMEM((hg * L, 1), F32)],
        compiler_params=_params("parallel", "parallel"),
        name="mlstm",
    )(qkvo, qkvo, qkvo, qkvo, gcol, grow, g_head.reshape(1, nh * dv), tril, triu)


def _moe(o, w_out, x, norm_ffn, w_group, b_group, w_expert, b_expert, w1, w3, w2, layer, g_next, *, final,
         tm=256):
    n_tok = x.shape[0]
    x1, ri, rw, cnt = proj_route(o, w_out, x, norm_ffn, w_group, b_group, w_expert, b_expert)
    pos, te, ts, n_tiles, nt_max = _route_tables(ri, cnt, tm=tm, n_tok=n_tok)
    xs = dispatch_rows(x1, norm_ffn, pos)
    ys = routed_experts(xs, te, ts, n_tiles, w1, w3, w2, layer, tm=tm, nt_max=nt_max, n_tok=n_tok)
    return moe_combine(x1, ys, pos, rw, g_next, final=final)


def kernel(x, norm_mix, norm_ffn, norm_final, sb_w_in, sb_w_out, ml_w_in, ml_b_i, ml_b_f, ml_g_head, ml_w_out,
           moe_w_group, moe_b_group, moe_w_expert, moe_b_expert, moe_w1, moe_w3, moe_w2):
    batch, seq, d = x.shape
    xt = x.reshape(batch * seq, d)

    h = rmsnorm_bf16(xt, norm_mix[0])
    qkv = matmul_bf16(h, sb_w_in, 3 * d)
    o = sb_attention(qkv, batch=batch, seq=seq, heads=SB_HEADS, dh=d // SB_HEADS)
    xt, h = _moe(o, sb_w_out, xt, norm_ffn[0], moe_w_group[0], moe_b_group[0], moe_w_expert[0],
                 moe_b_expert[0], moe_w1, moe_w3, moe_w2, 0, norm_mix[1], final=False)

    dqk = (d // 2) // ML_HEADS
    dv = d // ML_HEADS
    n_main = 2 * ML_HEADS * dqk + 2 * ML_HEADS * dv
    w_in_t = jnp.swapaxes(ml_w_in, 1, 2)
    qkvo = matmul_bf16(h, w_in_t, n_main, w_rows_are_outputs=True)
    gcol, grow = mlstm_gates(h, w_in_t[0, n_main:], ml_b_i[0], ml_b_f[0], hg=ML_HEAD_GROUP)
    o = mlstm(qkvo, gcol, grow, ml_g_head[0], batch=batch, seq=seq, heads=ML_HEADS, dqk=dqk, dv=dv,
              hg=ML_HEAD_GROUP)
    out = _moe(o, ml_w_out, xt, norm_ffn[1], moe_w_group[1], moe_b_group[1], moe_w_expert[1],
               moe_b_expert[1], moe_w1, moe_w3, moe_w2, 1, norm_final, final=True)
    return out.reshape(batch, seq, d)
```

```python
import functools

import jax
import jax.numpy as jnp
from jax import lax
from jax.experimental import pallas as pl
from jax.experimental.pallas import tpu as pltpu

F32 = jnp.float32
BF16 = jnp.bfloat16
EPS = 1e-6

V7X_VMEM_BYTES = 64 * 1024 * 1024
VMEM_LIMIT_BYTES = V7X_VMEM_BYTES - 8 * 1024 * 1024
LANES = 128
SUBLANES = 8

SB_HEADS = 16
ML_HEADS = 8
N_GROUPS = 4
EXPERTS_PER_GROUP = 8
N_EXPERTS = N_GROUPS * EXPERTS_PER_GROUP
TOP_K = 2
ML_CHUNK = 128
ML_HEAD_GROUP = 8
ML_SEGMENTS = 2
SB_BLOCK = 128
ROUTER_ROWS = 128
LOG2_E = 1.4426950408889634
EXP2_F32_UNDERFLOW = -150.5
DMA_PRIORITIES = 2
SLAB_SPARE_ROWS = 1
COMBINE_SLOTS = 3

NT_DIMS = (((1,), (1,)), ((), ()))


def _params(*sem):
    return pltpu.CompilerParams(dimension_semantics=sem, vmem_limit_bytes=VMEM_LIMIT_BYTES)


def _split2(x):
    hi = x.astype(BF16)
    lo = (x - hi.astype(F32)).astype(BF16)
    return hi, lo


def _split3(x):
    hi = x.astype(BF16)
    r = x - hi.astype(F32)
    mid = r.astype(BF16)
    lo = (r - mid.astype(F32)).astype(BF16)
    return hi, mid, lo


def _rms(x, g):
    return x * lax.rsqrt(jnp.mean(x * x, axis=-1, keepdims=True) + EPS) * g


def _slab_rows(d):
    return d // LANES


def _slab_pitch(d):
    return _slab_rows(d) + SLAB_SPARE_ROWS


def _slab_store(ref, x, n):
    d = x.shape[1]
    pitch = _slab_pitch(d)
    for c in range(_slab_rows(d)):
        ref[pl.ds(c, n, stride=pitch), :] = x[:, c * LANES:(c + 1) * LANES]
    for c in range(_slab_rows(d), pitch):
        ref[pl.ds(c, n, stride=pitch), :] = jnp.zeros((n, LANES), x.dtype)


def _slab_load(ref, lead, n, d):
    pitch = _slab_pitch(d)
    return jnp.concatenate([ref[lead + (pl.ds(c, n, stride=pitch), slice(None))] for c in range(_slab_rows(d))],
                           axis=1)


def _rmsnorm_body(x_ref, g_ref, h_ref):
    h_ref[...] = _rms(x_ref[...], g_ref[...]).astype(h_ref.dtype)


def rmsnorm_bf16(x, g, *, tm=512):
    t, d = x.shape
    return pl.pallas_call(
        _rmsnorm_body,
        grid=(t // tm,),
        in_specs=[pl.BlockSpec((tm, d), lambda i: (i, 0)), pl.BlockSpec((1, d), lambda i: (0, 0))],
        out_specs=pl.BlockSpec((tm, d), lambda i: (i, 0)),
        out_shape=jax.ShapeDtypeStruct((t, d), BF16),
        compiler_params=_params("parallel"),
        name="rmsnorm",
    )(x, g.reshape(1, d))


def _mm_body(h_ref, w_ref, o_ref, wb_ref, *, w_rows_are_outputs):
    @pl.when(pl.program_id(1) == 0)
    def _():
        wb_ref[...] = w_ref[...].astype(BF16)

    if w_rows_are_outputs:
        acc = lax.dot_general(h_ref[...], wb_ref[...], NT_DIMS, preferred_element_type=F32)
    else:
        acc = jnp.dot(h_ref[...], wb_ref[...], preferred_element_type=F32)
    o_ref[...] = acc.astype(o_ref.dtype)


def matmul_bf16(h, w, n, *, w_rows_are_outputs=False, tm=2048, tn=1024):
    t, k = h.shape
    if w_rows_are_outputs:
        w_spec, w_block = pl.BlockSpec((None, tn, k), lambda j, i: (0, j, 0)), (tn, k)
    else:
        w_spec, w_block = pl.BlockSpec((None, k, tn), lambda j, i: (0, 0, j)), (k, tn)
    return pl.pallas_call(
        functools.partial(_mm_body, w_rows_are_outputs=w_rows_are_outputs),
        grid=(n // tn, t // tm),
        in_specs=[pl.BlockSpec((tm, k), lambda j, i: (i, 0)), w_spec],
        out_specs=pl.BlockSpec((tm, tn), lambda j, i: (i, j)),
        out_shape=jax.ShapeDtypeStruct((t, n), BF16),
        scratch_shapes=[pltpu.VMEM(w_block, BF16)],
        compiler_params=_params("parallel", "arbitrary"),
        name="proj_in",
    )(h, w)


def _sb_attn_body(q_ref, k_ref, v_ref, cm_ref, o_ref, *, scale, nblk, hg, kc):
    blk = SB_BLOCK
    dh = SB_BLOCK
    wid = kc * blk
    rel = lax.broadcasted_iota(jnp.int32, (blk, wid), 1) - lax.broadcasted_iota(jnp.int32, (blk, wid), 0)

    def chunk(qbs, q0, c, rs, accs, masked):
        k0 = pl.multiple_of(c * wid, wid)
        if masked:
            valid = rel < (q0 - k0)
        z = jnp.concatenate(
            [lax.dot_general(qbs[h], k_ref[pl.ds(k0, wid), h * dh:(h + 1) * dh], NT_DIMS,
                             preferred_element_type=F32) for h in range(hg)], axis=1)
        sp = jnp.maximum(z, 0.0) + jnp.log(1.0 + jnp.exp2(-jnp.abs(z))) * LOG2_E
        log_beta = z - sp
        if masked:
            valid_all = jnp.concatenate([valid] * hg, axis=1)
            sp = jnp.where(valid_all, sp, 0.0)
        nb = hg * kc
        x = jnp.concatenate([sp[:, j * blk:(j + 1) * blk] for j in range(nb)], axis=0).astype(BF16)
        cs = jnp.dot(x, cm_ref[...], preferred_element_type=F32)
        new_rs = []
        suffix = [None] * nb
        for h in range(hg):
            r = rs[h]
            for j in reversed(range(kc)):
                csj = cs[(h * kc + j) * blk:(h * kc + j + 1) * blk]
                suffix[h * kc + j] = csj[:, :blk] + r
                r = r + csj[:, blk:]
            new_rs.append(r)
        a = jnp.exp2(log_beta + jnp.concatenate(suffix, axis=1))
        if masked:
            a = jnp.where(valid_all, a, 0.0)
        a = a.astype(BF16)
        new_accs = [accs[h] + jnp.dot(a[:, h * wid:(h + 1) * wid], v_ref[pl.ds(k0, wid), h * dh:(h + 1) * dh],
                                      preferred_element_type=F32) for h in range(hg)]
        return tuple(new_rs), tuple(new_accs)

    def q_step(qi, carry):
        q0 = pl.multiple_of(qi * blk, blk)
        qbs = [(q_ref[pl.ds(q0, blk), h * dh:(h + 1) * dh].astype(F32) * scale).astype(BF16) for h in range(hg)]
        zeros = tuple(jnp.zeros((blk, blk), F32) for _ in range(hg))
        top = qi // kc
        rs, accs = chunk(qbs, q0, top, zeros, zeros, True)

        def live(rs):
            return jnp.max(functools.reduce(jnp.maximum, rs)) > EXP2_F32_UNDERFLOW

        def k_cond(c):
            return (c[0] < top) & c[1]

        def k_step(c):
            rs, accs = chunk(qbs, q0, top - 1 - c[0], c[2], c[3], False)
            return c[0] + 1, live(rs), rs, accs

        accs = lax.while_loop(k_cond, k_step, (jnp.int32(0), live(rs), rs, accs))[3]
        for h in range(hg):
            o_ref[pl.ds(q0, blk), h * dh:(h + 1) * dh] = accs[h].astype(o_ref.dtype)
        return carry

    lax.fori_loop(0, nblk, q_step, 0)


def _cumsum_matrix(blk):
    j = jnp.arange(blk)[:, None]
    c = jnp.arange(2 * blk)[None, :]
    return jnp.where((c >= blk) | (j > c), -1.0, 0.0).astype(BF16)


def sb_attention(qkv, *, batch, seq, heads, dh, hg=8, kc=2):
    assert dh == SB_BLOCK
    nblk = seq // SB_BLOCK
    assert nblk % kc == 0 and heads % hg == 0
    body = functools.partial(_sb_attn_body, scale=dh ** -0.5 * LOG2_E, nblk=nblk, hg=hg, kc=kc)
    cm = _cumsum_matrix(SB_BLOCK)
    ng = heads // hg
    return pl.pallas_call(
        body,
        grid=(batch, ng),
        in_specs=[
            pl.BlockSpec((seq, hg * dh), lambda b, h: (b, h)),
            pl.BlockSpec((seq, hg * dh), lambda b, h: (b, ng + h)),
            pl.BlockSpec((seq, hg * dh), lambda b, h: (b, 2 * ng + h)),
            pl.BlockSpec((SB_BLOCK, 2 * SB_BLOCK), lambda b, h: (0, 0)),
        ],
        out_specs=pl.BlockSpec((seq, hg * dh), lambda b, h: (b, h)),
        out_shape=jax.ShapeDtypeStruct((batch * seq, heads * dh), BF16),
        compiler_params=_params("parallel", "parallel"),
        name="sb_attn",
    )(qkv, qkv, qkv, cm)


def _proj_route_body(o_ref, w_ref, x_ref, g_ref, wrh_ref, wrl_ref, br_ref, tri_ref,
                     x1_ref, ri_ref, rw_ref, cnt_ref, base_ref, xprev_ref, wb_ref, *, tm):
    i = pl.program_id(0)
    n = pl.num_programs(0) - 1

    @pl.when(i == 0)
    def _():
        base_ref[...] = jnp.zeros_like(base_ref)
        wb_ref[...] = w_ref[...].astype(BF16)

    @pl.when(i > 0)
    def _():
        _route_tile(xprev_ref[...], g_ref, wrh_ref, wrl_ref, br_ref, tri_ref, ri_ref, rw_ref, cnt_ref, base_ref, tm=tm)

    @pl.when(i < n)
    def _():
        x1 = x_ref[...] + jnp.dot(o_ref[...], wb_ref[...], preferred_element_type=F32)
        x1_ref[...] = x1
        xprev_ref[...] = x1


def _route_tile(x1, g_ref, wrh_ref, wrl_ref, br_ref, tri_ref, ri_ref, rw_ref, cnt_ref, base_ref, *, tm):
    h2 = _rms(x1, g_ref[...])

    hh, hl = _split2(h2)
    wrh = wrh_ref[...]
    lg = (lax.dot_general(wrh, hh, NT_DIMS, preferred_element_type=F32)
          + lax.dot_general(wrh, hl, NT_DIMS, preferred_element_type=F32)
          + lax.dot_general(wrl_ref[...], hh, NT_DIMS, preferred_element_type=F32)) + br_ref[...]

    r8 = lax.broadcasted_iota(jnp.int32, (SUBLANES, tm), 0)
    neg_inf = jnp.float32(-jnp.inf)
    gl = jnp.where(r8 < N_GROUPS, lg[N_EXPERTS:N_EXPERTS + SUBLANES, :], neg_inf)
    gmax = jnp.max(gl, axis=0, keepdims=True)
    gsel = jnp.min(jnp.where(gl == gmax, r8, SUBLANES), axis=0, keepdims=True)
    p_sel = 1.0 / jnp.sum(jnp.exp(gl - gmax), axis=0, keepdims=True)

    es = lg[0:EXPERTS_PER_GROUP, :]
    for g in range(1, N_GROUPS):
        es = jnp.where(gsel == g, lg[g * EXPERTS_PER_GROUP:(g + 1) * EXPERTS_PER_GROUP, :], es)
    m1 = jnp.max(es, axis=0, keepdims=True)
    i1 = jnp.min(jnp.where(es == m1, r8, SUBLANES), axis=0, keepdims=True)
    es2 = jnp.where(r8 == i1, neg_inf, es)
    m2 = jnp.max(es2, axis=0, keepdims=True)
    i2 = jnp.min(jnp.where(es2 == m2, r8, SUBLANES), axis=0, keepdims=True)
    t = jnp.exp(m2 - m1)
    w1 = p_sel / (1.0 + t)
    w2 = w1 * t
    e1 = gsel * EXPERTS_PER_GROUP + i1
    e2 = gsel * EXPERTS_PER_GROUP + i2

    r32 = lax.broadcasted_iota(jnp.int32, (N_EXPERTS, tm), 0)
    hit1 = r32 == e1
    hit2 = r32 == e2
    onehot = (jnp.where(hit1, 1.0, 0.0) + jnp.where(hit2, 1.0, 0.0)).astype(BF16)
    cs = jnp.dot(onehot, tri_ref[...], preferred_element_type=F32)
    base = base_ref[...]
    before = cs[:, :tm] + base
    rank1 = jnp.sum(jnp.where(hit1, before, 0.0), axis=0, keepdims=True)
    rank2 = jnp.sum(jnp.where(hit2, before, 0.0), axis=0, keepdims=True)
    tot = cs[:, tm:]
    new_base = base + jnp.concatenate([tot] * (tm // LANES), axis=1)
    base_ref[...] = new_base
    cnt_ref[...] = new_base[:, :LANES]

    zi = jnp.zeros((SUBLANES, tm), jnp.int32)
    ri = jnp.where(r8 == 0, e1, zi)
    ri = jnp.where(r8 == 1, e2, ri)
    ri = jnp.where(r8 == 2, rank1.astype(jnp.int32), ri)
    ri = jnp.where(r8 == 3, rank2.astype(jnp.int32), ri)
    ri_ref[...] = ri
    zf = jnp.zeros((SUBLANES, tm), F32)
    rw = jnp.where(r8 == 0, w1, zf)
    rw = jnp.where(r8 == 1, w2, rw)
    rw_ref[...] = rw


def proj_route(o, w_out, x, g, w_group, b_group, w_expert, b_expert, *, tm=256):
    t, d = x.shape
    k = o.shape[1]
    wr = jnp.zeros((ROUTER_ROWS, d), F32)
    wr = wr.at[:N_EXPERTS].set(w_expert.T).at[N_EXPERTS:N_EXPERTS + N_GROUPS].set(w_group.T)
    wrh, wrl = _split2(wr)
    br = jnp.zeros((ROUTER_ROWS,), F32).at[:N_EXPERTS].set(b_expert).at[N_EXPERTS:N_EXPERTS + N_GROUPS].set(b_group)
    br = jnp.broadcast_to(br[:, None], (ROUTER_ROWS, tm))
    tp = jnp.arange(tm)[:, None]
    tri = jnp.concatenate([(tp < jnp.arange(tm)[None, :]).astype(BF16), jnp.ones((tm, LANES), BF16)], axis=1)
    body = functools.partial(_proj_route_body, tm=tm)
    n = t // tm
    row = lambda i: (jnp.minimum(i, n - 1), 0)
    routed = lambda i: (0, jnp.maximum(i - 1, 0))
    const = lambda i: (0, 0)
    return pl.pallas_call(
        body,
        grid=(n + 1,),
        in_specs=[
            pl.BlockSpec((tm, k), row),
            pl.BlockSpec((None, k, d), lambda i: (0, 0, 0), pipeline_mode=pl.Buffered(1)),
            pl.BlockSpec((tm, d), row),
            pl.BlockSpec((1, d), const),
            pl.BlockSpec((ROUTER_ROWS, d), const),
            pl.BlockSpec((ROUTER_ROWS, d), const),
            pl.BlockSpec((ROUTER_ROWS, tm), const),
            pl.BlockSpec((tm, tm + LANES), const),
        ],
        out_specs=[
            pl.BlockSpec((tm, d), row),
            pl.BlockSpec((SUBLANES, tm), routed),
            pl.BlockSpec((SUBLANES, tm), routed),
            pl.BlockSpec((N_EXPERTS, LANES), const),
        ],
        out_shape=[
            jax.ShapeDtypeStruct((t, d), F32),
            jax.ShapeDtypeStruct((SUBLANES, t), jnp.int32),
            jax.ShapeDtypeStruct((SUBLANES, t), F32),
            jax.ShapeDtypeStruct((N_EXPERTS, LANES), F32),
        ],
        scratch_shapes=[pltpu.VMEM((N_EXPERTS, tm), F32), pltpu.VMEM((tm, d), F32), pltpu.VMEM((k, d), BF16)],
        compiler_params=_params("arbitrary"),
        name="proj_route",
    )(o, w_out, x, g.reshape(1, d), wrh, wrl, br, tri)


def _dispatch_body(pos_ref, x_ref, g_ref, xs_hbm, hs, sem, *, tm, n_pairs):
    i = pl.program_id(0)
    n = pl.num_programs(0)
    slot = lax.rem(i, 2)
    d = x_ref.shape[1]
    pitch = _slab_pitch(d)

    def row_out(p, r, s):
        return pltpu.make_async_copy(hs.at[s, pl.ds(r * pitch, pitch)], xs_hbm.at[pl.ds(p * pitch, pitch)], sem.at[s])

    def wait_tile(s):
        for k in range(TOP_K):
            for r in range(tm):
                row_out(0, r, s).wait()

    def tail(s):
        return pltpu.make_async_copy(hs.at[s], xs_hbm.at[pl.ds(n_pairs * pitch, tm * pitch)], sem.at[s])

    @pl.when(i >= 2)
    def _():
        wait_tile(slot)

    _slab_store(hs.at[slot], _rms(x_ref[...], g_ref[...]), tm)
    for k in range(TOP_K):
        for r in range(tm):
            row_out(pos_ref[0, 0, k * tm + r], r, slot).start(priority=r % DMA_PRIORITIES)

    @pl.when(i == n - 1)
    def _():
        wait_tile(slot)

        @pl.when(i >= 1)
        def _():
            wait_tile(1 - slot)

        hs[slot] = jnp.zeros(hs.shape[1:], hs.dtype)
        tail(slot).start()
        tail(slot).wait()


def dispatch_rows(x1, g, pos, *, tm=256):
    t, d = x1.shape
    nt = t // tm
    pitch = _slab_pitch(d)
    n_pairs = TOP_K * t
    pos3 = pos.reshape(TOP_K, nt, tm).transpose(1, 0, 2).reshape(nt, 1, TOP_K * tm)
    return pl.pallas_call(
        functools.partial(_dispatch_body, tm=tm, n_pairs=n_pairs),
        grid=(nt,),
        in_specs=[
            pl.BlockSpec((1, 1, TOP_K * tm), lambda i: (i, 0, 0), memory_space=pltpu.SMEM),
            pl.BlockSpec((tm, d), lambda i: (i, 0)),
            pl.BlockSpec((1, d), lambda i: (0, 0)),
        ],
        out_specs=pl.BlockSpec(memory_space=pl.ANY),
        out_shape=jax.ShapeDtypeStruct(((n_pairs + tm) * pitch, LANES), F32),
        scratch_shapes=[pltpu.VMEM((2, tm * pitch, LANES), F32), pltpu.SemaphoreType.DMA((2,))],
        compiler_params=_params("arbitrary"),
        name="dispatch",
    )(pos3, x1, g.reshape(1, d))


def _experts_body(te_ref, ts_ref, nt_ref, run_ref, nxt_ref, xs_hbm, w1_hbm, w3_hbm, w2_hbm, ys_hbm,
                  xbuf, ybuf, wf1, wf3, wf2, w1b, w3b, w2b, isem, osem, wsem, *, tm, d, n_pairs, layer):
    i = pl.program_id(0)
    n_tiles = nt_ref[0]
    slot = lax.rem(i, 2)
    pitch = _slab_pitch(d)

    def tile_in(j, s):
        return pltpu.make_async_copy(xs_hbm.at[pl.ds(ts_ref[j] * pitch, tm * pitch)], xbuf.at[s], isem.at[s])

    def tile_out(j, s):
        return pltpu.make_async_copy(ybuf.at[s], ys_hbm.at[pl.ds(ts_ref[j] * pitch, tm * pitch)], osem.at[s])

    def weights_in(e, s):
        return [pltpu.make_async_copy(w1_hbm.at[layer, e], wf1.at[s], wsem.at[s]),
                pltpu.make_async_copy(w3_hbm.at[layer, e], wf3.at[s], wsem.at[s]),
                pltpu.make_async_copy(w2_hbm.at[layer, e], wf2.at[s], wsem.at[s])]

    @pl.when(i == 0)
    def _():
        for c in weights_in(te_ref[0], 0):
            c.start()
        tile_in(0, 0).start()
        ybuf[1] = jnp.zeros(ybuf.shape[1:], ybuf.dtype)
        zero_tail = pltpu.make_async_copy(ybuf.at[1], ys_hbm.at[pl.ds(n_pairs * pitch, tm * pitch)], osem.at[1])
        zero_tail.start()
        zero_tail.wait()

    @pl.when(i < n_tiles)
    def _():
        tile_in(i, slot).wait()

        @pl.when(i + 1 < n_tiles)
        def _():
            tile_in(i + 1, 1 - slot).start(priority=DMA_PRIORITIES - 1)

        @pl.when((i == 0) | (te_ref[i] != te_ref[jnp.maximum(i - 1, 0)]))
        def _():
            ws = lax.rem(run_ref[i], 2)
            for c in weights_in(te_ref[i], ws):
                c.wait()
            w1b[...] = wf1[ws].astype(BF16)
            w3b[...] = wf3[ws].astype(BF16)
            w2b[...] = wf2[ws].astype(BF16)

            @pl.when(nxt_ref[i] >= 0)
            def _():
                for c in weights_in(nxt_ref[i], 1 - ws):
                    c.start()

        xb = _slab_load(xbuf, (slot,), tm, d).astype(BF16)
        a = jnp.dot(xb, w1b[...], preferred_element_type=F32)
        b = jnp.dot(xb, w3b[...], preferred_element_type=F32)
        hid = (a * (1.0 / (1.0 + jnp.exp(-a))) * b).astype(BF16)
        _slab_store(ybuf.at[slot], jnp.dot(hid, w2b[...], preferred_element_type=F32), tm)

        @pl.when(i >= 1)
        def _():
            tile_out(i - 1, 1 - slot).wait()

        tile_out(i, slot).start()

        @pl.when(i == n_tiles - 1)
        def _():
            tile_out(i, slot).wait()


def _route_tables(ri, cnt, *, tm, n_tok):
    nt_max = TOP_K * n_tok // tm + N_EXPERTS - 1
    counts = cnt[:, 0].astype(jnp.int32)
    ends = jnp.cumsum(counts)
    offs = ends - counts
    eid = jnp.arange(N_EXPERTS, dtype=jnp.int32)
    sel = ri[0:TOP_K, :, None] == eid
    pos = jnp.sum(jnp.where(sel, offs, 0), axis=-1) + ri[TOP_K:2 * TOP_K]
    tiles = (counts + tm - 1) // tm
    tile_ends = jnp.cumsum(tiles)
    n_tiles = tile_ends[-1].astype(jnp.int32)
    step = jnp.minimum(jnp.arange(nt_max, dtype=jnp.int32), n_tiles - 1)
    te = jnp.clip(jnp.sum((step[:, None] >= tile_ends[None, :]).astype(jnp.int32), axis=1), 0, N_EXPERTS - 1)
    first = jnp.sum(jnp.where(te[:, None] == eid, tile_ends - tiles, 0), axis=-1)
    ts = jnp.sum(jnp.where(te[:, None] == eid, offs, 0), axis=-1) + (step - first) * tm
    run = jnp.sum(((tiles > 0) & (eid < te[:, None])).astype(jnp.int32), axis=-1)
    after = jnp.sum(jnp.where(te[:, None] == eid, tile_ends, 0), axis=-1)
    te_after = jnp.clip(jnp.sum((after[:, None] >= tile_ends[None, :]).astype(jnp.int32), axis=1), 0, N_EXPERTS - 1)
    nxt = jnp.where(after < n_tiles, te_after, -1)
    tables = (te, ts.astype(jnp.int32), n_tiles.reshape(1), run.astype(jnp.int32), nxt.astype(jnp.int32))
    return pos, tables, nt_max


def routed_experts(xs, tables, w1, w3, w2, layer, *, tm, nt_max, n_tok):
    d, f = w1.shape[-2:]
    pitch = _slab_pitch(d)
    body = functools.partial(_experts_body, tm=tm, d=d, n_pairs=TOP_K * n_tok, layer=layer)
    any_space = pl.BlockSpec(memory_space=pl.ANY)
    grid_spec = pltpu.PrefetchScalarGridSpec(
        num_scalar_prefetch=len(tables),
        grid=(nt_max,),
        in_specs=[any_space, any_space, any_space, any_space],
        out_specs=any_space,
        scratch_shapes=[
            pltpu.VMEM((2, tm * pitch, LANES), F32),
            pltpu.VMEM((2, tm * pitch, LANES), F32),
            pltpu.VMEM((2, d, f), F32),
            pltpu.VMEM((2, d, f), F32),
            pltpu.VMEM((2, f, d), F32),
            pltpu.VMEM((d, f), BF16),
            pltpu.VMEM((d, f), BF16),
            pltpu.VMEM((f, d), BF16),
            pltpu.SemaphoreType.DMA((2,)),
            pltpu.SemaphoreType.DMA((2,)),
            pltpu.SemaphoreType.DMA((2,)),
        ],
    )
    return pl.pallas_call(
        body,
        grid_spec=grid_spec,
        out_shape=jax.ShapeDtypeStruct(xs.shape, F32),
        compiler_params=_params("arbitrary"),
        name="experts",
    )(*tables, xs, w1, w3, w2)


def _combine_body(pos_ref, nx1_ref, nx2_ref, x_ref, w_ref, g_ref, ys_hbm, *rest, tm, final):
    if final:
        h_ref, ybuf, sem = rest
        x2_ref = None
    else:
        x2_ref, h_ref, ybuf, sem = rest
    i = pl.program_id(0)
    n = pl.num_programs(0)
    slot = lax.rem(i, COMBINE_SLOTS)
    d = x_ref.shape[1]
    rows, pitch = _slab_rows(d), _slab_pitch(d)

    def row_in(src, k, r, s):
        return pltpu.make_async_copy(ys_hbm.at[pl.ds(src * pitch, rows)], ybuf.at[s, k, pl.ds(r * pitch, rows)],
                                     sem.at[s])

    def gather_start(idx_ref, s):
        for k in range(TOP_K):
            for r in range(tm):
                row_in(idx_ref[0, 0, k * tm + r], k, r, s).start(priority=r % DMA_PRIORITIES)

    def gather_wait(s):
        for k in range(TOP_K):
            for r in range(tm):
                row_in(0, k, r, s).wait()

    @pl.when(i == 0)
    def _():
        gather_start(pos_ref, 0)

        @pl.when(n > 1)
        def _():
            gather_start(nx1_ref, 1)

    gather_wait(slot)

    @pl.when(i + 2 < n)
    def _():
        gather_start(nx2_ref, lax.rem(i + 2, COMBINE_SLOTS))

    w = w_ref[...]
    x2 = x_ref[...] + w[:, 0:1] * _slab_load(ybuf, (slot, 0), tm, d) + w[:, 1:2] * _slab_load(ybuf, (slot, 1), tm, d)
    if x2_ref is not None:
        x2_ref[...] = x2
    h_ref[...] = _rms(x2, g_ref[...]).astype(h_ref.dtype)


def moe_combine(x1, ys, pos, rw, g, *, final, tm=128):
    t, d = x1.shape
    nt = t // tm
    w = rw[0:TOP_K].T
    pos3 = pos.reshape(TOP_K, nt, tm).transpose(1, 0, 2).reshape(nt, 1, TOP_K * tm)
    row = lambda i: (i, 0)
    smem_tile = lambda ahead: pl.BlockSpec((1, 1, TOP_K * tm), lambda i: (jnp.minimum(i + ahead, nt - 1), 0, 0),
                                           memory_space=pltpu.SMEM)
    in_specs = [
        smem_tile(0), smem_tile(1), smem_tile(2),
        pl.BlockSpec((tm, d), row),
        pl.BlockSpec((tm, TOP_K), row),
        pl.BlockSpec((1, d), lambda i: (0, 0)),
        pl.BlockSpec(memory_space=pl.ANY),
    ]
    scratch = [pltpu.VMEM((COMBINE_SLOTS, TOP_K, tm * _slab_pitch(d), LANES), F32),
               pltpu.SemaphoreType.DMA((COMBINE_SLOTS,))]
    args = (pos3, pos3, pos3, x1, w, g.reshape(1, d), ys)
    body = functools.partial(_combine_body, tm=tm, final=final)
    if final:
        return pl.pallas_call(
            body, grid=(nt,), in_specs=in_specs,
            out_specs=pl.BlockSpec((tm, d), row),
            out_shape=jax.ShapeDtypeStruct((t, d), F32),
            scratch_shapes=scratch,
            compiler_params=_params("arbitrary"), name="combine_final",
        )(*args)
    return pl.pallas_call(
        body, grid=(nt,), in_specs=in_specs,
        out_specs=[pl.BlockSpec((tm, d), row), pl.BlockSpec((tm, d), row)],
        out_shape=[jax.ShapeDtypeStruct((t, d), F32), jax.ShapeDtypeStruct((t, d), BF16)],
        scratch_shapes=scratch,
        compiler_params=_params("arbitrary"), name="combine",
    )(*args)


def _log_sigmoid(x):
    return jnp.minimum(x, 0.0) - jnp.log(1.0 + jnp.exp(-jnp.abs(x)))


def _gates_body(h_ref, wch_ref, wcl_ref, bc_ref, gc_ref, gr_ref, *, hg, ng):
    h = h_ref[...]
    col = (lax.dot_general(h, wch_ref[...], NT_DIMS, preferred_element_type=F32)
           + lax.dot_general(h, wcl_ref[...], NT_DIMS, preferred_element_type=F32)) + bc_ref[...]
    lane = lax.broadcasted_iota(jnp.int32, col.shape, 1) % LANES
    gates = jnp.where(lane < hg, col, _log_sigmoid(col))
    gc_ref[...] = gates
    gt = gates.T
    gr_ref[...] = jnp.concatenate([gt[g * LANES:g * LANES + 2 * hg] for g in range(ng)], axis=0)


def mlstm_gates(h, w_if_t, b_i, b_f, *, hg, tm=512):
    t, d = h.shape
    nh = b_i.shape[0]
    ng = nh // hg

    def grouped(x_i, x_f):
        pad = jnp.zeros((ng, LANES - 2 * hg) + x_i.shape[1:], F32)
        blocks = jnp.concatenate([x_i.reshape((ng, hg) + x_i.shape[1:]), x_f.reshape((ng, hg) + x_f.shape[1:]), pad],
                                 axis=1)
        return blocks.reshape((ng * LANES,) + x_i.shape[1:])

    wch, wcl = _split2(grouped(w_if_t[:nh], w_if_t[nh:]))
    bc = grouped(b_i, b_f).reshape(1, ng * LANES)
    const = lambda i: (0, 0)
    return pl.pallas_call(
        functools.partial(_gates_body, hg=hg, ng=ng),
        grid=(t // tm,),
        in_specs=[
            pl.BlockSpec((tm, d), lambda i: (i, 0)),
            pl.BlockSpec((ng * LANES, d), const), pl.BlockSpec((ng * LANES, d), const),
            pl.BlockSpec((1, ng * LANES), const),
        ],
        out_specs=[pl.BlockSpec((tm, ng * LANES), lambda i: (i, 0)),
                   pl.BlockSpec((ng * 2 * hg, tm), lambda i: (0, i))],
        out_shape=[jax.ShapeDtypeStruct((t, ng * LANES), F32), jax.ShapeDtypeStruct((ng * 2 * hg, t), F32)],
        compiler_params=_params("parallel"),
        name="mlstm_gates",
    )(h, wch, wcl, bc)


def _mlstm_body(q_ref, k_ref, v_ref, o_ref, gc_ref, gr_ref, gh_ref, tril_ref, triu_ref, out_ref,
                c_ref, n_ref, m_ref, *, scale, nchunk, hg, dqk, dv):
    L = ML_CHUNK
    row = lax.broadcasted_iota(jnp.int32, (L, L), 0)
    col = lax.broadcasted_iota(jnp.int32, (L, L), 1)
    heads = range(hg)
    stack = lambda parts: jnp.concatenate(parts, axis=0)
    rows_of = lambda x, h: x[h * L:(h + 1) * L]
    per_row = lambda scalars: stack([jnp.broadcast_to(s, (L, 1)) for s in scalars])
    tmask = stack([col <= row] * hg)

    @pl.when(pl.program_id(2) == 0)
    def _():
        c_ref[...] = jnp.zeros_like(c_ref)
        n_ref[...] = jnp.zeros_like(n_ref)
        m_ref[...] = jnp.zeros_like(m_ref)

    def cumsum_exact(parts, tri, left):
        acc = None
        for p in parts:
            d = (jnp.dot(tri, p, preferred_element_type=F32) if left
                 else jnp.dot(p, tri, preferred_element_type=F32))
            acc = d if acc is None else acc + d
        return acc

    def step(c, carry):
        c0 = pl.multiple_of(c * L, L)
        qc = [(q_ref[pl.ds(c0, L), h * dqk:(h + 1) * dqk].astype(F32) * scale).astype(BF16) for h in heads]
        kc = [k_ref[pl.ds(c0, L), h * dqk:(h + 1) * dqk] for h in heads]
        vc = [v_ref[pl.ds(c0, L), h * dv:(h + 1) * dv] for h in heads]
        gc = gc_ref[pl.ds(c0, L), :]
        gr = gr_ref[:, pl.ds(c0, L)]
        i_col = stack([gc[:, h:h + 1] for h in heads])
        f_col = stack([gc[:, hg + h:hg + h + 1] for h in heads])
        m_col = m_ref[...]
        m_prev = [m_col[h * L:h * L + 1] for h in heads]

        f_parts = _split3(jnp.broadcast_to(f_col, (hg * L, L)))
        b_col = stack([cumsum_exact([rows_of(p, h) for p in f_parts], tril_ref[...], True) for h in heads])
        u_rows = []
        for h in heads:
            f_row = gr[hg + h:hg + h + 1, :]
            b_row = cumsum_exact(_split3(jnp.broadcast_to(f_row, (SUBLANES, L))), triu_ref[...], False)[0:1, :]
            u_rows.append(jnp.broadcast_to(gr[h:h + 1, :] - b_row, (L, L)))
        bt = b_col[:, 0:1]
        b_last = [b_col[h * L + L - 1:(h + 1) * L, 0:1] for h in heads]

        d_mat = jnp.where(tmask, b_col + stack(u_rows), -1e30)
        m_t = jnp.maximum(bt + m_col, jnp.max(d_mat, axis=1, keepdims=True))
        e = jnp.where(tmask, jnp.exp(d_mat - m_t), 0.0)
        w = stack([lax.dot_general(qc[h], kc[h], NT_DIMS, preferred_element_type=F32) for h in heads]) * e
        wb = w.astype(BF16)
        inter = jnp.exp(bt + m_col - m_t)
        c_prev = [c_ref[h] for h in heads]
        n_prev = [n_ref[h] for h in heads]
        num = inter * stack([jnp.dot(qc[h], c_prev[h].astype(BF16), preferred_element_type=F32) for h in heads]) \
            + stack([jnp.dot(rows_of(wb, h), vc[h], preferred_element_type=F32) for h in heads])
        qn = jnp.sum(stack([qc[h].astype(F32) * n_prev[h] for h in heads]), axis=1, keepdims=True)
        den = inter * qn + jnp.sum(w, axis=1, keepdims=True)
        hh = num / jnp.maximum(jnp.abs(den), jnp.exp(-m_t))

        gains = stack([jnp.broadcast_to(gh_ref[:, h * dv:(h + 1) * dv], (L, dv)) for h in heads])
        hs = hh * lax.rsqrt(jnp.mean(hh * hh, axis=1, keepdims=True) + EPS) * gains
        og = stack([o_ref[pl.ds(c0, L), h * dv:(h + 1) * dv] for h in heads]).astype(F32)
        gated = (hs / (1.0 + jnp.exp(-og))).astype(out_ref.dtype)
        for h in heads:
            out_ref[pl.ds(c0, L), h * dv:(h + 1) * dv] = rows_of(gated, h)

        key_log = per_row(b_last) - bt + i_col
        m_new = [jnp.maximum(b_last[h] + m_prev[h], jnp.max(rows_of(key_log, h), axis=0, keepdims=True))
                 for h in heads]
        m_new_col = per_row(m_new)
        kd = stack([kc[h].astype(F32) for h in heads]) * jnp.exp(key_log - m_new_col)
        for h in heads:
            keep = jnp.exp(b_last[h] + m_prev[h] - m_new[h])
            kd_h = rows_of(kd, h)
            c_ref[h] = keep * c_prev[h] + jnp.dot(kd_h.T.astype(BF16), vc[h], preferred_element_type=F32)
            n_ref[h] = keep * n_prev[h] + jnp.sum(kd_h, axis=0, keepdims=True)
        m_ref[...] = m_new_col
        return carry

    lax.fori_loop(0, nchunk, step, 0)


def mlstm(qkvo, gcol, grow, g_head, *, batch, seq, heads, dqk, dv, hg):
    t = batch * seq
    nh = heads
    L = ML_CHUNK
    nchunk = seq // L
    idx = jnp.arange(L)
    tril = (idx[None, :] <= idx[:, None]).astype(BF16)
    triu = (idx[:, None] <= idx[None, :]).astype(BF16)
    assert nh % hg == 0 and seq % (ML_SEGMENTS * L) == 0
    ng = nh // hg
    ns = ML_SEGMENTS
    sl = seq // ns
    body = functools.partial(_mlstm_body, scale=dqk ** -0.5, nchunk=sl // L, hg=hg, dqk=dqk, dv=dv)
    v_blk0 = 2 * nh * dqk // (hg * dv)
    o_blk0 = v_blk0 + ng
    const = lambda b, g, s: (0, 0)
    return pl.pallas_call(
        body,
        grid=(batch, ng, ns),
        in_specs=[
            pl.BlockSpec((sl, hg * dqk), lambda b, g, s: (b * ns + s, g)),
            pl.BlockSpec((sl, hg * dqk), lambda b, g, s: (b * ns + s, ng + g)),
            pl.BlockSpec((sl, hg * dv), lambda b, g, s: (b * ns + s, v_blk0 + g)),
            pl.BlockSpec((sl, hg * dv), lambda b, g, s: (b * ns + s, o_blk0 + g)),
            pl.BlockSpec((sl, LANES), lambda b, g, s: (b * ns + s, g)),
            pl.BlockSpec((2 * hg, sl), lambda b, g, s: (g, b * ns + s)),
            pl.BlockSpec((1, hg * dv), lambda b, g, s: (0, g)),
            pl.BlockSpec((L, L), const),
            pl.BlockSpec((L, L), const),
        ],
        out_specs=pl.BlockSpec((sl, hg * dv), lambda b, g, s: (b * ns + s, g)),
        out_shape=jax.ShapeDtypeStruct((t, nh * dv), BF16),
        scratch_shapes=[pltpu.VMEM((hg, dqk, dv), F32), pltpu.VMEM((hg, 1, dqk), F32), pltpu.VMEM((hg * L, 1), F32)],
        compiler_params=_params("parallel", "parallel", "arbitrary"),
        name="mlstm",
    )(qkvo, qkvo, qkvo, qkvo, gcol, grow, g_head.reshape(1, nh * dv), tril, triu)


def _moe(o, w_out, x, norm_ffn, w_group, b_group, w_expert, b_expert, w1, w3, w2, layer, g_next, *, final,
         tm=256):
    n_tok = x.shape[0]
    x1, ri, rw, cnt = proj_route(o, w_out, x, norm_ffn, w_group, b_group, w_expert, b_expert)
    pos, tables, nt_max = _route_tables(ri, cnt, tm=tm, n_tok=n_tok)
    xs = dispatch_rows(x1, norm_ffn, pos)
    ys = routed_experts(xs, tables, w1, w3, w2, layer, tm=tm, nt_max=nt_max, n_tok=n_tok)
    return moe_combine(x1, ys, pos, rw, g_next, final=final)


def kernel(x, norm_mix, norm_ffn, norm_final, sb_w_in, sb_w_out, ml_w_in, ml_b_i, ml_b_f, ml_g_head, ml_w_out,
           moe_w_group, moe_b_group, moe_w_expert, moe_b_expert, moe_w1, moe_w3, moe_w2):
    batch, seq, d = x.shape
    xt = x.reshape(batch * seq, d)

    h = rmsnorm_bf16(xt, norm_mix[0])
    qkv = matmul_bf16(h, sb_w_in, 3 * d)
    o = sb_attention(qkv, batch=batch, seq=seq, heads=SB_HEADS, dh=d // SB_HEADS)
    xt, h = _moe(o, sb_w_out, xt, norm_ffn[0], moe_w_group[0], moe_b_group[0], moe_w_expert[0],
                 moe_b_expert[0], moe_w1, moe_w3, moe_w2, 0, norm_mix[1], final=False)

    dqk = (d // 2) // ML_HEADS
    dv = d // ML_HEADS
    n_main = 2 * ML_HEADS * dqk + 2 * ML_HEADS * dv
    w_in_t = jnp.swapaxes(ml_w_in, 1, 2)
    qkvo = matmul_bf16(h, w_in_t, n_main, w_rows_are_outputs=True)
    gcol, grow = mlstm_gates(h, w_in_t[0, n_main:], ml_b_i[0], ml_b_f[0], hg=ML_HEAD_GROUP)
    o = mlstm(qkvo, gcol, grow, ml_g_head[0], batch=batch, seq=seq, heads=ML_HEADS, dqk=dqk, dv=dv,
              hg=ML_HEAD_GROUP)
    out = _moe(o, ml_w_out, xt, norm_ffn[1], moe_w_group[1], moe_b_group[1], moe_w_expert[1],
               moe_b_expert[1], moe_w1, moe_w3, moe_w2, 1, norm_final, final=True)
    return out.reshape(batch, seq, d)
```

```python
import functools

import jax
import jax.numpy as jnp
from jax import lax
from jax.experimental import pallas as pl
from jax.experimental.pallas import tpu as pltpu

F32 = jnp.float32
BF16 = jnp.bfloat16
EPS = 1e-6

V7X_VMEM_BYTES = 64 * 1024 * 1024
VMEM_LIMIT_BYTES = V7X_VMEM_BYTES - 8 * 1024 * 1024
LANES = 128
SUBLANES = 8

SB_HEADS = 16
ML_HEADS = 8
N_GROUPS = 4
EXPERTS_PER_GROUP = 8
N_EXPERTS = N_GROUPS * EXPERTS_PER_GROUP
TOP_K = 2
ML_CHUNK = 128
ML_HEAD_GROUP = 8
ML_SEGMENTS = 2
SB_BLOCK = 128
ROUTER_ROWS = 128
LOG2_E = 1.4426950408889634
EXP2_F32_UNDERFLOW = -150.5
DMA_PRIORITIES = 2
SLAB_SPARE_ROWS = 1
COMBINE_SLOTS = 3

NT_DIMS = (((1,), (1,)), ((), ()))


def _params(*sem):
    return pltpu.CompilerParams(dimension_semantics=sem, vmem_limit_bytes=VMEM_LIMIT_BYTES)


def _split2(x):
    hi = x.astype(BF16)
    lo = (x - hi.astype(F32)).astype(BF16)
    return hi, lo


def _split3(x):
    hi = x.astype(BF16)
    r = x - hi.astype(F32)
    mid = r.astype(BF16)
    lo = (r - mid.astype(F32)).astype(BF16)
    return hi, mid, lo


def _rms(x, g):
    return x * lax.rsqrt(jnp.mean(x * x, axis=-1, keepdims=True) + EPS) * g


def _slab_rows(d):
    return d // LANES


def _slab_pitch(d):
    return _slab_rows(d) + SLAB_SPARE_ROWS


def _slab_store(ref, x, n):
    d = x.shape[1]
    pitch = _slab_pitch(d)
    for c in range(_slab_rows(d)):
        ref[pl.ds(c, n, stride=pitch), :] = x[:, c * LANES:(c + 1) * LANES]
    for c in range(_slab_rows(d), pitch):
        ref[pl.ds(c, n, stride=pitch), :] = jnp.zeros((n, LANES), x.dtype)


def _slab_load(ref, lead, n, d):
    pitch = _slab_pitch(d)
    return jnp.concatenate([ref[lead + (pl.ds(c, n, stride=pitch), slice(None))] for c in range(_slab_rows(d))],
                           axis=1)


def _rmsnorm_body(x_ref, g_ref, h_ref):
    h_ref[...] = _rms(x_ref[...], g_ref[...]).astype(h_ref.dtype)


def rmsnorm_bf16(x, g, *, tm=512):
    t, d = x.shape
    return pl.pallas_call(
        _rmsnorm_body,
        grid=(t // tm,),
        in_specs=[pl.BlockSpec((tm, d), lambda i: (i, 0)), pl.BlockSpec((1, d), lambda i: (0, 0))],
        out_specs=pl.BlockSpec((tm, d), lambda i: (i, 0)),
        out_shape=jax.ShapeDtypeStruct((t, d), BF16),
        compiler_params=_params("parallel"),
        name="rmsnorm",
    )(x, g.reshape(1, d))


def _mm_body(h_ref, w_ref, o_ref, wb_ref, *, w_rows_are_outputs):
    @pl.when(pl.program_id(1) == 0)
    def _():
        wb_ref[...] = w_ref[...].astype(BF16)

    if w_rows_are_outputs:
        acc = lax.dot_general(h_ref[...], wb_ref[...], NT_DIMS, preferred_element_type=F32)
    else:
        acc = jnp.dot(h_ref[...], wb_ref[...], preferred_element_type=F32)
    o_ref[...] = acc.astype(o_ref.dtype)


def matmul_bf16(h, w, n, *, w_rows_are_outputs=False, tm=2048, tn=1024):
    t, k = h.shape
    if w_rows_are_outputs:
        w_spec, w_block = pl.BlockSpec((None, tn, k), lambda j, i: (0, j, 0)), (tn, k)
    else:
        w_spec, w_block = pl.BlockSpec((None, k, tn), lambda j, i: (0, 0, j)), (k, tn)
    return pl.pallas_call(
        functools.partial(_mm_body, w_rows_are_outputs=w_rows_are_outputs),
        grid=(n // tn, t // tm),
        in_specs=[pl.BlockSpec((tm, k), lambda j, i: (i, 0)), w_spec],
        out_specs=pl.BlockSpec((tm, tn), lambda j, i: (i, j)),
        out_shape=jax.ShapeDtypeStruct((t, n), BF16),
        scratch_shapes=[pltpu.VMEM(w_block, BF16)],
        compiler_params=_params("parallel", "arbitrary"),
        name="proj_in",
    )(h, w)


def _sb_attn_body(q_ref, k_ref, v_ref, cm_ref, o_ref, *, scale, nblk, hg, kc):
    blk = SB_BLOCK
    dh = SB_BLOCK
    wid = kc * blk
    rel = lax.broadcasted_iota(jnp.int32, (blk, wid), 1) - lax.broadcasted_iota(jnp.int32, (blk, wid), 0)

    def chunk(qbs, q0, c, rs, accs, masked):
        k0 = pl.multiple_of(c * wid, wid)
        if masked:
            valid = rel < (q0 - k0)
        z = jnp.concatenate(
            [lax.dot_general(qbs[h], k_ref[pl.ds(k0, wid), h * dh:(h + 1) * dh], NT_DIMS,
                             preferred_element_type=F32) for h in range(hg)], axis=1)
        sp = jnp.maximum(z, 0.0) + jnp.log(1.0 + jnp.exp2(-jnp.abs(z))) * LOG2_E
        log_beta = z - sp
        if masked:
            valid_all = jnp.concatenate([valid] * hg, axis=1)
            sp = jnp.where(valid_all, sp, 0.0)
        nb = hg * kc
        x = jnp.concatenate([sp[:, j * blk:(j + 1) * blk] for j in range(nb)], axis=0).astype(BF16)
        cs = jnp.dot(x, cm_ref[...], preferred_element_type=F32)
        new_rs = []
        suffix = [None] * nb
        for h in range(hg):
            r = rs[h]
            for j in reversed(range(kc)):
                csj = cs[(h * kc + j) * blk:(h * kc + j + 1) * blk]
                suffix[h * kc + j] = csj[:, :blk] + r
                r = r + csj[:, blk:]
            new_rs.append(r)
        a = jnp.exp2(log_beta + jnp.concatenate(suffix, axis=1))
        if masked:
            a = jnp.where(valid_all, a, 0.0)
        a = a.astype(BF16)
        new_accs = [accs[h] + jnp.dot(a[:, h * wid:(h + 1) * wid], v_ref[pl.ds(k0, wid), h * dh:(h + 1) * dh],
                                      preferred_element_type=F32) for h in range(hg)]
        return tuple(new_rs), tuple(new_accs)

    def q_step(qi, carry):
        q0 = pl.multiple_of(qi * blk, blk)
        qbs = [(q_ref[pl.ds(q0, blk), h * dh:(h + 1) * dh].astype(F32) * scale).astype(BF16) for h in range(hg)]
        zeros = tuple(jnp.zeros((blk, blk), F32) for _ in range(hg))
        top = qi // kc
        rs, accs = chunk(qbs, q0, top, zeros, zeros, True)

        def live(rs):
            return jnp.max(functools.reduce(jnp.maximum, rs)) > EXP2_F32_UNDERFLOW

        def k_cond(c):
            return (c[0] < top) & c[1]

        def k_step(c):
            rs, accs = chunk(qbs, q0, top - 1 - c[0], c[2], c[3], False)
            return c[0] + 1, live(rs), rs, accs

        accs = lax.while_loop(k_cond, k_step, (jnp.int32(0), live(rs), rs, accs))[3]
        for h in range(hg):
            o_ref[pl.ds(q0, blk), h * dh:(h + 1) * dh] = accs[h].astype(o_ref.dtype)
        return carry

    lax.fori_loop(0, nblk, q_step, 0)


def _cumsum_matrix(blk):
    j = jnp.arange(blk)[:, None]
    c = jnp.arange(2 * blk)[None, :]
    return jnp.where((c >= blk) | (j > c), -1.0, 0.0).astype(BF16)


def sb_attention(qkv, *, batch, seq, heads, dh, hg=8, kc=2):
    assert dh == SB_BLOCK
    nblk = seq // SB_BLOCK
    assert nblk % kc == 0 and heads % hg == 0
    body = functools.partial(_sb_attn_body, scale=dh ** -0.5 * LOG2_E, nblk=nblk, hg=hg, kc=kc)
    cm = _cumsum_matrix(SB_BLOCK)
    ng = heads // hg
    return pl.pallas_call(
        body,
        grid=(batch, ng),
        in_specs=[
            pl.BlockSpec((seq, hg * dh), lambda b, h: (b, h)),
            pl.BlockSpec((seq, hg * dh), lambda b, h: (b, ng + h)),
            pl.BlockSpec((seq, hg * dh), lambda b, h: (b, 2 * ng + h)),
            pl.BlockSpec((SB_BLOCK, 2 * SB_BLOCK), lambda b, h: (0, 0)),
        ],
        out_specs=pl.BlockSpec((seq, hg * dh), lambda b, h: (b, h)),
        out_shape=jax.ShapeDtypeStruct((batch * seq, heads * dh), BF16),
        compiler_params=_params("parallel", "parallel"),
        name="sb_attn",
    )(qkv, qkv, qkv, cm)


def _proj_route_body(o_ref, w_ref, x_ref, g_ref, wrh_ref, wrl_ref, br_ref, tri_ref,
                     x1_ref, ri_ref, rw_ref, cnt_ref, base_ref, xprev_ref, wb_ref, *, tm):
    i = pl.program_id(0)
    n = pl.num_programs(0) - 1

    @pl.when(i == 0)
    def _():
        base_ref[...] = jnp.zeros_like(base_ref)
        wb_ref[...] = w_ref[...].astype(BF16)

    @pl.when(i > 0)
    def _():
        _route_tile(xprev_ref[...], g_ref, wrh_ref, wrl_ref, br_ref, tri_ref, ri_ref, rw_ref, cnt_ref, base_ref, tm=tm)

    @pl.when(i < n)
    def _():
        x1 = x_ref[...] + jnp.dot(o_ref[...], wb_ref[...], preferred_element_type=F32)
        x1_ref[...] = x1
        xprev_ref[...] = x1


def _route_tile(x1, g_ref, wrh_ref, wrl_ref, br_ref, tri_ref, ri_ref, rw_ref, cnt_ref, base_ref, *, tm):
    h2 = _rms(x1, g_ref[...])

    hh, hl = _split2(h2)
    wrh = wrh_ref[...]
    lg = (lax.dot_general(wrh, hh, NT_DIMS, preferred_element_type=F32)
          + lax.dot_general(wrh, hl, NT_DIMS, preferred_element_type=F32)
          + lax.dot_general(wrl_ref[...], hh, NT_DIMS, preferred_element_type=F32)) + br_ref[...]

    r8 = lax.broadcasted_iota(jnp.int32, (SUBLANES, tm), 0)
    neg_inf = jnp.float32(-jnp.inf)
    gl = jnp.where(r8 < N_GROUPS, lg[N_EXPERTS:N_EXPERTS + SUBLANES, :], neg_inf)
    gmax = jnp.max(gl, axis=0, keepdims=True)
    gsel = jnp.min(jnp.where(gl == gmax, r8, SUBLANES), axis=0, keepdims=True)
    p_sel = 1.0 / jnp.sum(jnp.exp(gl - gmax), axis=0, keepdims=True)

    es = lg[0:EXPERTS_PER_GROUP, :]
    for g in range(1, N_GROUPS):
        es = jnp.where(gsel == g, lg[g * EXPERTS_PER_GROUP:(g + 1) * EXPERTS_PER_GROUP, :], es)
    m1 = jnp.max(es, axis=0, keepdims=True)
    i1 = jnp.min(jnp.where(es == m1, r8, SUBLANES), axis=0, keepdims=True)
    es2 = jnp.where(r8 == i1, neg_inf, es)
    m2 = jnp.max(es2, axis=0, keepdims=True)
    i2 = jnp.min(jnp.where(es2 == m2, r8, SUBLANES), axis=0, keepdims=True)
    t = jnp.exp(m2 - m1)
    w1 = p_sel / (1.0 + t)
    w2 = w1 * t
    e1 = gsel * EXPERTS_PER_GROUP + i1
    e2 = gsel * EXPERTS_PER_GROUP + i2

    r32 = lax.broadcasted_iota(jnp.int32, (N_EXPERTS, tm), 0)
    hit1 = r32 == e1
    hit2 = r32 == e2
    onehot = (jnp.where(hit1, 1.0, 0.0) + jnp.where(hit2, 1.0, 0.0)).astype(BF16)
    cs = jnp.dot(onehot, tri_ref[...], preferred_element_type=F32)
    base = base_ref[...]
    before = cs[:, :tm] + base
    rank1 = jnp.sum(jnp.where(hit1, before, 0.0), axis=0, keepdims=True)
    rank2 = jnp.sum(jnp.where(hit2, before, 0.0), axis=0, keepdims=True)
    tot = cs[:, tm:]
    new_base = base + jnp.concatenate([tot] * (tm // LANES), axis=1)
    base_ref[...] = new_base
    cnt_ref[...] = new_base[:, :LANES]

    zi = jnp.zeros((SUBLANES, tm), jnp.int32)
    ri = jnp.where(r8 == 0, e1, zi)
    ri = jnp.where(r8 == 1, e2, ri)
    ri = jnp.where(r8 == 2, rank1.astype(jnp.int32), ri)
    ri = jnp.where(r8 == 3, rank2.astype(jnp.int32), ri)
    ri_ref[...] = ri
    zf = jnp.zeros((SUBLANES, tm), F32)
    rw = jnp.where(r8 == 0, w1, zf)
    rw = jnp.where(r8 == 1, w2, rw)
    rw_ref[...] = rw


def proj_route(o, w_out, x, g, w_group, b_group, w_expert, b_expert, *, tm=256):
    t, d = x.shape
    k = o.shape[1]
    wr = jnp.zeros((ROUTER_ROWS, d), F32)
    wr = wr.at[:N_EXPERTS].set(w_expert.T).at[N_EXPERTS:N_EXPERTS + N_GROUPS].set(w_group.T)
    wrh, wrl = _split2(wr)
    br = jnp.zeros((ROUTER_ROWS,), F32).at[:N_EXPERTS].set(b_expert).at[N_EXPERTS:N_EXPERTS + N_GROUPS].set(b_group)
    br = jnp.broadcast_to(br[:, None], (ROUTER_ROWS, tm))
    tp = jnp.arange(tm)[:, None]
    tri = jnp.concatenate([(tp < jnp.arange(tm)[None, :]).astype(BF16), jnp.ones((tm, LANES), BF16)], axis=1)
    body = functools.partial(_proj_route_body, tm=tm)
    n = t // tm
    row = lambda i: (jnp.minimum(i, n - 1), 0)
    routed = lambda i: (0, jnp.maximum(i - 1, 0))
    const = lambda i: (0, 0)
    return pl.pallas_call(
        body,
        grid=(n + 1,),
        in_specs=[
            pl.BlockSpec((tm, k), row),
            pl.BlockSpec((None, k, d), lambda i: (0, 0, 0), pipeline_mode=pl.Buffered(1)),
            pl.BlockSpec((tm, d), row),
            pl.BlockSpec((1, d), const),
            pl.BlockSpec((ROUTER_ROWS, d), const),
            pl.BlockSpec((ROUTER_ROWS, d), const),
            pl.BlockSpec((ROUTER_ROWS, tm), const),
            pl.BlockSpec((tm, tm + LANES), const),
        ],
        out_specs=[
            pl.BlockSpec((tm, d), row),
            pl.BlockSpec((SUBLANES, tm), routed),
            pl.BlockSpec((SUBLANES, tm), routed),
            pl.BlockSpec((N_EXPERTS, LANES), const),
        ],
        out_shape=[
            jax.ShapeDtypeStruct((t, d), F32),
            jax.ShapeDtypeStruct((SUBLANES, t), jnp.int32),
            jax.ShapeDtypeStruct((SUBLANES, t), F32),
            jax.ShapeDtypeStruct((N_EXPERTS, LANES), F32),
        ],
        scratch_shapes=[pltpu.VMEM((N_EXPERTS, tm), F32), pltpu.VMEM((tm, d), F32), pltpu.VMEM((k, d), BF16)],
        compiler_params=_params("arbitrary"),
        name="proj_route",
    )(o, w_out, x, g.reshape(1, d), wrh, wrl, br, tri)


def _dispatch_body(pos_ref, x_ref, g_ref, xs_hbm, hs, sem, *, tm, n_pairs, spare):
    i = pl.program_id(0)
    n = pl.num_programs(0)
    slot = lax.rem(i, 2)
    d = x_ref.shape[1]
    pitch = _slab_pitch(d)

    def row_out(p, r, s):
        return pltpu.make_async_copy(hs.at[s, pl.ds(r * pitch, pitch)], xs_hbm.at[pl.ds(p * pitch, pitch)], sem.at[s])

    def wait_tile(s):
        for k in range(TOP_K):
            for r in range(tm):
                row_out(0, r, s).wait()

    def tail(s):
        return pltpu.make_async_copy(hs.at[s, pl.ds(0, spare * pitch)],
                                     xs_hbm.at[pl.ds(n_pairs * pitch, spare * pitch)], sem.at[s])

    @pl.when(i >= 2)
    def _():
        wait_tile(slot)

    _slab_store(hs.at[slot], _rms(x_ref[...], g_ref[...]), tm)
    for k in range(TOP_K):
        for r in range(tm):
            row_out(pos_ref[0, 0, k * tm + r], r, slot).start(priority=r % DMA_PRIORITIES)

    @pl.when(i == n - 1)
    def _():
        wait_tile(slot)

        @pl.when(i >= 1)
        def _():
            wait_tile(1 - slot)

        hs[slot] = jnp.zeros(hs.shape[1:], hs.dtype)
        tail(slot).start()
        tail(slot).wait()


def dispatch_rows(x1, g, pos, *, spare, tm=512):
    t, d = x1.shape
    nt = t // tm
    pitch = _slab_pitch(d)
    n_pairs = TOP_K * t
    assert spare <= tm
    pos3 = pos.reshape(TOP_K, nt, tm).transpose(1, 0, 2).reshape(nt, 1, TOP_K * tm)
    return pl.pallas_call(
        functools.partial(_dispatch_body, tm=tm, n_pairs=n_pairs, spare=spare),
        grid=(nt,),
        in_specs=[
            pl.BlockSpec((1, 1, TOP_K * tm), lambda i: (i, 0, 0), memory_space=pltpu.SMEM),
            pl.BlockSpec((tm, d), lambda i: (i, 0)),
            pl.BlockSpec((1, d), lambda i: (0, 0)),
        ],
        out_specs=pl.BlockSpec(memory_space=pl.ANY),
        out_shape=jax.ShapeDtypeStruct(((n_pairs + spare) * pitch, LANES), F32),
        scratch_shapes=[pltpu.VMEM((2, tm * pitch, LANES), F32), pltpu.SemaphoreType.DMA((2,))],
        compiler_params=_params("arbitrary"),
        name="dispatch",
    )(pos3, x1, g.reshape(1, d))


def _experts_body(te_ref, ts_ref, nt_ref, run_ref, nxt_ref, xs_hbm, w1_hbm, w3_hbm, w2_hbm, ys_hbm,
                  xbuf, ybuf, wf1, wf3, wf2, w1b, w3b, w2b, isem, osem, wsem, *, tm, d, n_pairs, layer):
    i = pl.program_id(0)
    n_tiles = nt_ref[0]
    slot = lax.rem(i, 2)
    pitch = _slab_pitch(d)

    def tile_in(j, s):
        return pltpu.make_async_copy(xs_hbm.at[pl.ds(ts_ref[j] * pitch, tm * pitch)], xbuf.at[s], isem.at[s])

    def tile_out(j, s):
        return pltpu.make_async_copy(ybuf.at[s], ys_hbm.at[pl.ds(ts_ref[j] * pitch, tm * pitch)], osem.at[s])

    def weights_in(e, s):
        return [pltpu.make_async_copy(w1_hbm.at[layer, e], wf1.at[s], wsem.at[s]),
                pltpu.make_async_copy(w3_hbm.at[layer, e], wf3.at[s], wsem.at[s]),
                pltpu.make_async_copy(w2_hbm.at[layer, e], wf2.at[s], wsem.at[s])]

    @pl.when(i == 0)
    def _():
        for c in weights_in(te_ref[0], 0):
            c.start()
        tile_in(0, 0).start()
        ybuf[1] = jnp.zeros(ybuf.shape[1:], ybuf.dtype)
        zero_tail = pltpu.make_async_copy(ybuf.at[1], ys_hbm.at[pl.ds(n_pairs * pitch, tm * pitch)], osem.at[1])
        zero_tail.start()
        zero_tail.wait()

    @pl.when(i < n_tiles)
    def _():
        tile_in(i, slot).wait()

        @pl.when(i + 1 < n_tiles)
        def _():
            tile_in(i + 1, 1 - slot).start(priority=DMA_PRIORITIES - 1)

        @pl.when((i == 0) | (te_ref[i] != te_ref[jnp.maximum(i - 1, 0)]))
        def _():
            ws = lax.rem(run_ref[i], 2)
            for c in weights_in(te_ref[i], ws):
                c.wait()
            w1b[...] = wf1[ws].astype(BF16)
            w3b[...] = wf3[ws].astype(BF16)
            w2b[...] = wf2[ws].astype(BF16)

            @pl.when(nxt_ref[i] >= 0)
            def _():
                for c in weights_in(nxt_ref[i], 1 - ws):
                    c.start()

        xb = _slab_load(xbuf, (slot,), tm, d).astype(BF16)
        a = jnp.dot(xb, w1b[...], preferred_element_type=F32)
        b = jnp.dot(xb, w3b[...], preferred_element_type=F32)
        hid = (a * (1.0 / (1.0 + jnp.exp(-a))) * b).astype(BF16)
        _slab_store(ybuf.at[slot], jnp.dot(hid, w2b[...], preferred_element_type=F32), tm)

        @pl.when(i >= 1)
        def _():
            tile_out(i - 1, 1 - slot).wait()

        tile_out(i, slot).start()

        @pl.when(i == n_tiles - 1)
        def _():
            tile_out(i, slot).wait()


def _route_tables(ri, cnt, *, tm, n_tok):
    nt_max = TOP_K * n_tok // tm + N_EXPERTS - 1
    counts = cnt[:, 0].astype(jnp.int32)
    ends = jnp.cumsum(counts)
    offs = ends - counts
    eid = jnp.arange(N_EXPERTS, dtype=jnp.int32)
    sel = ri[0:TOP_K, :, None] == eid
    pos = jnp.sum(jnp.where(sel, offs, 0), axis=-1) + ri[TOP_K:2 * TOP_K]
    tiles = (counts + tm - 1) // tm
    tile_ends = jnp.cumsum(tiles)
    n_tiles = tile_ends[-1].astype(jnp.int32)
    step = jnp.minimum(jnp.arange(nt_max, dtype=jnp.int32), n_tiles - 1)
    te = jnp.clip(jnp.sum((step[:, None] >= tile_ends[None, :]).astype(jnp.int32), axis=1), 0, N_EXPERTS - 1)
    first = jnp.sum(jnp.where(te[:, None] == eid, tile_ends - tiles, 0), axis=-1)
    ts = jnp.sum(jnp.where(te[:, None] == eid, offs, 0), axis=-1) + (step - first) * tm
    run = jnp.sum(((tiles > 0) & (eid < te[:, None])).astype(jnp.int32), axis=-1)
    after = jnp.sum(jnp.where(te[:, None] == eid, tile_ends, 0), axis=-1)
    te_after = jnp.clip(jnp.sum((after[:, None] >= tile_ends[None, :]).astype(jnp.int32), axis=1), 0, N_EXPERTS - 1)
    nxt = jnp.where(after < n_tiles, te_after, -1)
    tables = (te, ts.astype(jnp.int32), n_tiles.reshape(1), run.astype(jnp.int32), nxt.astype(jnp.int32))
    return pos, tables, nt_max


def routed_experts(xs, tables, w1, w3, w2, layer, *, tm, nt_max, n_tok):
    d, f = w1.shape[-2:]
    pitch = _slab_pitch(d)
    body = functools.partial(_experts_body, tm=tm, d=d, n_pairs=TOP_K * n_tok, layer=layer)
    any_space = pl.BlockSpec(memory_space=pl.ANY)
    grid_spec = pltpu.PrefetchScalarGridSpec(
        num_scalar_prefetch=len(tables),
        grid=(nt_max,),
        in_specs=[any_space, any_space, any_space, any_space],
        out_specs=any_space,
        scratch_shapes=[
            pltpu.VMEM((2, tm * pitch, LANES), F32),
            pltpu.VMEM((2, tm * pitch, LANES), F32),
            pltpu.VMEM((2, d, f), F32),
            pltpu.VMEM((2, d, f), F32),
            pltpu.VMEM((2, f, d), F32),
            pltpu.VMEM((d, f), BF16),
            pltpu.VMEM((d, f), BF16),
            pltpu.VMEM((f, d), BF16),
            pltpu.SemaphoreType.DMA((2,)),
            pltpu.SemaphoreType.DMA((2,)),
            pltpu.SemaphoreType.DMA((2,)),
        ],
    )
    return pl.pallas_call(
        body,
        grid_spec=grid_spec,
        out_shape=jax.ShapeDtypeStruct(xs.shape, F32),
        compiler_params=_params("arbitrary"),
        name="experts",
    )(*tables, xs, w1, w3, w2)


def _combine_body(pos_ref, nx1_ref, nx2_ref, x_ref, w_ref, g_ref, ys_hbm, *rest, tm, final, gate_groups):
    x2_ref = gate_in = gate_out = None
    if final:
        h_ref, ybuf, sem = rest
    elif gate_groups is None:
        x2_ref, h_ref, ybuf, sem = rest
    else:
        gate_in, (x2_ref, h_ref), gate_out, (ybuf, sem) = rest[0:3], rest[3:5], rest[5:7], rest[7:9]
    i = pl.program_id(0)
    n = pl.num_programs(0)
    slot = lax.rem(i, COMBINE_SLOTS)
    d = x_ref.shape[1]
    rows, pitch = _slab_rows(d), _slab_pitch(d)

    def row_in(src, k, r, s):
        return pltpu.make_async_copy(ys_hbm.at[pl.ds(src * pitch, rows)], ybuf.at[s, k, pl.ds(r * pitch, rows)],
                                     sem.at[s])

    def gather_start(idx_ref, s):
        for k in range(TOP_K):
            for r in range(tm):
                row_in(idx_ref[0, 0, k * tm + r], k, r, s).start(priority=r % DMA_PRIORITIES)

    def gather_wait(s):
        for k in range(TOP_K):
            for r in range(tm):
                row_in(0, k, r, s).wait()

    @pl.when(i == 0)
    def _():
        gather_start(pos_ref, 0)

        @pl.when(n > 1)
        def _():
            gather_start(nx1_ref, 1)

    gather_wait(slot)

    @pl.when(i + 2 < n)
    def _():
        gather_start(nx2_ref, lax.rem(i + 2, COMBINE_SLOTS))

    w = w_ref[...]
    x2 = x_ref[...] + w[:, 0:1] * _slab_load(ybuf, (slot, 0), tm, d) + w[:, 1:2] * _slab_load(ybuf, (slot, 1), tm, d)
    if x2_ref is not None:
        x2_ref[...] = x2
    h = _rms(x2, g_ref[...]).astype(h_ref.dtype)
    h_ref[...] = h
    if gate_groups is not None:
        _gates(h, *gate_in, *gate_out, hg=gate_groups[0], ng=gate_groups[1])


def moe_combine(x1, ys, pos, rw, g, *, final, gate_params=None, tm=256):
    t, d = x1.shape
    nt = t // tm
    w = rw[0:TOP_K].T
    pos3 = pos.reshape(TOP_K, nt, tm).transpose(1, 0, 2).reshape(nt, 1, TOP_K * tm)
    row = lambda i: (i, 0)
    const = lambda i: (0, 0)
    smem_tile = lambda ahead: pl.BlockSpec((1, 1, TOP_K * tm), lambda i: (jnp.minimum(i + ahead, nt - 1), 0, 0),
                                           memory_space=pltpu.SMEM)
    in_specs = [
        smem_tile(0), smem_tile(1), smem_tile(2),
        pl.BlockSpec((tm, d), row),
        pl.BlockSpec((tm, TOP_K), row),
        pl.BlockSpec((1, d), const),
        pl.BlockSpec(memory_space=pl.ANY),
    ]
    scratch = [pltpu.VMEM((COMBINE_SLOTS, TOP_K, tm * _slab_pitch(d), LANES), F32),
               pltpu.SemaphoreType.DMA((COMBINE_SLOTS,))]
    args = [pos3, pos3, pos3, x1, w, g.reshape(1, d), ys]
    if final:
        return pl.pallas_call(
            functools.partial(_combine_body, tm=tm, final=True, gate_groups=None), grid=(nt,), in_specs=in_specs,
            out_specs=pl.BlockSpec((tm, d), row),
            out_shape=jax.ShapeDtypeStruct((t, d), F32),
            scratch_shapes=scratch,
            compiler_params=_params("arbitrary"), name="combine_final",
        )(*args)
    out_specs = [pl.BlockSpec((tm, d), row), pl.BlockSpec((tm, d), row)]
    out_shape = [jax.ShapeDtypeStruct((t, d), F32), jax.ShapeDtypeStruct((t, d), BF16)]
    gate_groups = None
    if gate_params is not None:
        w_hi, w_lo, bias, hg, ng = gate_params
        gate_groups = (hg, ng)
        gw = ng * LANES
        in_specs += [pl.BlockSpec((gw, d), const), pl.BlockSpec((gw, d), const), pl.BlockSpec((1, gw), const)]
        args += [w_hi, w_lo, bias]
        out_specs += [pl.BlockSpec((tm, gw), row), pl.BlockSpec((ng * 2 * hg, tm), lambda i: (0, i))]
        out_shape += [jax.ShapeDtypeStruct((t, gw), F32), jax.ShapeDtypeStruct((ng * 2 * hg, t), F32)]
    return pl.pallas_call(
        functools.partial(_combine_body, tm=tm, final=False, gate_groups=gate_groups), grid=(nt,), in_specs=in_specs,
        out_specs=out_specs, out_shape=out_shape,
        scratch_shapes=scratch,
        compiler_params=_params("arbitrary"), name="combine",
    )(*args)


def _log_sigmoid(x):
    return jnp.minimum(x, 0.0) - jnp.log(1.0 + jnp.exp(-jnp.abs(x)))


def _gates(h, wch_ref, wcl_ref, bc_ref, gc_ref, gr_ref, *, hg, ng):
    col = (lax.dot_general(h, wch_ref[...], NT_DIMS, preferred_element_type=F32)
           + lax.dot_general(h, wcl_ref[...], NT_DIMS, preferred_element_type=F32)) + bc_ref[...]
    lane = lax.broadcasted_iota(jnp.int32, col.shape, 1) % LANES
    gates = jnp.where(lane < hg, col, _log_sigmoid(col))
    gc_ref[...] = gates
    gt = gates.T
    gr_ref[...] = jnp.concatenate([gt[g * LANES:g * LANES + 2 * hg] for g in range(ng)], axis=0)


def _gate_weights(w_if_t, b_i, b_f, hg):
    nh = b_i.shape[0]
    ng = nh // hg

    def grouped(x_i, x_f):
        pad = jnp.zeros((ng, LANES - 2 * hg) + x_i.shape[1:], F32)
        blocks = jnp.concatenate([x_i.reshape((ng, hg) + x_i.shape[1:]), x_f.reshape((ng, hg) + x_f.shape[1:]), pad],
                                 axis=1)
        return blocks.reshape((ng * LANES,) + x_i.shape[1:])

    w_hi, w_lo = _split2(grouped(w_if_t[:nh], w_if_t[nh:]))
    return w_hi, w_lo, grouped(b_i, b_f).reshape(1, ng * LANES), hg, ng


def _mlstm_body(q_ref, k_ref, v_ref, o_ref, gc_ref, gr_ref, gh_ref, tril_ref, triu_ref, out_ref,
                c_ref, n_ref, m_ref, *, scale, nchunk, hg, dqk, dv):
    L = ML_CHUNK
    row = lax.broadcasted_iota(jnp.int32, (L, L), 0)
    col = lax.broadcasted_iota(jnp.int32, (L, L), 1)
    heads = range(hg)
    stack = lambda parts: jnp.concatenate(parts, axis=0)
    rows_of = lambda x, h: x[h * L:(h + 1) * L]
    per_row = lambda scalars: stack([jnp.broadcast_to(s, (L, 1)) for s in scalars])
    tmask = stack([col <= row] * hg)

    @pl.when(pl.program_id(2) == 0)
    def _():
        c_ref[...] = jnp.zeros_like(c_ref)
        n_ref[...] = jnp.zeros_like(n_ref)
        m_ref[...] = jnp.zeros_like(m_ref)

    def cumsum_exact(parts, tri, left):
        acc = None
        for p in parts:
            d = (jnp.dot(tri, p, preferred_element_type=F32) if left
                 else jnp.dot(p, tri, preferred_element_type=F32))
            acc = d if acc is None else acc + d
        return acc

    def step(c, carry):
        c0 = pl.multiple_of(c * L, L)
        qc = [(q_ref[pl.ds(c0, L), h * dqk:(h + 1) * dqk].astype(F32) * scale).astype(BF16) for h in heads]
        kc = [k_ref[pl.ds(c0, L), h * dqk:(h + 1) * dqk] for h in heads]
        vc = [v_ref[pl.ds(c0, L), h * dv:(h + 1) * dv] for h in heads]
        gc = gc_ref[pl.ds(c0, L), :]
        gr = gr_ref[:, pl.ds(c0, L)]
        i_col = stack([gc[:, h:h + 1] for h in heads])
        f_col = stack([gc[:, hg + h:hg + h + 1] for h in heads])
        m_col = m_ref[...]
        m_prev = [m_col[h * L:h * L + 1] for h in heads]

        f_parts = _split3(jnp.broadcast_to(f_col, (hg * L, L)))
        b_col = stack([cumsum_exact([rows_of(p, h) for p in f_parts], tril_ref[...], True) for h in heads])
        u_rows = []
        for h in heads:
            f_row = gr[hg + h:hg + h + 1, :]
            b_row = cumsum_exact(_split3(jnp.broadcast_to(f_row, (SUBLANES, L))), triu_ref[...], False)[0:1, :]
            u_rows.append(jnp.broadcast_to(gr[h:h + 1, :] - b_row, (L, L)))
        bt = b_col[:, 0:1]
        b_last = [b_col[h * L + L - 1:(h + 1) * L, 0:1] for h in heads]

        d_mat = jnp.where(tmask, b_col + stack(u_rows), -1e30)
        m_t = jnp.maximum(bt + m_col, jnp.max(d_mat, axis=1, keepdims=True))
        e = jnp.where(tmask, jnp.exp(d_mat - m_t), 0.0)
        w = stack([lax.dot_general(qc[h], kc[h], NT_DIMS, preferred_element_type=F32) for h in heads]) * e
        wb = w.astype(BF16)
        inter = jnp.exp(bt + m_col - m_t)
        c_prev = [c_ref[h] for h in heads]
        n_prev = [n_ref[h] for h in heads]
        num = inter * stack([jnp.dot(qc[h], c_prev[h].astype(BF16), preferred_element_type=F32) for h in heads]) \
            + stack([jnp.dot(rows_of(wb, h), vc[h], preferred_element_type=F32) for h in heads])
        qn = jnp.sum(stack([qc[h].astype(F32) * n_prev[h] for h in heads]), axis=1, keepdims=True)
        den = inter * qn + jnp.sum(w, axis=1, keepdims=True)
        hh = num / jnp.maximum(jnp.abs(den), jnp.exp(-m_t))

        gains = stack([jnp.broadcast_to(gh_ref[:, h * dv:(h + 1) * dv], (L, dv)) for h in heads])
        hs = hh * lax.rsqrt(jnp.mean(hh * hh, axis=1, keepdims=True) + EPS) * gains
        og = stack([o_ref[pl.ds(c0, L), h * dv:(h + 1) * dv] for h in heads]).astype(F32)
        gated = (hs / (1.0 + jnp.exp(-og))).astype(out_ref.dtype)
        for h in heads:
            out_ref[pl.ds(c0, L), h * dv:(h + 1) * dv] = rows_of(gated, h)

        key_log = per_row(b_last) - bt + i_col
        m_new = [jnp.maximum(b_last[h] + m_prev[h], jnp.max(rows_of(key_log, h), axis=0, keepdims=True))
                 for h in heads]
        m_new_col = per_row(m_new)
        kd = stack([kc[h].astype(F32) for h in heads]) * jnp.exp(key_log - m_new_col)
        for h in heads:
            keep = jnp.exp(b_last[h] + m_prev[h] - m_new[h])
            kd_h = rows_of(kd, h)
            c_ref[h] = keep * c_prev[h] + jnp.dot(kd_h.T.astype(BF16), vc[h], preferred_element_type=F32)
            n_ref[h] = keep * n_prev[h] + jnp.sum(kd_h, axis=0, keepdims=True)
        m_ref[...] = m_new_col
        return carry

    lax.fori_loop(0, nchunk, step, 0)


def mlstm(qkvo, gcol, grow, g_head, *, batch, seq, heads, dqk, dv, hg):
    t = batch * seq
    nh = heads
    L = ML_CHUNK
    nchunk = seq // L
    idx = jnp.arange(L)
    tril = (idx[None, :] <= idx[:, None]).astype(BF16)
    triu = (idx[:, None] <= idx[None, :]).astype(BF16)
    assert nh % hg == 0 and seq % (ML_SEGMENTS * L) == 0
    ng = nh // hg
    ns = ML_SEGMENTS
    sl = seq // ns
    body = functools.partial(_mlstm_body, scale=dqk ** -0.5, nchunk=sl // L, hg=hg, dqk=dqk, dv=dv)
    v_blk0 = 2 * nh * dqk // (hg * dv)
    o_blk0 = v_blk0 + ng
    const = lambda b, g, s: (0, 0)
    return pl.pallas_call(
        body,
        grid=(batch, ng, ns),
        in_specs=[
            pl.BlockSpec((sl, hg * dqk), lambda b, g, s: (b * ns + s, g)),
            pl.BlockSpec((sl, hg * dqk), lambda b, g, s: (b * ns + s, ng + g)),
            pl.BlockSpec((sl, hg * dv), lambda b, g, s: (b * ns + s, v_blk0 + g)),
            pl.BlockSpec((sl, hg * dv), lambda b, g, s: (b * ns + s, o_blk0 + g)),
            pl.BlockSpec((sl, LANES), lambda b, g, s: (b * ns + s, g)),
            pl.BlockSpec((2 * hg, sl), lambda b, g, s: (g, b * ns + s)),
            pl.BlockSpec((1, hg * dv), lambda b, g, s: (0, g)),
            pl.BlockSpec((L, L), const),
            pl.BlockSpec((L, L), const),
        ],
        out_specs=pl.BlockSpec((sl, hg * dv), lambda b, g, s: (b * ns + s, g)),
        out_shape=jax.ShapeDtypeStruct((t, nh * dv), BF16),
        scratch_shapes=[pltpu.VMEM((hg, dqk, dv), F32), pltpu.VMEM((hg, 1, dqk), F32), pltpu.VMEM((hg * L, 1), F32)],
        compiler_params=_params("parallel", "parallel", "arbitrary"),
        name="mlstm",
    )(qkvo, qkvo, qkvo, qkvo, gcol, grow, g_head.reshape(1, nh * dv), tril, triu)


def _moe(o, w_out, x, norm_ffn, w_group, b_group, w_expert, b_expert, w1, w3, w2, layer, g_next, *, final,
         gate_params=None, tm=256):
    n_tok = x.shape[0]
    x1, ri, rw, cnt = proj_route(o, w_out, x, norm_ffn, w_group, b_group, w_expert, b_expert)
    pos, tables, nt_max = _route_tables(ri, cnt, tm=tm, n_tok=n_tok)
    xs = dispatch_rows(x1, norm_ffn, pos, spare=tm)
    ys = routed_experts(xs, tables, w1, w3, w2, layer, tm=tm, nt_max=nt_max, n_tok=n_tok)
    return moe_combine(x1, ys, pos, rw, g_next, final=final, gate_params=gate_params)


def kernel(x, norm_mix, norm_ffn, norm_final, sb_w_in, sb_w_out, ml_w_in, ml_b_i, ml_b_f, ml_g_head, ml_w_out,
           moe_w_group, moe_b_group, moe_w_expert, moe_b_expert, moe_w1, moe_w3, moe_w2):
    batch, seq, d = x.shape
    xt = x.reshape(batch * seq, d)

    h = rmsnorm_bf16(xt, norm_mix[0])
    qkv = matmul_bf16(h, sb_w_in, 3 * d)
    o = sb_attention(qkv, batch=batch, seq=seq, heads=SB_HEADS, dh=d // SB_HEADS)
    dqk = (d // 2) // ML_HEADS
    dv = d // ML_HEADS
    n_main = 2 * ML_HEADS * dqk + 2 * ML_HEADS * dv
    w_in_t = jnp.swapaxes(ml_w_in, 1, 2)
    gate_params = _gate_weights(w_in_t[0, n_main:], ml_b_i[0], ml_b_f[0], ML_HEAD_GROUP)
    xt, h, gcol, grow = _moe(o, sb_w_out, xt, norm_ffn[0], moe_w_group[0], moe_b_group[0], moe_w_expert[0],
                             moe_b_expert[0], moe_w1, moe_w3, moe_w2, 0, norm_mix[1], final=False,
                             gate_params=gate_params)

    qkvo = matmul_bf16(h, w_in_t, n_main, w_rows_are_outputs=True)
    o = mlstm(qkvo, gcol, grow, ml_g_head[0], batch=batch, seq=seq, heads=ML_HEADS, dqk=dqk, dv=dv,
              hg=ML_HEAD_GROUP)
    out = _moe(o, ml_w_out, xt, norm_ffn[1], moe_w_group[1], moe_b_group[1], moe_w_expert[1],
               moe_b_expert[1], moe_w1, moe_w3, moe_w2, 1, norm_final, final=True)
    return out.reshape(batch, seq, d)
```

```python
import functools

import jax
import jax.numpy as jnp
from jax import lax
from jax.experimental import pallas as pl
from jax.experimental.pallas import tpu as pltpu

F32 = jnp.float32
BF16 = jnp.bfloat16
EPS = 1e-6

V7X_VMEM_BYTES = 64 * 1024 * 1024
VMEM_LIMIT_BYTES = V7X_VMEM_BYTES - 8 * 1024 * 1024
LANES = 128
SUBLANES = 8

SB_HEADS = 16
ML_HEADS = 8
N_GROUPS = 4
EXPERTS_PER_GROUP = 8
N_EXPERTS = N_GROUPS * EXPERTS_PER_GROUP
TOP_K = 2
ML_CHUNK = 128
ML_HEAD_GROUP = 8
ML_SEGMENTS = 2
SB_BLOCK = 128
ROUTER_ROWS = 128
LOG2_E = 1.4426950408889634
EXP2_F32_UNDERFLOW = -150.5
DMA_PRIORITIES = 2
SLAB_SPARE_ROWS = 1
COMBINE_SLOTS = 3

TILE_NORM = 512
TILE_PROJ_IN = (2048, 1024)
TILE_PROJ_ROUTE = 512
TILE_DISPATCH = 512
TILE_EXPERT = 256
TILE_COMBINE = 256
SB_HEAD_GROUP = 8
SB_KEY_BLOCKS = 2

NT_DIMS = (((1,), (1,)), ((), ()))


def _params(*sem):
    return pltpu.CompilerParams(dimension_semantics=sem, vmem_limit_bytes=VMEM_LIMIT_BYTES)


def _split2(x):
    hi = x.astype(BF16)
    lo = (x - hi.astype(F32)).astype(BF16)
    return hi, lo


def _split3(x):
    hi = x.astype(BF16)
    r = x - hi.astype(F32)
    mid = r.astype(BF16)
    lo = (r - mid.astype(F32)).astype(BF16)
    return hi, mid, lo


def _rms(x, g):
    return x * lax.rsqrt(jnp.mean(x * x, axis=-1, keepdims=True) + EPS) * g


def _slab_rows(d):
    return d // LANES


def _slab_pitch(d):
    return _slab_rows(d) + SLAB_SPARE_ROWS


def _slab_store(ref, x, n):
    d = x.shape[1]
    pitch = _slab_pitch(d)
    for c in range(_slab_rows(d)):
        ref[pl.ds(c, n, stride=pitch), :] = x[:, c * LANES:(c + 1) * LANES]
    for c in range(_slab_rows(d), pitch):
        ref[pl.ds(c, n, stride=pitch), :] = jnp.zeros((n, LANES), x.dtype)


def _slab_load(ref, lead, n, d):
    pitch = _slab_pitch(d)
    return jnp.concatenate([ref[lead + (pl.ds(c, n, stride=pitch), slice(None))] for c in range(_slab_rows(d))],
                           axis=1)


def _rmsnorm_body(x_ref, g_ref, h_ref):
    h_ref[...] = _rms(x_ref[...], g_ref[...]).astype(h_ref.dtype)


def rmsnorm_bf16(x, g, *, tm=TILE_NORM):
    t, d = x.shape
    return pl.pallas_call(
        _rmsnorm_body,
        grid=(t // tm,),
        in_specs=[pl.BlockSpec((tm, d), lambda i: (i, 0)), pl.BlockSpec((1, d), lambda i: (0, 0))],
        out_specs=pl.BlockSpec((tm, d), lambda i: (i, 0)),
        out_shape=jax.ShapeDtypeStruct((t, d), BF16),
        compiler_params=_params("parallel"),
        name="rmsnorm",
    )(x, g.reshape(1, d))


def _mm_body(h_ref, w_ref, o_ref, wb_ref, *, w_rows_are_outputs):
    @pl.when(pl.program_id(1) == 0)
    def _():
        wb_ref[...] = w_ref[...].astype(BF16)

    if w_rows_are_outputs:
        acc = lax.dot_general(h_ref[...], wb_ref[...], NT_DIMS, preferred_element_type=F32)
    else:
        acc = jnp.dot(h_ref[...], wb_ref[...], preferred_element_type=F32)
    o_ref[...] = acc.astype(o_ref.dtype)


def matmul_bf16(h, w, n, *, w_rows_are_outputs=False, tm=TILE_PROJ_IN[0], tn=TILE_PROJ_IN[1]):
    t, k = h.shape
    if w_rows_are_outputs:
        w_spec, w_block = pl.BlockSpec((None, tn, k), lambda j, i: (0, j, 0)), (tn, k)
    else:
        w_spec, w_block = pl.BlockSpec((None, k, tn), lambda j, i: (0, 0, j)), (k, tn)
    return pl.pallas_call(
        functools.partial(_mm_body, w_rows_are_outputs=w_rows_are_outputs),
        grid=(n // tn, t // tm),
        in_specs=[pl.BlockSpec((tm, k), lambda j, i: (i, 0)), w_spec],
        out_specs=pl.BlockSpec((tm, tn), lambda j, i: (i, j)),
        out_shape=jax.ShapeDtypeStruct((t, n), BF16),
        scratch_shapes=[pltpu.VMEM(w_block, BF16)],
        compiler_params=_params("parallel", "arbitrary"),
        name="proj_in",
    )(h, w)


def _sb_attn_body(q_ref, k_ref, v_ref, cm_ref, o_ref, *, scale, nblk, hg, kc):
    blk = SB_BLOCK
    dh = SB_BLOCK
    wid = kc * blk
    rel = lax.broadcasted_iota(jnp.int32, (blk, wid), 1) - lax.broadcasted_iota(jnp.int32, (blk, wid), 0)

    def chunk(qbs, q0, c, rs, accs, masked):
        k0 = pl.multiple_of(c * wid, wid)
        if masked:
            valid = rel < (q0 - k0)
        z = jnp.concatenate(
            [lax.dot_general(qbs[h], k_ref[pl.ds(k0, wid), h * dh:(h + 1) * dh], NT_DIMS,
                             preferred_element_type=F32) for h in range(hg)], axis=1)
        sp = jnp.maximum(z, 0.0) + jnp.log(1.0 + jnp.exp2(-jnp.abs(z))) * LOG2_E
        log_beta = z - sp
        if masked:
            valid_all = jnp.concatenate([valid] * hg, axis=1)
            sp = jnp.where(valid_all, sp, 0.0)
        nb = hg * kc
        x = jnp.concatenate([sp[:, j * blk:(j + 1) * blk] for j in range(nb)], axis=0).astype(BF16)
        cs = jnp.dot(x, cm_ref[...], preferred_element_type=F32)
        new_rs = []
        suffix = [None] * nb
        for h in range(hg):
            r = rs[h]
            for j in reversed(range(kc)):
                csj = cs[(h * kc + j) * blk:(h * kc + j + 1) * blk]
                suffix[h * kc + j] = csj[:, :blk] + r
                r = r + csj[:, blk:]
            new_rs.append(r)
        a = jnp.exp2(log_beta + jnp.concatenate(suffix, axis=1))
        if masked:
            a = jnp.where(valid_all, a, 0.0)
        a = a.astype(BF16)
        new_accs = [accs[h] + jnp.dot(a[:, h * wid:(h + 1) * wid], v_ref[pl.ds(k0, wid), h * dh:(h + 1) * dh],
                                      preferred_element_type=F32) for h in range(hg)]
        return tuple(new_rs), tuple(new_accs)

    def q_step(qi, carry):
        q0 = pl.multiple_of(qi * blk, blk)
        qbs = [(q_ref[pl.ds(q0, blk), h * dh:(h + 1) * dh].astype(F32) * scale).astype(BF16) for h in range(hg)]
        zeros = tuple(jnp.zeros((blk, blk), F32) for _ in range(hg))
        top = qi // kc
        rs, accs = chunk(qbs, q0, top, zeros, zeros, True)

        def live(rs):
            return jnp.max(functools.reduce(jnp.maximum, rs)) > EXP2_F32_UNDERFLOW

        def k_cond(c):
            return (c[0] < top) & c[1]

        def k_step(c):
            rs, accs = chunk(qbs, q0, top - 1 - c[0], c[2], c[3], False)
            return c[0] + 1, live(rs), rs, accs

        accs = lax.while_loop(k_cond, k_step, (jnp.int32(0), live(rs), rs, accs))[3]
        for h in range(hg):
            o_ref[pl.ds(q0, blk), h * dh:(h + 1) * dh] = accs[h].astype(o_ref.dtype)
        return carry

    lax.fori_loop(0, nblk, q_step, 0)


def _cumsum_matrix(blk):
    j = jnp.arange(blk)[:, None]
    c = jnp.arange(2 * blk)[None, :]
    return jnp.where((c >= blk) | (j > c), -1.0, 0.0).astype(BF16)


def sb_attention(qkv, *, batch, seq, heads, dh, hg=SB_HEAD_GROUP, kc=SB_KEY_BLOCKS):
    assert dh == SB_BLOCK
    nblk = seq // SB_BLOCK
    assert nblk % kc == 0 and heads % hg == 0
    body = functools.partial(_sb_attn_body, scale=dh ** -0.5 * LOG2_E, nblk=nblk, hg=hg, kc=kc)
    cm = _cumsum_matrix(SB_BLOCK)
    ng = heads // hg
    return pl.pallas_call(
        body,
        grid=(batch, ng),
        in_specs=[
            pl.BlockSpec((seq, hg * dh), lambda b, h: (b, h)),
            pl.BlockSpec((seq, hg * dh), lambda b, h: (b, ng + h)),
            pl.BlockSpec((seq, hg * dh), lambda b, h: (b, 2 * ng + h)),
            pl.BlockSpec((SB_BLOCK, 2 * SB_BLOCK), lambda b, h: (0, 0)),
        ],
        out_specs=pl.BlockSpec((seq, hg * dh), lambda b, h: (b, h)),
        out_shape=jax.ShapeDtypeStruct((batch * seq, heads * dh), BF16),
        compiler_params=_params("parallel", "parallel"),
        name="sb_attn",
    )(qkv, qkv, qkv, cm)


def _proj_route_body(o_ref, w_ref, x_ref, g_ref, wrh_ref, wrl_ref, br_ref, tri_ref,
                     x1_ref, ri_ref, rw_ref, cnt_ref, base_ref, xprev_ref, wb_ref, *, tm):
    i = pl.program_id(0)
    n = pl.num_programs(0) - 1

    @pl.when(i == 0)
    def _():
        base_ref[...] = jnp.zeros_like(base_ref)
        wb_ref[...] = w_ref[...].astype(BF16)

    @pl.when(i > 0)
    def _():
        _route_tile(xprev_ref[...], g_ref, wrh_ref, wrl_ref, br_ref, tri_ref, ri_ref, rw_ref, cnt_ref, base_ref, tm=tm)

    @pl.when(i < n)
    def _():
        x1 = x_ref[...] + jnp.dot(o_ref[...], wb_ref[...], preferred_element_type=F32)
        x1_ref[...] = x1
        xprev_ref[...] = x1


def _route_tile(x1, g_ref, wrh_ref, wrl_ref, br_ref, tri_ref, ri_ref, rw_ref, cnt_ref, base_ref, *, tm):
    h2 = _rms(x1, g_ref[...])

    hh, hl = _split2(h2)
    wrh = wrh_ref[...]
    lg = (lax.dot_general(wrh, hh, NT_DIMS, preferred_element_type=F32)
          + lax.dot_general(wrh, hl, NT_DIMS, preferred_element_type=F32)
          + lax.dot_general(wrl_ref[...], hh, NT_DIMS, preferred_element_type=F32)) + br_ref[...]

    r8 = lax.broadcasted_iota(jnp.int32, (SUBLANES, tm), 0)
    neg_inf = jnp.float32(-jnp.inf)
    gl = jnp.where(r8 < N_GROUPS, lg[N_EXPERTS:N_EXPERTS + SUBLANES, :], neg_inf)
    gmax = jnp.max(gl, axis=0, keepdims=True)
    gsel = jnp.min(jnp.where(gl == gmax, r8, SUBLANES), axis=0, keepdims=True)
    p_sel = 1.0 / jnp.sum(jnp.exp(gl - gmax), axis=0, keepdims=True)

    es = lg[0:EXPERTS_PER_GROUP, :]
    for g in range(1, N_GROUPS):
        es = jnp.where(gsel == g, lg[g * EXPERTS_PER_GROUP:(g + 1) * EXPERTS_PER_GROUP, :], es)
    m1 = jnp.max(es, axis=0, keepdims=True)
    i1 = jnp.min(jnp.where(es == m1, r8, SUBLANES), axis=0, keepdims=True)
    es2 = jnp.where(r8 == i1, neg_inf, es)
    m2 = jnp.max(es2, axis=0, keepdims=True)
    i2 = jnp.min(jnp.where(es2 == m2, r8, SUBLANES), axis=0, keepdims=True)
    t = jnp.exp(m2 - m1)
    w1 = p_sel / (1.0 + t)
    w2 = w1 * t
    e1 = gsel * EXPERTS_PER_GROUP + i1
    e2 = gsel * EXPERTS_PER_GROUP + i2

    r32 = lax.broadcasted_iota(jnp.int32, (N_EXPERTS, tm), 0)
    hit1 = r32 == e1
    hit2 = r32 == e2
    onehot = (jnp.where(hit1, 1.0, 0.0) + jnp.where(hit2, 1.0, 0.0)).astype(BF16)
    cs = jnp.dot(onehot, tri_ref[...], preferred_element_type=F32)
    base = base_ref[...]
    before = cs[:, :tm] + base
    rank1 = jnp.sum(jnp.where(hit1, before, 0.0), axis=0, keepdims=True)
    rank2 = jnp.sum(jnp.where(hit2, before, 0.0), axis=0, keepdims=True)
    tot = cs[:, tm:]
    new_base = base + jnp.concatenate([tot] * (tm // LANES), axis=1)
    base_ref[...] = new_base
    cnt_ref[...] = new_base[:, :LANES]

    zi = jnp.zeros((SUBLANES, tm), jnp.int32)
    ri = jnp.where(r8 == 0, e1, zi)
    ri = jnp.where(r8 == 1, e2, ri)
    ri = jnp.where(r8 == 2, rank1.astype(jnp.int32), ri)
    ri = jnp.where(r8 == 3, rank2.astype(jnp.int32), ri)
    ri_ref[...] = ri
    zf = jnp.zeros((SUBLANES, tm), F32)
    rw = jnp.where(r8 == 0, w1, zf)
    rw = jnp.where(r8 == 1, w2, rw)
    rw_ref[...] = rw


def proj_route(o, w_out, x, g, w_group, b_group, w_expert, b_expert, *, tm=TILE_PROJ_ROUTE):
    t, d = x.shape
    k = o.shape[1]
    wr = jnp.zeros((ROUTER_ROWS, d), F32)
    wr = wr.at[:N_EXPERTS].set(w_expert.T).at[N_EXPERTS:N_EXPERTS + N_GROUPS].set(w_group.T)
    wrh, wrl = _split2(wr)
    br = jnp.zeros((ROUTER_ROWS,), F32).at[:N_EXPERTS].set(b_expert).at[N_EXPERTS:N_EXPERTS + N_GROUPS].set(b_group)
    br = jnp.broadcast_to(br[:, None], (ROUTER_ROWS, tm))
    tp = jnp.arange(tm)[:, None]
    tri = jnp.concatenate([(tp < jnp.arange(tm)[None, :]).astype(BF16), jnp.ones((tm, LANES), BF16)], axis=1)
    body = functools.partial(_proj_route_body, tm=tm)
    n = t // tm
    row = lambda i: (jnp.minimum(i, n - 1), 0)
    routed = lambda i: (0, jnp.maximum(i - 1, 0))
    const = lambda i: (0, 0)
    return pl.pallas_call(
        body,
        grid=(n + 1,),
        in_specs=[
            pl.BlockSpec((tm, k), row),
            pl.BlockSpec((None, k, d), lambda i: (0, 0, 0), pipeline_mode=pl.Buffered(1)),
            pl.BlockSpec((tm, d), row),
            pl.BlockSpec((1, d), const),
            pl.BlockSpec((ROUTER_ROWS, d), const),
            pl.BlockSpec((ROUTER_ROWS, d), const),
            pl.BlockSpec((ROUTER_ROWS, tm), const),
            pl.BlockSpec((tm, tm + LANES), const),
        ],
        out_specs=[
            pl.BlockSpec((tm, d), row),
            pl.BlockSpec((SUBLANES, tm), routed),
            pl.BlockSpec((SUBLANES, tm), routed),
            pl.BlockSpec((N_EXPERTS, LANES), const),
        ],
        out_shape=[
            jax.ShapeDtypeStruct((t, d), F32),
            jax.ShapeDtypeStruct((SUBLANES, t), jnp.int32),
            jax.ShapeDtypeStruct((SUBLANES, t), F32),
            jax.ShapeDtypeStruct((N_EXPERTS, LANES), F32),
        ],
        scratch_shapes=[pltpu.VMEM((N_EXPERTS, tm), F32), pltpu.VMEM((tm, d), F32), pltpu.VMEM((k, d), BF16)],
        compiler_params=_params("arbitrary"),
        name="proj_route",
    )(o, w_out, x, g.reshape(1, d), wrh, wrl, br, tri)


def _dispatch_body(pos_ref, x_ref, g_ref, xs_hbm, hs, sem, *, tm, n_pairs, spare):
    i = pl.program_id(0)
    n = pl.num_programs(0)
    slot = lax.rem(i, 2)
    d = x_ref.shape[1]
    pitch = _slab_pitch(d)

    def row_out(p, r, s):
        return pltpu.make_async_copy(hs.at[s, pl.ds(r * pitch, pitch)], xs_hbm.at[pl.ds(p * pitch, pitch)], sem.at[s])

    def wait_tile(s):
        for k in range(TOP_K):
            for r in range(tm):
                row_out(0, r, s).wait()

    def tail(s):
        return pltpu.make_async_copy(hs.at[s, pl.ds(0, spare * pitch)],
                                     xs_hbm.at[pl.ds(n_pairs * pitch, spare * pitch)], sem.at[s])

    @pl.when(i >= 2)
    def _():
        wait_tile(slot)

    _slab_store(hs.at[slot], _rms(x_ref[...], g_ref[...]), tm)
    for k in range(TOP_K):
        for r in range(tm):
            row_out(pos_ref[0, 0, k * tm + r], r, slot).start(priority=r % DMA_PRIORITIES)

    @pl.when(i == n - 1)
    def _():
        wait_tile(slot)

        @pl.when(i >= 1)
        def _():
            wait_tile(1 - slot)

        hs[slot] = jnp.zeros(hs.shape[1:], hs.dtype)
        tail(slot).start()
        tail(slot).wait()


def dispatch_rows(x1, g, pos, *, spare, tm=TILE_DISPATCH):
    t, d = x1.shape
    nt = t // tm
    pitch = _slab_pitch(d)
    n_pairs = TOP_K * t
    assert spare <= tm
    pos3 = pos.reshape(TOP_K, nt, tm).transpose(1, 0, 2).reshape(nt, 1, TOP_K * tm)
    return pl.pallas_call(
        functools.partial(_dispatch_body, tm=tm, n_pairs=n_pairs, spare=spare),
        grid=(nt,),
        in_specs=[
            pl.BlockSpec((1, 1, TOP_K * tm), lambda i: (i, 0, 0), memory_space=pltpu.SMEM),
            pl.BlockSpec((tm, d), lambda i: (i, 0)),
            pl.BlockSpec((1, d), lambda i: (0, 0)),
        ],
        out_specs=pl.BlockSpec(memory_space=pl.ANY),
        out_shape=jax.ShapeDtypeStruct(((n_pairs + spare) * pitch, LANES), F32),
        scratch_shapes=[pltpu.VMEM((2, tm * pitch, LANES), F32), pltpu.SemaphoreType.DMA((2,))],
        compiler_params=_params("arbitrary"),
        name="dispatch",
    )(pos3, x1, g.reshape(1, d))


def _experts_body(te_ref, ts_ref, nt_ref, run_ref, nxt_ref, xs_hbm, w1_hbm, w3_hbm, w2_hbm, ys_hbm,
                  xbuf, ybuf, wf1, wf3, wf2, w1b, w3b, w2b, isem, osem, wsem, *, tm, d, n_pairs, layer):
    i = pl.program_id(0)
    n_tiles = nt_ref[0]
    slot = lax.rem(i, 2)
    pitch = _slab_pitch(d)

    def tile_in(j, s):
        return pltpu.make_async_copy(xs_hbm.at[pl.ds(ts_ref[j] * pitch, tm * pitch)], xbuf.at[s], isem.at[s])

    def tile_out(j, s):
        return pltpu.make_async_copy(ybuf.at[s], ys_hbm.at[pl.ds(ts_ref[j] * pitch, tm * pitch)], osem.at[s])

    def weights_in(e, s):
        return [pltpu.make_async_copy(w1_hbm.at[layer, e], wf1.at[s], wsem.at[s]),
                pltpu.make_async_copy(w3_hbm.at[layer, e], wf3.at[s], wsem.at[s]),
                pltpu.make_async_copy(w2_hbm.at[layer, e], wf2.at[s], wsem.at[s])]

    @pl.when(i == 0)
    def _():
        for c in weights_in(te_ref[0], 0):
            c.start()
        tile_in(0, 0).start()
        ybuf[1] = jnp.zeros(ybuf.shape[1:], ybuf.dtype)
        zero_tail = pltpu.make_async_copy(ybuf.at[1], ys_hbm.at[pl.ds(n_pairs * pitch, tm * pitch)], osem.at[1])
        zero_tail.start()
        zero_tail.wait()

    @pl.when(i < n_tiles)
    def _():
        tile_in(i, slot).wait()

        @pl.when(i + 1 < n_tiles)
        def _():
            tile_in(i + 1, 1 - slot).start(priority=DMA_PRIORITIES - 1)

        @pl.when((i == 0) | (te_ref[i] != te_ref[jnp.maximum(i - 1, 0)]))
        def _():
            ws = lax.rem(run_ref[i], 2)
            for c in weights_in(te_ref[i], ws):
                c.wait()
            w1b[...] = wf1[ws].astype(BF16)
            w3b[...] = wf3[ws].astype(BF16)
            w2b[...] = wf2[ws].astype(BF16)

            @pl.when(nxt_ref[i] >= 0)
            def _():
                for c in weights_in(nxt_ref[i], 1 - ws):
                    c.start()

        xb = _slab_load(xbuf, (slot,), tm, d).astype(BF16)
        a = jnp.dot(xb, w1b[...], preferred_element_type=F32)
        b = jnp.dot(xb, w3b[...], preferred_element_type=F32)
        hid = (a * (1.0 / (1.0 + jnp.exp(-a))) * b).astype(BF16)
        _slab_store(ybuf.at[slot], jnp.dot(hid, w2b[...], preferred_element_type=F32), tm)

        @pl.when(i >= 1)
        def _():
            tile_out(i - 1, 1 - slot).wait()

        tile_out(i, slot).start()

        @pl.when(i == n_tiles - 1)
        def _():
            tile_out(i, slot).wait()


def _route_tables(ri, cnt, *, tm, n_tok):
    nt_max = TOP_K * n_tok // tm + N_EXPERTS - 1
    counts = cnt[:, 0].astype(jnp.int32)
    ends = jnp.cumsum(counts)
    offs = ends - counts
    eid = jnp.arange(N_EXPERTS, dtype=jnp.int32)
    sel = ri[0:TOP_K, :, None] == eid
    pos = jnp.sum(jnp.where(sel, offs, 0), axis=-1) + ri[TOP_K:2 * TOP_K]
    tiles = (counts + tm - 1) // tm
    tile_ends = jnp.cumsum(tiles)
    n_tiles = tile_ends[-1].astype(jnp.int32)
    step = jnp.minimum(jnp.arange(nt_max, dtype=jnp.int32), n_tiles - 1)
    te = jnp.clip(jnp.sum((step[:, None] >= tile_ends[None, :]).astype(jnp.int32), axis=1), 0, N_EXPERTS - 1)
    first = jnp.sum(jnp.where(te[:, None] == eid, tile_ends - tiles, 0), axis=-1)
    ts = jnp.sum(jnp.where(te[:, None] == eid, offs, 0), axis=-1) + (step - first) * tm
    run = jnp.sum(((tiles > 0) & (eid < te[:, None])).astype(jnp.int32), axis=-1)
    after = jnp.sum(jnp.where(te[:, None] == eid, tile_ends, 0), axis=-1)
    te_after = jnp.clip(jnp.sum((after[:, None] >= tile_ends[None, :]).astype(jnp.int32), axis=1), 0, N_EXPERTS - 1)
    nxt = jnp.where(after < n_tiles, te_after, -1)
    tables = (te, ts.astype(jnp.int32), n_tiles.reshape(1), run.astype(jnp.int32), nxt.astype(jnp.int32))
    return pos, tables, nt_max


def routed_experts(xs, tables, w1, w3, w2, layer, *, tm, nt_max, n_tok):
    d, f = w1.shape[-2:]
    pitch = _slab_pitch(d)
    body = functools.partial(_experts_body, tm=tm, d=d, n_pairs=TOP_K * n_tok, layer=layer)
    any_space = pl.BlockSpec(memory_space=pl.ANY)
    grid_spec = pltpu.PrefetchScalarGridSpec(
        num_scalar_prefetch=len(tables),
        grid=(nt_max,),
        in_specs=[any_space, any_space, any_space, any_space],
        out_specs=any_space,
        scratch_shapes=[
            pltpu.VMEM((2, tm * pitch, LANES), F32),
            pltpu.VMEM((2, tm * pitch, LANES), F32),
            pltpu.VMEM((2, d, f), F32),
            pltpu.VMEM((2, d, f), F32),
            pltpu.VMEM((2, f, d), F32),
            pltpu.VMEM((d, f), BF16),
            pltpu.VMEM((d, f), BF16),
            pltpu.VMEM((f, d), BF16),
            pltpu.SemaphoreType.DMA((2,)),
            pltpu.SemaphoreType.DMA((2,)),
            pltpu.SemaphoreType.DMA((2,)),
        ],
    )
    return pl.pallas_call(
        body,
        grid_spec=grid_spec,
        out_shape=jax.ShapeDtypeStruct(xs.shape, F32),
        compiler_params=_params("arbitrary"),
        name="experts",
    )(*tables, xs, w1, w3, w2)


def _combine_body(pos_ref, nx1_ref, nx2_ref, x_ref, w_ref, g_ref, ys_hbm, *rest, tm, final, gate_groups):
    x2_ref = gate_in = gate_out = None
    if final:
        h_ref, ybuf, sem = rest
    elif gate_groups is None:
        x2_ref, h_ref, ybuf, sem = rest
    else:
        gate_in, (x2_ref, h_ref), gate_out, (ybuf, sem) = rest[0:3], rest[3:5], rest[5:7], rest[7:9]
    i = pl.program_id(0)
    n = pl.num_programs(0)
    slot = lax.rem(i, COMBINE_SLOTS)
    d = x_ref.shape[1]
    rows, pitch = _slab_rows(d), _slab_pitch(d)

    def row_in(src, k, r, s):
        return pltpu.make_async_copy(ys_hbm.at[pl.ds(src * pitch, rows)], ybuf.at[s, k, pl.ds(r * pitch, rows)],
                                     sem.at[s])

    def gather_start(idx_ref, s):
        for k in range(TOP_K):
            for r in range(tm):
                row_in(idx_ref[0, 0, k * tm + r], k, r, s).start(priority=r % DMA_PRIORITIES)

    def gather_wait(s):
        for k in range(TOP_K):
            for r in range(tm):
                row_in(0, k, r, s).wait()

    @pl.when(i == 0)
    def _():
        gather_start(pos_ref, 0)

        @pl.when(n > 1)
        def _():
            gather_start(nx1_ref, 1)

    gather_wait(slot)

    @pl.when(i + 2 < n)
    def _():
        gather_start(nx2_ref, lax.rem(i + 2, COMBINE_SLOTS))

    w = w_ref[...]
    x2 = x_ref[...] + w[:, 0:1] * _slab_load(ybuf, (slot, 0), tm, d) + w[:, 1:2] * _slab_load(ybuf, (slot, 1), tm, d)
    if x2_ref is not None:
        x2_ref[...] = x2
    h = _rms(x2, g_ref[...]).astype(h_ref.dtype)
    h_ref[...] = h
    if gate_groups is not None:
        _gates(h, *gate_in, *gate_out, hg=gate_groups[0], ng=gate_groups[1])


def moe_combine(x1, ys, pos, rw, g, *, final, gate_params=None, tm=TILE_COMBINE):
    t, d = x1.shape
    nt = t // tm
    w = rw[0:TOP_K].T
    pos3 = pos.reshape(TOP_K, nt, tm).transpose(1, 0, 2).reshape(nt, 1, TOP_K * tm)
    row = lambda i: (i, 0)
    const = lambda i: (0, 0)
    smem_tile = lambda ahead: pl.BlockSpec((1, 1, TOP_K * tm), lambda i: (jnp.minimum(i + ahead, nt - 1), 0, 0),
                                           memory_space=pltpu.SMEM)
    in_specs = [
        smem_tile(0), smem_tile(1), smem_tile(2),
        pl.BlockSpec((tm, d), row),
        pl.BlockSpec((tm, TOP_K), row),
        pl.BlockSpec((1, d), const),
        pl.BlockSpec(memory_space=pl.ANY),
    ]
    scratch = [pltpu.VMEM((COMBINE_SLOTS, TOP_K, tm * _slab_pitch(d), LANES), F32),
               pltpu.SemaphoreType.DMA((COMBINE_SLOTS,))]
    args = [pos3, pos3, pos3, x1, w, g.reshape(1, d), ys]
    if final:
        return pl.pallas_call(
            functools.partial(_combine_body, tm=tm, final=True, gate_groups=None), grid=(nt,), in_specs=in_specs,
            out_specs=pl.BlockSpec((tm, d), row),
            out_shape=jax.ShapeDtypeStruct((t, d), F32),
            scratch_shapes=scratch,
            compiler_params=_params("arbitrary"), name="combine_final",
        )(*args)
    out_specs = [pl.BlockSpec((tm, d), row), pl.BlockSpec((tm, d), row)]
    out_shape = [jax.ShapeDtypeStruct((t, d), F32), jax.ShapeDtypeStruct((t, d), BF16)]
    gate_groups = None
    if gate_params is not None:
        w_hi, w_lo, bias, hg, ng = gate_params
        gate_groups = (hg, ng)
        gw = ng * LANES
        in_specs += [pl.BlockSpec((gw, d), const), pl.BlockSpec((gw, d), const), pl.BlockSpec((1, gw), const)]
        args += [w_hi, w_lo, bias]
        out_specs += [pl.BlockSpec((tm, gw), row), pl.BlockSpec((ng * 2 * hg, tm), lambda i: (0, i))]
        out_shape += [jax.ShapeDtypeStruct((t, gw), F32), jax.ShapeDtypeStruct((ng * 2 * hg, t), F32)]
    return pl.pallas_call(
        functools.partial(_combine_body, tm=tm, final=False, gate_groups=gate_groups), grid=(nt,), in_specs=in_specs,
        out_specs=out_specs, out_shape=out_shape,
        scratch_shapes=scratch,
        compiler_params=_params("arbitrary"), name="combine",
    )(*args)


def _log_sigmoid(x):
    return jnp.minimum(x, 0.0) - jnp.log(1.0 + jnp.exp(-jnp.abs(x)))


def _gates(h, wch_ref, wcl_ref, bc_ref, gc_ref, gr_ref, *, hg, ng):
    col = (lax.dot_general(h, wch_ref[...], NT_DIMS, preferred_element_type=F32)
           + lax.dot_general(h, wcl_ref[...], NT_DIMS, preferred_element_type=F32)) + bc_ref[...]
    lane = lax.broadcasted_iota(jnp.int32, col.shape, 1) % LANES
    gates = jnp.where(lane < hg, col, _log_sigmoid(col))
    gc_ref[...] = gates
    gt = gates.T
    gr_ref[...] = jnp.concatenate([gt[g * LANES:g * LANES + 2 * hg] for g in range(ng)], axis=0)


def _gate_weights(w_if_t, b_i, b_f, hg):
    nh = b_i.shape[0]
    ng = nh // hg

    def grouped(x_i, x_f):
        pad = jnp.zeros((ng, LANES - 2 * hg) + x_i.shape[1:], F32)
        blocks = jnp.concatenate([x_i.reshape((ng, hg) + x_i.shape[1:]), x_f.reshape((ng, hg) + x_f.shape[1:]), pad],
                                 axis=1)
        return blocks.reshape((ng * LANES,) + x_i.shape[1:])

    w_hi, w_lo = _split2(grouped(w_if_t[:nh], w_if_t[nh:]))
    return w_hi, w_lo, grouped(b_i, b_f).reshape(1, ng * LANES), hg, ng


def _mlstm_body(q_ref, k_ref, v_ref, o_ref, gc_ref, gr_ref, gh_ref, tril_ref, triu_ref, out_ref,
                c_ref, n_ref, m_ref, *, scale, nchunk, hg, dqk, dv):
    L = ML_CHUNK
    row = lax.broadcasted_iota(jnp.int32, (L, L), 0)
    col = lax.broadcasted_iota(jnp.int32, (L, L), 1)
    heads = range(hg)
    stack = lambda parts: jnp.concatenate(parts, axis=0)
    rows_of = lambda x, h: x[h * L:(h + 1) * L]
    per_row = lambda scalars: stack([jnp.broadcast_to(s, (L, 1)) for s in scalars])
    tmask = stack([col <= row] * hg)

    @pl.when(pl.program_id(2) == 0)
    def _():
        c_ref[...] = jnp.zeros_like(c_ref)
        n_ref[...] = jnp.zeros_like(n_ref)
        m_ref[...] = jnp.zeros_like(m_ref)

    def cumsum_exact(parts, tri, left):
        acc = None
        for p in parts:
            d = (jnp.dot(tri, p, preferred_element_type=F32) if left
                 else jnp.dot(p, tri, preferred_element_type=F32))
            acc = d if acc is None else acc + d
        return acc

    def step(c, carry):
        c0 = pl.multiple_of(c * L, L)
        qc = [(q_ref[pl.ds(c0, L), h * dqk:(h + 1) * dqk].astype(F32) * scale).astype(BF16) for h in heads]
        kc = [k_ref[pl.ds(c0, L), h * dqk:(h + 1) * dqk] for h in heads]
        vc = [v_ref[pl.ds(c0, L), h * dv:(h + 1) * dv] for h in heads]
        gc = gc_ref[pl.ds(c0, L), :]
        gr = gr_ref[:, pl.ds(c0, L)]
        i_col = stack([gc[:, h:h + 1] for h in heads])
        f_col = stack([gc[:, hg + h:hg + h + 1] for h in heads])
        m_col = m_ref[...]
        m_prev = [m_col[h * L:h * L + 1] for h in heads]

        f_parts = _split3(jnp.broadcast_to(f_col, (hg * L, L)))
        b_col = stack([cumsum_exact([rows_of(p, h) for p in f_parts], tril_ref[...], True) for h in heads])
        u_rows = []
        for h in heads:
            f_row = gr[hg + h:hg + h + 1, :]
            b_row = cumsum_exact(_split3(jnp.broadcast_to(f_row, (SUBLANES, L))), triu_ref[...], False)[0:1, :]
            u_rows.append(jnp.broadcast_to(gr[h:h + 1, :] - b_row, (L, L)))
        bt = b_col[:, 0:1]
        b_last = [b_col[h * L + L - 1:(h + 1) * L, 0:1] for h in heads]

        d_mat = jnp.where(tmask, b_col + stack(u_rows), -1e30)
        m_t = jnp.maximum(bt + m_col, jnp.max(d_mat, axis=1, keepdims=True))
        e = jnp.where(tmask, jnp.exp(d_mat - m_t), 0.0)
        w = stack([lax.dot_general(qc[h], kc[h], NT_DIMS, preferred_element_type=F32) for h in heads]) * e
        wb = w.astype(BF16)
        inter = jnp.exp(bt + m_col - m_t)
        c_prev = [c_ref[h] for h in heads]
        n_prev = [n_ref[h] for h in heads]
        num = inter * stack([jnp.dot(qc[h], c_prev[h].astype(BF16), preferred_element_type=F32) for h in heads]) \
            + stack([jnp.dot(rows_of(wb, h), vc[h], preferred_element_type=F32) for h in heads])
        qn = jnp.sum(stack([qc[h].astype(F32) * n_prev[h] for h in heads]), axis=1, keepdims=True)
        den = inter * qn + jnp.sum(w, axis=1, keepdims=True)
        hh = num / jnp.maximum(jnp.abs(den), jnp.exp(-m_t))

        gains = stack([jnp.broadcast_to(gh_ref[:, h * dv:(h + 1) * dv], (L, dv)) for h in heads])
        hs = hh * lax.rsqrt(jnp.mean(hh * hh, axis=1, keepdims=True) + EPS) * gains
        og = stack([o_ref[pl.ds(c0, L), h * dv:(h + 1) * dv] for h in heads]).astype(F32)
        gated = (hs / (1.0 + jnp.exp(-og))).astype(out_ref.dtype)
        for h in heads:
            out_ref[pl.ds(c0, L), h * dv:(h + 1) * dv] = rows_of(gated, h)

        key_log = per_row(b_last) - bt + i_col
        m_new = [jnp.maximum(b_last[h] + m_prev[h], jnp.max(rows_of(key_log, h), axis=0, keepdims=True))
                 for h in heads]
        m_new_col = per_row(m_new)
        kd = stack([kc[h].astype(F32) for h in heads]) * jnp.exp(key_log - m_new_col)
        for h in heads:
            keep = jnp.exp(b_last[h] + m_prev[h] - m_new[h])
            kd_h = rows_of(kd, h)
            c_ref[h] = keep * c_prev[h] + jnp.dot(kd_h.T.astype(BF16), vc[h], preferred_element_type=F32)
            n_ref[h] = keep * n_prev[h] + jnp.sum(kd_h, axis=0, keepdims=True)
        m_ref[...] = m_new_col
        return carry

    lax.fori_loop(0, nchunk, step, 0)


def mlstm(qkvo, gcol, grow, g_head, *, batch, seq, heads, dqk, dv, hg):
    t = batch * seq
    nh = heads
    L = ML_CHUNK
    nchunk = seq // L
    idx = jnp.arange(L)
    tril = (idx[None, :] <= idx[:, None]).astype(BF16)
    triu = (idx[:, None] <= idx[None, :]).astype(BF16)
    assert nh % hg == 0 and seq % (ML_SEGMENTS * L) == 0
    ng = nh // hg
    ns = ML_SEGMENTS
    sl = seq // ns
    body = functools.partial(_mlstm_body, scale=dqk ** -0.5, nchunk=sl // L, hg=hg, dqk=dqk, dv=dv)
    v_blk0 = 2 * nh * dqk // (hg * dv)
    o_blk0 = v_blk0 + ng
    const = lambda b, g, s: (0, 0)
    return pl.pallas_call(
        body,
        grid=(batch, ng, ns),
        in_specs=[
            pl.BlockSpec((sl, hg * dqk), lambda b, g, s: (b * ns + s, g)),
            pl.BlockSpec((sl, hg * dqk), lambda b, g, s: (b * ns + s, ng + g)),
            pl.BlockSpec((sl, hg * dv), lambda b, g, s: (b * ns + s, v_blk0 + g)),
            pl.BlockSpec((sl, hg * dv), lambda b, g, s: (b * ns + s, o_blk0 + g)),
            pl.BlockSpec((sl, LANES), lambda b, g, s: (b * ns + s, g)),
            pl.BlockSpec((2 * hg, sl), lambda b, g, s: (g, b * ns + s)),
            pl.BlockSpec((1, hg * dv), lambda b, g, s: (0, g)),
            pl.BlockSpec((L, L), const),
            pl.BlockSpec((L, L), const),
        ],
        out_specs=pl.BlockSpec((sl, hg * dv), lambda b, g, s: (b * ns + s, g)),
        out_shape=jax.ShapeDtypeStruct((t, nh * dv), BF16),
        scratch_shapes=[pltpu.VMEM((hg, dqk, dv), F32), pltpu.VMEM((hg, 1, dqk), F32), pltpu.VMEM((hg * L, 1), F32)],
        compiler_params=_params("parallel", "parallel", "arbitrary"),
        name="mlstm",
    )(qkvo, qkvo, qkvo, qkvo, gcol, grow, g_head.reshape(1, nh * dv), tril, triu)


def _moe(o, w_out, x, norm_ffn, w_group, b_group, w_expert, b_expert, w1, w3, w2, layer, g_next, *, final,
         gate_params=None, tm=TILE_EXPERT):
    n_tok = x.shape[0]
    x1, ri, rw, cnt = proj_route(o, w_out, x, norm_ffn, w_group, b_group, w_expert, b_expert)
    pos, tables, nt_max = _route_tables(ri, cnt, tm=tm, n_tok=n_tok)
    xs = dispatch_rows(x1, norm_ffn, pos, spare=tm)
    ys = routed_experts(xs, tables, w1, w3, w2, layer, tm=tm, nt_max=nt_max, n_tok=n_tok)
    return moe_combine(x1, ys, pos, rw, g_next, final=final, gate_params=gate_params)


def kernel(x, norm_mix, norm_ffn, norm_final, sb_w_in, sb_w_out, ml_w_in, ml_b_i, ml_b_f, ml_g_head, ml_w_out,
           moe_w_group, moe_b_group, moe_w_expert, moe_b_expert, moe_w1, moe_w3, moe_w2):
    batch, seq, d = x.shape
    xt = x.reshape(batch * seq, d)

    h = rmsnorm_bf16(xt, norm_mix[0])
    qkv = matmul_bf16(h, sb_w_in, 3 * d)
    o = sb_attention(qkv, batch=batch, seq=seq, heads=SB_HEADS, dh=d // SB_HEADS)
    dqk = (d // 2) // ML_HEADS
    dv = d // ML_HEADS
    n_main = 2 * ML_HEADS * dqk + 2 * ML_HEADS * dv
    w_in_t = jnp.swapaxes(ml_w_in, 1, 2)
    gate_params = _gate_weights(w_in_t[0, n_main:], ml_b_i[0], ml_b_f[0], ML_HEAD_GROUP)
    xt, h, gcol, grow = _moe(o, sb_w_out, xt, norm_ffn[0], moe_w_group[0], moe_b_group[0], moe_w_expert[0],
                             moe_b_expert[0], moe_w1, moe_w3, moe_w2, 0, norm_mix[1], final=False,
                             gate_params=gate_params)

    qkvo = matmul_bf16(h, w_in_t, n_main, w_rows_are_outputs=True)
    o = mlstm(qkvo, gcol, grow, ml_g_head[0], batch=batch, seq=seq, heads=ML_HEADS, dqk=dqk, dv=dv,
              hg=ML_HEAD_GROUP)
    out = _moe(o, ml_w_out, xt, norm_ffn[1], moe_w_group[1], moe_b_group[1], moe_w_expert[1],
               moe_b_expert[1], moe_w1, moe_w3, moe_w2, 1, norm_final, final=True)
    return out.reshape(batch, seq, d)
```

```python
import functools

import jax
import jax.numpy as jnp
from jax import lax
from jax.experimental import pallas as pl
from jax.experimental.pallas import tpu as pltpu

F32 = jnp.float32
BF16 = jnp.bfloat16
EPS = 1e-6

V7X_VMEM_BYTES = 64 * 1024 * 1024
VMEM_LIMIT_BYTES = V7X_VMEM_BYTES - 8 * 1024 * 1024
LANES = 128
SUBLANES = 8

SB_HEADS = 16
ML_HEADS = 8
N_GROUPS = 4
EXPERTS_PER_GROUP = 8
N_EXPERTS = N_GROUPS * EXPERTS_PER_GROUP
TOP_K = 2
ML_CHUNK = 128
ML_HEAD_GROUP = 8
ML_SEGMENTS = 2
SB_BLOCK = 128
ROUTER_ROWS = 128
LOG2_E = 1.4426950408889634
EXP2_F32_UNDERFLOW = -150.5
DMA_PRIORITIES = 2
SLAB_SPARE_ROWS = 1
COMBINE_SLOTS = 3

TILE_NORM = 512
TILE_PROJ_IN = (2048, 1024)
TILE_PROJ_ROUTE = 512
TILE_DISPATCH = 512
TILE_EXPERT = 256
TILE_COMBINE = 256
SB_HEAD_GROUP = 8
SB_KEY_BLOCKS = 2

NT_DIMS = (((1,), (1,)), ((), ()))


def _params(*sem):
    return pltpu.CompilerParams(dimension_semantics=sem, vmem_limit_bytes=VMEM_LIMIT_BYTES)


def _split2(x):
    hi = x.astype(BF16)
    lo = (x - hi.astype(F32)).astype(BF16)
    return hi, lo


def _split3(x):
    hi = x.astype(BF16)
    r = x - hi.astype(F32)
    mid = r.astype(BF16)
    lo = (r - mid.astype(F32)).astype(BF16)
    return hi, mid, lo


def _rms(x, g):
    return x * lax.rsqrt(jnp.mean(x * x, axis=-1, keepdims=True) + EPS) * g


def _slab_rows(d):
    return d // LANES


def _slab_pitch(d):
    return _slab_rows(d) + SLAB_SPARE_ROWS


def _slab_store(ref, x, n):
    d = x.shape[1]
    pitch = _slab_pitch(d)
    for c in range(_slab_rows(d)):
        ref[pl.ds(c, n, stride=pitch), :] = x[:, c * LANES:(c + 1) * LANES]
    for c in range(_slab_rows(d), pitch):
        ref[pl.ds(c, n, stride=pitch), :] = jnp.zeros((n, LANES), x.dtype)


def _slab_load(ref, lead, n, d):
    pitch = _slab_pitch(d)
    return jnp.concatenate([ref[lead + (pl.ds(c, n, stride=pitch), slice(None))] for c in range(_slab_rows(d))],
                           axis=1)


def _rmsnorm_body(x_ref, g_ref, h_ref):
    h_ref[...] = _rms(x_ref[...], g_ref[...]).astype(h_ref.dtype)


def rmsnorm_bf16(x, g, *, tm=TILE_NORM):
    t, d = x.shape
    return pl.pallas_call(
        _rmsnorm_body,
        grid=(t // tm,),
        in_specs=[pl.BlockSpec((tm, d), lambda i: (i, 0)), pl.BlockSpec((1, d), lambda i: (0, 0))],
        out_specs=pl.BlockSpec((tm, d), lambda i: (i, 0)),
        out_shape=jax.ShapeDtypeStruct((t, d), BF16),
        compiler_params=_params("parallel"),
        name="rmsnorm",
    )(x, g.reshape(1, d))


def _mm_body(h_ref, w_ref, o_ref, wb_ref, *, w_rows_are_outputs):
    @pl.when(pl.program_id(1) == 0)
    def _():
        wb_ref[...] = w_ref[...].astype(BF16)

    if w_rows_are_outputs:
        acc = lax.dot_general(h_ref[...], wb_ref[...], NT_DIMS, preferred_element_type=F32)
    else:
        acc = jnp.dot(h_ref[...], wb_ref[...], preferred_element_type=F32)
    o_ref[...] = acc.astype(o_ref.dtype)


def matmul_bf16(h, w, n, *, w_rows_are_outputs=False, tm=TILE_PROJ_IN[0], tn=TILE_PROJ_IN[1]):
    t, k = h.shape
    if w_rows_are_outputs:
        w_spec, w_block = pl.BlockSpec((None, tn, k), lambda j, i: (0, j, 0)), (tn, k)
    else:
        w_spec, w_block = pl.BlockSpec((None, k, tn), lambda j, i: (0, 0, j)), (k, tn)
    return pl.pallas_call(
        functools.partial(_mm_body, w_rows_are_outputs=w_rows_are_outputs),
        grid=(n // tn, t // tm),
        in_specs=[pl.BlockSpec((tm, k), lambda j, i: (i, 0)), w_spec],
        out_specs=pl.BlockSpec((tm, tn), lambda j, i: (i, j)),
        out_shape=jax.ShapeDtypeStruct((t, n), BF16),
        scratch_shapes=[pltpu.VMEM(w_block, BF16)],
        compiler_params=_params("parallel", "arbitrary"),
        name="proj_in",
    )(h, w)


def _sb_attn_body(q_ref, k_ref, v_ref, cm_ref, o_ref, *, scale, nblk, hg, kc):
    blk = SB_BLOCK
    dh = SB_BLOCK
    wid = kc * blk
    rel = lax.broadcasted_iota(jnp.int32, (blk, wid), 1) - lax.broadcasted_iota(jnp.int32, (blk, wid), 0)

    def chunk(qbs, q0, c, rs, accs, masked):
        k0 = pl.multiple_of(c * wid, wid)
        if masked:
            valid = rel < (q0 - k0)
        z = jnp.concatenate(
            [lax.dot_general(qbs[h], k_ref[pl.ds(k0, wid), h * dh:(h + 1) * dh], NT_DIMS,
                             preferred_element_type=F32) for h in range(hg)], axis=1)
        sp = jnp.maximum(z, 0.0) + jnp.log(1.0 + jnp.exp2(-jnp.abs(z))) * LOG2_E
        log_beta = z - sp
        if masked:
            valid_all = jnp.concatenate([valid] * hg, axis=1)
            sp = jnp.where(valid_all, sp, 0.0)
        nb = hg * kc
        x = jnp.concatenate([sp[:, j * blk:(j + 1) * blk] for j in range(nb)], axis=0).astype(BF16)
        cs = jnp.dot(x, cm_ref[...], preferred_element_type=F32)
        new_rs = []
        suffix = [None] * nb
        for h in range(hg):
            r = rs[h]
            for j in reversed(range(kc)):
                csj = cs[(h * kc + j) * blk:(h * kc + j + 1) * blk]
                suffix[h * kc + j] = csj[:, :blk] + r
                r = r + csj[:, blk:]
            new_rs.append(r)
        a = jnp.exp2(log_beta + jnp.concatenate(suffix, axis=1))
        if masked:
            a = jnp.where(valid_all, a, 0.0)
        a = a.astype(BF16)
        new_accs = [accs[h] + jnp.dot(a[:, h * wid:(h + 1) * wid], v_ref[pl.ds(k0, wid), h * dh:(h + 1) * dh],
                                      preferred_element_type=F32) for h in range(hg)]
        return tuple(new_rs), tuple(new_accs)

    def q_step(qi, carry):
        q0 = pl.multiple_of(qi * blk, blk)
        qbs = [(q_ref[pl.ds(q0, blk), h * dh:(h + 1) * dh].astype(F32) * scale).astype(BF16) for h in range(hg)]
        zeros = tuple(jnp.zeros((blk, blk), F32) for _ in range(hg))
        top = qi // kc
        rs, accs = chunk(qbs, q0, top, zeros, zeros, True)

        def live(rs):
            return jnp.max(functools.reduce(jnp.maximum, rs)) > EXP2_F32_UNDERFLOW

        def k_cond(c):
            return (c[0] < top) & c[1]

        def k_step(c):
            rs, accs = chunk(qbs, q0, top - 1 - c[0], c[2], c[3], False)
            return c[0] + 1, live(rs), rs, accs

        accs = lax.while_loop(k_cond, k_step, (jnp.int32(0), live(rs), rs, accs))[3]
        for h in range(hg):
            o_ref[pl.ds(q0, blk), h * dh:(h + 1) * dh] = accs[h].astype(o_ref.dtype)
        return carry

    lax.fori_loop(0, nblk, q_step, 0)


def _cumsum_matrix(blk):
    j = jnp.arange(blk)[:, None]
    c = jnp.arange(2 * blk)[None, :]
    return jnp.where((c >= blk) | (j > c), -1.0, 0.0).astype(BF16)


def sb_attention(qkv, *, batch, seq, heads, dh, hg=SB_HEAD_GROUP, kc=SB_KEY_BLOCKS):
    assert dh == SB_BLOCK
    nblk = seq // SB_BLOCK
    assert nblk % kc == 0 and heads % hg == 0
    body = functools.partial(_sb_attn_body, scale=dh ** -0.5 * LOG2_E, nblk=nblk, hg=hg, kc=kc)
    cm = _cumsum_matrix(SB_BLOCK)
    ng = heads // hg
    return pl.pallas_call(
        body,
        grid=(batch, ng),
        in_specs=[
            pl.BlockSpec((seq, hg * dh), lambda b, h: (b, h)),
            pl.BlockSpec((seq, hg * dh), lambda b, h: (b, ng + h)),
            pl.BlockSpec((seq, hg * dh), lambda b, h: (b, 2 * ng + h)),
            pl.BlockSpec((SB_BLOCK, 2 * SB_BLOCK), lambda b, h: (0, 0)),
        ],
        out_specs=pl.BlockSpec((seq, hg * dh), lambda b, h: (b, h)),
        out_shape=jax.ShapeDtypeStruct((batch * seq, heads * dh), BF16),
        compiler_params=_params("parallel", "parallel"),
        name="sb_attn",
    )(qkv, qkv, qkv, cm)


def _proj_route_body(o_ref, w_ref, x_ref, g_ref, wrh_ref, wrl_ref, br_ref, tri_ref,
                     x1_ref, ri_ref, rw_ref, cnt_ref, base_ref, xprev_ref, wb_ref, *, tm):
    i = pl.program_id(0)
    n = pl.num_programs(0) - 1

    @pl.when(i == 0)
    def _():
        base_ref[...] = jnp.zeros_like(base_ref)
        wb_ref[...] = w_ref[...].astype(BF16)

    @pl.when(i > 0)
    def _():
        _route_tile(xprev_ref[...], g_ref, wrh_ref, wrl_ref, br_ref, tri_ref, ri_ref, rw_ref, cnt_ref, base_ref, tm=tm)

    @pl.when(i < n)
    def _():
        x1 = x_ref[...] + jnp.dot(o_ref[...], wb_ref[...], preferred_element_type=F32)
        x1_ref[...] = x1
        xprev_ref[...] = x1


def _route_tile(x1, g_ref, wrh_ref, wrl_ref, br_ref, tri_ref, ri_ref, rw_ref, cnt_ref, base_ref, *, tm):
    h2 = _rms(x1, g_ref[...])

    hh, hl = _split2(h2)
    wrh = wrh_ref[...]
    lg = (lax.dot_general(wrh, hh, NT_DIMS, preferred_element_type=F32)
          + lax.dot_general(wrh, hl, NT_DIMS, preferred_element_type=F32)
          + lax.dot_general(wrl_ref[...], hh, NT_DIMS, preferred_element_type=F32)) + br_ref[...]

    r8 = lax.broadcasted_iota(jnp.int32, (SUBLANES, tm), 0)
    neg_inf = jnp.float32(-jnp.inf)
    gl = jnp.where(r8 < N_GROUPS, lg[N_EXPERTS:N_EXPERTS + SUBLANES, :], neg_inf)
    gmax = jnp.max(gl, axis=0, keepdims=True)
    gsel = jnp.min(jnp.where(gl == gmax, r8, SUBLANES), axis=0, keepdims=True)
    p_sel = 1.0 / jnp.sum(jnp.exp(gl - gmax), axis=0, keepdims=True)

    es = lg[0:EXPERTS_PER_GROUP, :]
    for g in range(1, N_GROUPS):
        es = jnp.where(gsel == g, lg[g * EXPERTS_PER_GROUP:(g + 1) * EXPERTS_PER_GROUP, :], es)
    m1 = jnp.max(es, axis=0, keepdims=True)
    i1 = jnp.min(jnp.where(es == m1, r8, SUBLANES), axis=0, keepdims=True)
    es2 = jnp.where(r8 == i1, neg_inf, es)
    m2 = jnp.max(es2, axis=0, keepdims=True)
    i2 = jnp.min(jnp.where(es2 == m2, r8, SUBLANES), axis=0, keepdims=True)
    t = jnp.exp(m2 - m1)
    w1 = p_sel / (1.0 + t)
    w2 = w1 * t
    e1 = gsel * EXPERTS_PER_GROUP + i1
    e2 = gsel * EXPERTS_PER_GROUP + i2

    r32 = lax.broadcasted_iota(jnp.int32, (N_EXPERTS, tm), 0)
    hit1 = r32 == e1
    hit2 = r32 == e2
    onehot = (jnp.where(hit1, 1.0, 0.0) + jnp.where(hit2, 1.0, 0.0)).astype(BF16)
    cs = jnp.dot(onehot, tri_ref[...], preferred_element_type=F32)
    base = base_ref[...]
    before = cs[:, :tm] + base
    rank1 = jnp.sum(jnp.where(hit1, before, 0.0), axis=0, keepdims=True)
    rank2 = jnp.sum(jnp.where(hit2, before, 0.0), axis=0, keepdims=True)
    tot = cs[:, tm:]
    new_base = base + jnp.concatenate([tot] * (tm // LANES), axis=1)
    base_ref[...] = new_base
    cnt_ref[...] = new_base[:, :LANES]

    zi = jnp.zeros((SUBLANES, tm), jnp.int32)
    ri = jnp.where(r8 == 0, e1, zi)
    ri = jnp.where(r8 == 1, e2, ri)
    ri = jnp.where(r8 == 2, rank1.astype(jnp.int32), ri)
    ri = jnp.where(r8 == 3, rank2.astype(jnp.int32), ri)
    ri_ref[...] = ri
    zf = jnp.zeros((SUBLANES, tm), F32)
    rw = jnp.where(r8 == 0, w1, zf)
    rw = jnp.where(r8 == 1, w2, rw)
    rw_ref[...] = jnp.concatenate([rw, jnp.zeros((LANES - SUBLANES, tm), F32)], axis=0).T


def proj_route(o, w_out, x, g, w_group, b_group, w_expert, b_expert, *, tm=TILE_PROJ_ROUTE):
    t, d = x.shape
    k = o.shape[1]
    wr = jnp.zeros((ROUTER_ROWS, d), F32)
    wr = wr.at[:N_EXPERTS].set(w_expert.T).at[N_EXPERTS:N_EXPERTS + N_GROUPS].set(w_group.T)
    wrh, wrl = _split2(wr)
    br = jnp.zeros((ROUTER_ROWS,), F32).at[:N_EXPERTS].set(b_expert).at[N_EXPERTS:N_EXPERTS + N_GROUPS].set(b_group)
    br = jnp.broadcast_to(br[:, None], (ROUTER_ROWS, tm))
    tp = jnp.arange(tm)[:, None]
    tri = jnp.concatenate([(tp < jnp.arange(tm)[None, :]).astype(BF16), jnp.ones((tm, LANES), BF16)], axis=1)
    body = functools.partial(_proj_route_body, tm=tm)
    n = t // tm
    row = lambda i: (jnp.minimum(i, n - 1), 0)
    routed = lambda i: (0, jnp.maximum(i - 1, 0))
    const = lambda i: (0, 0)
    return pl.pallas_call(
        body,
        grid=(n + 1,),
        in_specs=[
            pl.BlockSpec((tm, k), row),
            pl.BlockSpec((None, k, d), lambda i: (0, 0, 0), pipeline_mode=pl.Buffered(1)),
            pl.BlockSpec((tm, d), row),
            pl.BlockSpec((1, d), const),
            pl.BlockSpec((ROUTER_ROWS, d), const),
            pl.BlockSpec((ROUTER_ROWS, d), const),
            pl.BlockSpec((ROUTER_ROWS, tm), const),
            pl.BlockSpec((tm, tm + LANES), const),
        ],
        out_specs=[
            pl.BlockSpec((tm, d), row),
            pl.BlockSpec((SUBLANES, tm), routed),
            pl.BlockSpec((tm, LANES), lambda i: (jnp.maximum(i - 1, 0), 0)),
            pl.BlockSpec((N_EXPERTS, LANES), const),
        ],
        out_shape=[
            jax.ShapeDtypeStruct((t, d), F32),
            jax.ShapeDtypeStruct((SUBLANES, t), jnp.int32),
            jax.ShapeDtypeStruct((t, LANES), F32),
            jax.ShapeDtypeStruct((N_EXPERTS, LANES), F32),
        ],
        scratch_shapes=[pltpu.VMEM((N_EXPERTS, tm), F32), pltpu.VMEM((tm, d), F32), pltpu.VMEM((k, d), BF16)],
        compiler_params=_params("arbitrary"),
        name="proj_route",
    )(o, w_out, x, g.reshape(1, d), wrh, wrl, br, tri)


def _dispatch_body(pos_ref, x_ref, g_ref, xs_hbm, hs, sem, *, tm, n_pairs, spare):
    i = pl.program_id(0)
    n = pl.num_programs(0)
    slot = lax.rem(i, 2)
    d = x_ref.shape[1]
    pitch = _slab_pitch(d)

    def row_out(p, r, s):
        return pltpu.make_async_copy(hs.at[s, pl.ds(r * pitch, pitch)], xs_hbm.at[pl.ds(p * pitch, pitch)], sem.at[s])

    def wait_tile(s):
        for k in range(TOP_K):
            for r in range(tm):
                row_out(0, r, s).wait()

    def tail(s):
        return pltpu.make_async_copy(hs.at[s, pl.ds(0, spare * pitch)],
                                     xs_hbm.at[pl.ds(n_pairs * pitch, spare * pitch)], sem.at[s])

    @pl.when(i >= 2)
    def _():
        wait_tile(slot)

    _slab_store(hs.at[slot], _rms(x_ref[...], g_ref[...]), tm)
    for k in range(TOP_K):
        for r in range(tm):
            row_out(pos_ref[0, 0, k * tm + r], r, slot).start(priority=r % DMA_PRIORITIES)

    @pl.when(i == n - 1)
    def _():
        wait_tile(slot)

        @pl.when(i >= 1)
        def _():
            wait_tile(1 - slot)

        hs[slot] = jnp.zeros(hs.shape[1:], hs.dtype)
        tail(slot).start()
        tail(slot).wait()


def dispatch_rows(x1, g, pos, *, spare, tm=TILE_DISPATCH):
    t, d = x1.shape
    nt = t // tm
    pitch = _slab_pitch(d)
    n_pairs = TOP_K * t
    assert spare <= tm
    pos3 = pos.reshape(TOP_K, nt, tm).transpose(1, 0, 2).reshape(nt, 1, TOP_K * tm)
    return pl.pallas_call(
        functools.partial(_dispatch_body, tm=tm, n_pairs=n_pairs, spare=spare),
        grid=(nt,),
        in_specs=[
            pl.BlockSpec((1, 1, TOP_K * tm), lambda i: (i, 0, 0), memory_space=pltpu.SMEM),
            pl.BlockSpec((tm, d), lambda i: (i, 0)),
            pl.BlockSpec((1, d), lambda i: (0, 0)),
        ],
        out_specs=pl.BlockSpec(memory_space=pl.ANY),
        out_shape=jax.ShapeDtypeStruct(((n_pairs + spare) * pitch, LANES), F32),
        scratch_shapes=[pltpu.VMEM((2, tm * pitch, LANES), F32), pltpu.SemaphoreType.DMA((2,))],
        compiler_params=_params("arbitrary"),
        name="dispatch",
    )(pos3, x1, g.reshape(1, d))


def _experts_body(te_ref, ts_ref, nt_ref, run_ref, nxt_ref, xs_hbm, w1_hbm, w3_hbm, w2_hbm, ys_hbm,
                  xbuf, ybuf, wf1, wf3, wf2, w1b, w3b, w2b, isem, osem, wsem, *, tm, d, n_pairs, layer):
    i = pl.program_id(0)
    n_tiles = nt_ref[0]
    slot = lax.rem(i, 2)
    pitch = _slab_pitch(d)

    def tile_in(j, s):
        return pltpu.make_async_copy(xs_hbm.at[pl.ds(ts_ref[j] * pitch, tm * pitch)], xbuf.at[s], isem.at[s])

    def tile_out(j, s):
        return pltpu.make_async_copy(ybuf.at[s], ys_hbm.at[pl.ds(ts_ref[j] * pitch, tm * pitch)], osem.at[s])

    def weights_in(e, s):
        return [pltpu.make_async_copy(w1_hbm.at[layer, e], wf1.at[s], wsem.at[s]),
                pltpu.make_async_copy(w3_hbm.at[layer, e], wf3.at[s], wsem.at[s]),
                pltpu.make_async_copy(w2_hbm.at[layer, e], wf2.at[s], wsem.at[s])]

    @pl.when(i == 0)
    def _():
        for c in weights_in(te_ref[0], 0):
            c.start()
        tile_in(0, 0).start()
        ybuf[1] = jnp.zeros(ybuf.shape[1:], ybuf.dtype)
        zero_tail = pltpu.make_async_copy(ybuf.at[1], ys_hbm.at[pl.ds(n_pairs * pitch, tm * pitch)], osem.at[1])
        zero_tail.start()
        zero_tail.wait()

    @pl.when(i < n_tiles)
    def _():
        tile_in(i, slot).wait()

        @pl.when(i + 1 < n_tiles)
        def _():
            tile_in(i + 1, 1 - slot).start(priority=DMA_PRIORITIES - 1)

        @pl.when((i == 0) | (te_ref[i] != te_ref[jnp.maximum(i - 1, 0)]))
        def _():
            ws = lax.rem(run_ref[i], 2)
            for c in weights_in(te_ref[i], ws):
                c.wait()
            w1b[...] = wf1[ws].astype(BF16)
            w3b[...] = wf3[ws].astype(BF16)
            w2b[...] = wf2[ws].astype(BF16)

            @pl.when(nxt_ref[i] >= 0)
            def _():
                for c in weights_in(nxt_ref[i], 1 - ws):
                    c.start()

        xb = _slab_load(xbuf, (slot,), tm, d).astype(BF16)
        a = jnp.dot(xb, w1b[...], preferred_element_type=F32)
        b = jnp.dot(xb, w3b[...], preferred_element_type=F32)
        hid = (a * (1.0 / (1.0 + jnp.exp(-a))) * b).astype(BF16)
        _slab_store(ybuf.at[slot], jnp.dot(hid, w2b[...], preferred_element_type=F32), tm)

        @pl.when(i >= 1)
        def _():
            tile_out(i - 1, 1 - slot).wait()

        tile_out(i, slot).start()

        @pl.when(i == n_tiles - 1)
        def _():
            tile_out(i, slot).wait()


def _route_tables(ri, cnt, *, tm, n_tok):
    nt_max = TOP_K * n_tok // tm + N_EXPERTS - 1
    counts = cnt[:, 0].astype(jnp.int32)
    ends = jnp.cumsum(counts)
    offs = ends - counts
    eid = jnp.arange(N_EXPERTS, dtype=jnp.int32)
    sel = ri[0:TOP_K, :, None] == eid
    pos = jnp.sum(jnp.where(sel, offs, 0), axis=-1) + ri[TOP_K:2 * TOP_K]
    tiles = (counts + tm - 1) // tm
    tile_ends = jnp.cumsum(tiles)
    n_tiles = tile_ends[-1].astype(jnp.int32)
    step = jnp.minimum(jnp.arange(nt_max, dtype=jnp.int32), n_tiles - 1)
    te = jnp.clip(jnp.sum((step[:, None] >= tile_ends[None, :]).astype(jnp.int32), axis=1), 0, N_EXPERTS - 1)
    first = jnp.sum(jnp.where(te[:, None] == eid, tile_ends - tiles, 0), axis=-1)
    ts = jnp.sum(jnp.where(te[:, None] == eid, offs, 0), axis=-1) + (step - first) * tm
    run = jnp.sum(((tiles > 0) & (eid < te[:, None])).astype(jnp.int32), axis=-1)
    after = jnp.sum(jnp.where(te[:, None] == eid, tile_ends, 0), axis=-1)
    te_after = jnp.clip(jnp.sum((after[:, None] >= tile_ends[None, :]).astype(jnp.int32), axis=1), 0, N_EXPERTS - 1)
    nxt = jnp.where(after < n_tiles, te_after, -1)
    tables = (te, ts.astype(jnp.int32), n_tiles.reshape(1), run.astype(jnp.int32), nxt.astype(jnp.int32))
    return pos, tables, nt_max


def routed_experts(xs, tables, w1, w3, w2, layer, *, tm, nt_max, n_tok):
    d, f = w1.shape[-2:]
    pitch = _slab_pitch(d)
    body = functools.partial(_experts_body, tm=tm, d=d, n_pairs=TOP_K * n_tok, layer=layer)
    any_space = pl.BlockSpec(memory_space=pl.ANY)
    grid_spec = pltpu.PrefetchScalarGridSpec(
        num_scalar_prefetch=len(tables),
        grid=(nt_max,),
        in_specs=[any_space, any_space, any_space, any_space],
        out_specs=any_space,
        scratch_shapes=[
            pltpu.VMEM((2, tm * pitch, LANES), F32),
            pltpu.VMEM((2, tm * pitch, LANES), F32),
            pltpu.VMEM((2, d, f), F32),
            pltpu.VMEM((2, d, f), F32),
            pltpu.VMEM((2, f, d), F32),
            pltpu.VMEM((d, f), BF16),
            pltpu.VMEM((d, f), BF16),
            pltpu.VMEM((f, d), BF16),
            pltpu.SemaphoreType.DMA((2,)),
            pltpu.SemaphoreType.DMA((2,)),
            pltpu.SemaphoreType.DMA((2,)),
        ],
    )
    return pl.pallas_call(
        body,
        grid_spec=grid_spec,
        out_shape=jax.ShapeDtypeStruct(xs.shape, F32),
        compiler_params=_params("arbitrary"),
        name="experts",
    )(*tables, xs, w1, w3, w2)


def _combine_body(pos_ref, nx1_ref, nx2_ref, x_ref, w_ref, g_ref, ys_hbm, *rest, tm, final, gate_groups):
    x2_ref = gate_in = gate_out = None
    if final:
        h_ref, ybuf, sem = rest
    elif gate_groups is None:
        x2_ref, h_ref, ybuf, sem = rest
    else:
        gate_in, (x2_ref, h_ref), gate_out, (ybuf, sem) = rest[0:3], rest[3:5], rest[5:7], rest[7:9]
    i = pl.program_id(0)
    n = pl.num_programs(0)
    slot = lax.rem(i, COMBINE_SLOTS)
    d = x_ref.shape[1]
    rows, pitch = _slab_rows(d), _slab_pitch(d)

    def row_in(src, k, r, s):
        return pltpu.make_async_copy(ys_hbm.at[pl.ds(src * pitch, rows)], ybuf.at[s, k, pl.ds(r * pitch, rows)],
                                     sem.at[s])

    def gather_start(idx_ref, s):
        for k in range(TOP_K):
            for r in range(tm):
                row_in(idx_ref[0, 0, k * tm + r], k, r, s).start(priority=r % DMA_PRIORITIES)

    def gather_wait(s):
        for k in range(TOP_K):
            for r in range(tm):
                row_in(0, k, r, s).wait()

    @pl.when(i == 0)
    def _():
        gather_start(pos_ref, 0)

        @pl.when(n > 1)
        def _():
            gather_start(nx1_ref, 1)

    gather_wait(slot)

    @pl.when(i + 2 < n)
    def _():
        gather_start(nx2_ref, lax.rem(i + 2, COMBINE_SLOTS))

    w = w_ref[...]
    x2 = x_ref[...] + w[:, 0:1] * _slab_load(ybuf, (slot, 0), tm, d) + w[:, 1:2] * _slab_load(ybuf, (slot, 1), tm, d)
    if x2_ref is not None:
        x2_ref[...] = x2
    h = _rms(x2, g_ref[...]).astype(h_ref.dtype)
    h_ref[...] = h
    if gate_groups is not None:
        _gates(h, *gate_in, *gate_out, hg=gate_groups[0], ng=gate_groups[1])


def moe_combine(x1, ys, pos, rw, g, *, final, gate_params=None, tm=TILE_COMBINE):
    t, d = x1.shape
    nt = t // tm
    pos3 = pos.reshape(TOP_K, nt, tm).transpose(1, 0, 2).reshape(nt, 1, TOP_K * tm)
    row = lambda i: (i, 0)
    const = lambda i: (0, 0)
    smem_tile = lambda ahead: pl.BlockSpec((1, 1, TOP_K * tm), lambda i: (jnp.minimum(i + ahead, nt - 1), 0, 0),
                                           memory_space=pltpu.SMEM)
    in_specs = [
        smem_tile(0), smem_tile(1), smem_tile(2),
        pl.BlockSpec((tm, d), row),
        pl.BlockSpec((tm, LANES), row),
        pl.BlockSpec((1, d), const),
        pl.BlockSpec(memory_space=pl.ANY),
    ]
    scratch = [pltpu.VMEM((COMBINE_SLOTS, TOP_K, tm * _slab_pitch(d), LANES), F32),
               pltpu.SemaphoreType.DMA((COMBINE_SLOTS,))]
    args = [pos3, pos3, pos3, x1, rw, g.reshape(1, d), ys]
    if final:
        return pl.pallas_call(
            functools.partial(_combine_body, tm=tm, final=True, gate_groups=None), grid=(nt,), in_specs=in_specs,
            out_specs=pl.BlockSpec((tm, d), row),
            out_shape=jax.ShapeDtypeStruct((t, d), F32),
            scratch_shapes=scratch,
            compiler_params=_params("arbitrary"), name="combine_final",
        )(*args)
    out_specs = [pl.BlockSpec((tm, d), row), pl.BlockSpec((tm, d), row)]
    out_shape = [jax.ShapeDtypeStruct((t, d), F32), jax.ShapeDtypeStruct((t, d), BF16)]
    gate_groups = None
    if gate_params is not None:
        w_hi, w_lo, bias, hg, ng = gate_params
        gate_groups = (hg, ng)
        gw = ng * LANES
        in_specs += [pl.BlockSpec((gw, d), const), pl.BlockSpec((gw, d), const), pl.BlockSpec((1, gw), const)]
        args += [w_hi, w_lo, bias]
        out_specs += [pl.BlockSpec((tm, gw), row), pl.BlockSpec((ng * 2 * hg, tm), lambda i: (0, i))]
        out_shape += [jax.ShapeDtypeStruct((t, gw), F32), jax.ShapeDtypeStruct((ng * 2 * hg, t), F32)]
    return pl.pallas_call(
        functools.partial(_combine_body, tm=tm, final=False, gate_groups=gate_groups), grid=(nt,), in_specs=in_specs,
        out_specs=out_specs, out_shape=out_shape,
        scratch_shapes=scratch,
        compiler_params=_params("arbitrary"), name="combine",
    )(*args)


def _log_sigmoid(x):
    return jnp.minimum(x, 0.0) - jnp.log(1.0 + jnp.exp(-jnp.abs(x)))


def _gates(h, wch_ref, wcl_ref, bc_ref, gc_ref, gr_ref, *, hg, ng):
    col = (lax.dot_general(h, wch_ref[...], NT_DIMS, preferred_element_type=F32)
           + lax.dot_general(h, wcl_ref[...], NT_DIMS, preferred_element_type=F32)) + bc_ref[...]
    lane = lax.broadcasted_iota(jnp.int32, col.shape, 1) % LANES
    gates = jnp.where(lane < hg, col, _log_sigmoid(col))
    gc_ref[...] = gates
    gt = gates.T
    gr_ref[...] = jnp.concatenate([gt[g * LANES:g * LANES + 2 * hg] for g in range(ng)], axis=0)


def _gate_weights(w_if_t, b_i, b_f, hg):
    nh = b_i.shape[0]
    ng = nh // hg

    def grouped(x_i, x_f):
        pad = jnp.zeros((ng, LANES - 2 * hg) + x_i.shape[1:], F32)
        blocks = jnp.concatenate([x_i.reshape((ng, hg) + x_i.shape[1:]), x_f.reshape((ng, hg) + x_f.shape[1:]), pad],
                                 axis=1)
        return blocks.reshape((ng * LANES,) + x_i.shape[1:])

    w_hi, w_lo = _split2(grouped(w_if_t[:nh], w_if_t[nh:]))
    return w_hi, w_lo, grouped(b_i, b_f).reshape(1, ng * LANES), hg, ng


def _mlstm_body(q_ref, k_ref, v_ref, o_ref, gc_ref, gr_ref, gh_ref, tril_ref, triu_ref, out_ref,
                c_ref, n_ref, m_ref, *, scale, nchunk, hg, dqk, dv):
    L = ML_CHUNK
    row = lax.broadcasted_iota(jnp.int32, (L, L), 0)
    col = lax.broadcasted_iota(jnp.int32, (L, L), 1)
    heads = range(hg)
    stack = lambda parts: jnp.concatenate(parts, axis=0)
    rows_of = lambda x, h: x[h * L:(h + 1) * L]
    per_row = lambda scalars: stack([jnp.broadcast_to(s, (L, 1)) for s in scalars])
    tmask = stack([col <= row] * hg)

    @pl.when(pl.program_id(2) == 0)
    def _():
        c_ref[...] = jnp.zeros_like(c_ref)
        n_ref[...] = jnp.zeros_like(n_ref)
        m_ref[...] = jnp.zeros_like(m_ref)

    def cumsum_exact(parts, tri, left):
        acc = None
        for p in parts:
            d = (jnp.dot(tri, p, preferred_element_type=F32) if left
                 else jnp.dot(p, tri, preferred_element_type=F32))
            acc = d if acc is None else acc + d
        return acc

    def step(c, carry):
        c0 = pl.multiple_of(c * L, L)
        qc = [(q_ref[pl.ds(c0, L), h * dqk:(h + 1) * dqk].astype(F32) * scale).astype(BF16) for h in heads]
        kc = [k_ref[pl.ds(c0, L), h * dqk:(h + 1) * dqk] for h in heads]
        vc = [v_ref[pl.ds(c0, L), h * dv:(h + 1) * dv] for h in heads]
        gc = gc_ref[pl.ds(c0, L), :]
        gr = gr_ref[:, pl.ds(c0, L)]
        i_col = stack([gc[:, h:h + 1] for h in heads])
        f_col = stack([gc[:, hg + h:hg + h + 1] for h in heads])
        m_col = m_ref[...]
        m_prev = [m_col[h * L:h * L + 1] for h in heads]

        f_parts = _split3(jnp.broadcast_to(f_col, (hg * L, L)))
        b_col = stack([cumsum_exact([rows_of(p, h) for p in f_parts], tril_ref[...], True) for h in heads])
        u_rows = []
        for h in heads:
            f_row = gr[hg + h:hg + h + 1, :]
            b_row = cumsum_exact(_split3(jnp.broadcast_to(f_row, (SUBLANES, L))), triu_ref[...], False)[0:1, :]
            u_rows.append(jnp.broadcast_to(gr[h:h + 1, :] - b_row, (L, L)))
        bt = b_col[:, 0:1]
        b_last = [b_col[h * L + L - 1:(h + 1) * L, 0:1] for h in heads]

        d_mat = jnp.where(tmask, b_col + stack(u_rows), -1e30)
        m_t = jnp.maximum(bt + m_col, jnp.max(d_mat, axis=1, keepdims=True))
        e = jnp.where(tmask, jnp.exp(d_mat - m_t), 0.0)
        w = stack([lax.dot_general(qc[h], kc[h], NT_DIMS, preferred_element_type=F32) for h in heads]) * e
        wb = w.astype(BF16)
        inter = jnp.exp(bt + m_col - m_t)
        c_prev = [c_ref[h] for h in heads]
        n_prev = [n_ref[h] for h in heads]
        num = inter * stack([jnp.dot(qc[h], c_prev[h].astype(BF16), preferred_element_type=F32) for h in heads]) \
            + stack([jnp.dot(rows_of(wb, h), vc[h], preferred_element_type=F32) for h in heads])
        qn = jnp.sum(stack([qc[h].astype(F32) * n_prev[h] for h in heads]), axis=1, keepdims=True)
        den = inter * qn + jnp.sum(w, axis=1, keepdims=True)
        hh = num / jnp.maximum(jnp.abs(den), jnp.exp(-m_t))

        gains = stack([jnp.broadcast_to(gh_ref[:, h * dv:(h + 1) * dv], (L, dv)) for h in heads])
        hs = hh * lax.rsqrt(jnp.mean(hh * hh, axis=1, keepdims=True) + EPS) * gains
        og = stack([o_ref[pl.ds(c0, L), h * dv:(h + 1) * dv] for h in heads]).astype(F32)
        gated = (hs / (1.0 + jnp.exp(-og))).astype(out_ref.dtype)
        for h in heads:
            out_ref[pl.ds(c0, L), h * dv:(h + 1) * dv] = rows_of(gated, h)

        key_log = per_row(b_last) - bt + i_col
        m_new = [jnp.maximum(b_last[h] + m_prev[h], jnp.max(rows_of(key_log, h), axis=0, keepdims=True))
                 for h in heads]
        m_new_col = per_row(m_new)
        kd = stack([kc[h].astype(F32) for h in heads]) * jnp.exp(key_log - m_new_col)
        for h in heads:
            keep = jnp.exp(b_last[h] + m_prev[h] - m_new[h])
            kd_h = rows_of(kd, h)
            c_ref[h] = keep * c_prev[h] + jnp.dot(kd_h.T.astype(BF16), vc[h], preferred_element_type=F32)
            n_ref[h] = keep * n_prev[h] + jnp.sum(kd_h, axis=0, keepdims=True)
        m_ref[...] = m_new_col
        return carry

    lax.fori_loop(0, nchunk, step, 0)


def mlstm(qkvo, gcol, grow, g_head, *, batch, seq, heads, dqk, dv, hg):
    t = batch * seq
    nh = heads
    L = ML_CHUNK
    nchunk = seq // L
    idx = jnp.arange(L)
    tril = (idx[None, :] <= idx[:, None]).astype(BF16)
    triu = (idx[:, None] <= idx[None, :]).astype(BF16)
    assert nh % hg == 0 and seq % (ML_SEGMENTS * L) == 0
    ng = nh // hg
    ns = ML_SEGMENTS
    sl = seq // ns
    body = functools.partial(_mlstm_body, scale=dqk ** -0.5, nchunk=sl // L, hg=hg, dqk=dqk, dv=dv)
    v_blk0 = 2 * nh * dqk // (hg * dv)
    o_blk0 = v_blk0 + ng
    const = lambda b, g, s: (0, 0)
    return pl.pallas_call(
        body,
        grid=(batch, ng, ns),
        in_specs=[
            pl.BlockSpec((sl, hg * dqk), lambda b, g, s: (b * ns + s, g)),
            pl.BlockSpec((sl, hg * dqk), lambda b, g, s: (b * ns + s, ng + g)),
            pl.BlockSpec((sl, hg * dv), lambda b, g, s: (b * ns + s, v_blk0 + g)),
            pl.BlockSpec((sl, hg * dv), lambda b, g, s: (b * ns + s, o_blk0 + g)),
            pl.BlockSpec((sl, LANES), lambda b, g, s: (b * ns + s, g)),
            pl.BlockSpec((2 * hg, sl), lambda b, g, s: (g, b * ns + s)),
            pl.BlockSpec((1, hg * dv), lambda b, g, s: (0, g)),
            pl.BlockSpec((L, L), const),
            pl.BlockSpec((L, L), const),
        ],
        out_specs=pl.BlockSpec((sl, hg * dv), lambda b, g, s: (b * ns + s, g)),
        out_shape=jax.ShapeDtypeStruct((t, nh * dv), BF16),
        scratch_shapes=[pltpu.VMEM((hg, dqk, dv), F32), pltpu.VMEM((hg, 1, dqk), F32), pltpu.VMEM((hg * L, 1), F32)],
        compiler_params=_params("parallel", "parallel", "arbitrary"),
        name="mlstm",
    )(qkvo, qkvo, qkvo, qkvo, gcol, grow, g_head.reshape(1, nh * dv), tril, triu)


def _moe(o, w_out, x, norm_ffn, w_group, b_group, w_expert, b_expert, w1, w3, w2, layer, g_next, *, final,
         gate_params=None, tm=TILE_EXPERT):
    n_tok = x.shape[0]
    x1, ri, rw, cnt = proj_route(o, w_out, x, norm_ffn, w_group, b_group, w_expert, b_expert)
    pos, tables, nt_max = _route_tables(ri, cnt, tm=tm, n_tok=n_tok)
    xs = dispatch_rows(x1, norm_ffn, pos, spare=tm)
    ys = routed_experts(xs, tables, w1, w3, w2, layer, tm=tm, nt_max=nt_max, n_tok=n_tok)
    return moe_combine(x1, ys, pos, rw, g_next, final=final, gate_params=gate_params)


def kernel(x, norm_mix, norm_ffn, norm_final, sb_w_in, sb_w_out, ml_w_in, ml_b_i, ml_b_f, ml_g_head, ml_w_out,
           moe_w_group, moe_b_group, moe_w_expert, moe_b_expert, moe_w1, moe_w3, moe_w2):
    batch, seq, d = x.shape
    xt = x.reshape(batch * seq, d)

    h = rmsnorm_bf16(xt, norm_mix[0])
    qkv = matmul_bf16(h, sb_w_in, 3 * d)
    o = sb_attention(qkv, batch=batch, seq=seq, heads=SB_HEADS, dh=d // SB_HEADS)
    dqk = (d // 2) // ML_HEADS
    dv = d // ML_HEADS
    n_main = 2 * ML_HEADS * dqk + 2 * ML_HEADS * dv
    w_in_t = jnp.swapaxes(ml_w_in, 1, 2)
    gate_params = _gate_weights(w_in_t[0, n_main:], ml_b_i[0], ml_b_f[0], ML_HEAD_GROUP)
    xt, h, gcol, grow = _moe(o, sb_w_out, xt, norm_ffn[0], moe_w_group[0], moe_b_group[0], moe_w_expert[0],
                             moe_b_expert[0], moe_w1, moe_w3, moe_w2, 0, norm_mix[1], final=False,
                             gate_params=gate_params)

    qkvo = matmul_bf16(h, w_in_t, n_main, w_rows_are_outputs=True)
    o = mlstm(qkvo, gcol, grow, ml_g_head[0], batch=batch, seq=seq, heads=ML_HEADS, dqk=dqk, dv=dv,
              hg=ML_HEAD_GROUP)
    out = _moe(o, ml_w_out, xt, norm_ffn[1], moe_w_group[1], moe_b_group[1], moe_w_expert[1],
               moe_b_expert[1], moe_w1, moe_w3, moe_w2, 1, norm_final, final=True)
    return out.reshape(batch, seq, d)
```

```python
import functools

import jax
import jax.numpy as jnp
from jax import lax
from jax.experimental import pallas as pl
from jax.experimental.pallas import tpu as pltpu

F32 = jnp.float32
BF16 = jnp.bfloat16
EPS = 1e-6

V7X_VMEM_BYTES = 64 * 1024 * 1024
VMEM_LIMIT_BYTES = V7X_VMEM_BYTES - 8 * 1024 * 1024
LANES = 128
SUBLANES = 8

SB_HEADS = 16
ML_HEADS = 8
N_GROUPS = 4
EXPERTS_PER_GROUP = 8
N_EXPERTS = N_GROUPS * EXPERTS_PER_GROUP
TOP_K = 2
ML_CHUNK = 128
ML_HEAD_GROUP = 8
ML_SEGMENTS = 2
SB_BLOCK = 128
ROUTER_ROWS = 128
LOG2_E = 1.4426950408889634
EXP2_F32_UNDERFLOW = -150.5
DMA_PRIORITIES = 2
SLAB_SPARE_ROWS = 1
COMBINE_SLOTS = 3

TILE_NORM_PROJ = 1024
TILE_PROJ_IN = (2048, 1024)
TILE_PROJ_ROUTE = 512
TILE_DISPATCH = 512
TILE_EXPERT = 256
TILE_COMBINE = 256
SB_HEAD_GROUP = 8
SB_KEY_BLOCKS = 2

NT_DIMS = (((1,), (1,)), ((), ()))


def _params(*sem):
    return pltpu.CompilerParams(dimension_semantics=sem, vmem_limit_bytes=VMEM_LIMIT_BYTES)


def _split2(x):
    hi = x.astype(BF16)
    lo = (x - hi.astype(F32)).astype(BF16)
    return hi, lo


def _split3(x):
    hi = x.astype(BF16)
    r = x - hi.astype(F32)
    mid = r.astype(BF16)
    lo = (r - mid.astype(F32)).astype(BF16)
    return hi, mid, lo


def _rms(x, g):
    return x * lax.rsqrt(jnp.mean(x * x, axis=-1, keepdims=True) + EPS) * g


def _slab_rows(d):
    return d // LANES


def _slab_pitch(d):
    return _slab_rows(d) + SLAB_SPARE_ROWS


def _slab_store(ref, x, n):
    d = x.shape[1]
    pitch = _slab_pitch(d)
    for c in range(_slab_rows(d)):
        ref[pl.ds(c, n, stride=pitch), :] = x[:, c * LANES:(c + 1) * LANES]
    for c in range(_slab_rows(d), pitch):
        ref[pl.ds(c, n, stride=pitch), :] = jnp.zeros((n, LANES), x.dtype)


def _slab_load(ref, lead, n, d):
    pitch = _slab_pitch(d)
    return jnp.concatenate([ref[lead + (pl.ds(c, n, stride=pitch), slice(None))] for c in range(_slab_rows(d))],
                           axis=1)


def _mm_body(h_ref, w_ref, o_ref, wb_ref, *, w_rows_are_outputs):
    @pl.when(pl.program_id(1) == 0)
    def _():
        wb_ref[...] = w_ref[...].astype(BF16)

    if w_rows_are_outputs:
        acc = lax.dot_general(h_ref[...], wb_ref[...], NT_DIMS, preferred_element_type=F32)
    else:
        acc = jnp.dot(h_ref[...], wb_ref[...], preferred_element_type=F32)
    o_ref[...] = acc.astype(o_ref.dtype)


def _norm_mm_body(x_ref, g_ref, w_ref, o_ref, wb_ref):
    @pl.when(pl.program_id(1) == 0)
    def _():
        wb_ref[...] = w_ref[...].astype(BF16)

    x = x_ref[...]
    r = lax.rsqrt(jnp.mean(x * x, axis=-1, keepdims=True) + EPS)
    acc = jnp.dot((x * g_ref[...]).astype(BF16), wb_ref[...], preferred_element_type=F32)
    o_ref[...] = (acc * r).astype(o_ref.dtype)


def norm_matmul_bf16(x, g, w, n, *, tm=TILE_NORM_PROJ, tn=TILE_PROJ_IN[1]):
    t, k = x.shape
    return pl.pallas_call(
        _norm_mm_body,
        grid=(n // tn, t // tm),
        in_specs=[pl.BlockSpec((tm, k), lambda j, i: (i, 0)), pl.BlockSpec((1, k), lambda j, i: (0, 0)),
                  pl.BlockSpec((None, k, tn), lambda j, i: (0, 0, j))],
        out_specs=pl.BlockSpec((tm, tn), lambda j, i: (i, j)),
        out_shape=jax.ShapeDtypeStruct((t, n), BF16),
        scratch_shapes=[pltpu.VMEM((k, tn), BF16)],
        compiler_params=_params("parallel", "arbitrary"),
        name="norm_proj_in",
    )(x, g.reshape(1, k), w)


def matmul_bf16(h, w, n, *, w_rows_are_outputs=False, tm=TILE_PROJ_IN[0], tn=TILE_PROJ_IN[1]):
    t, k = h.shape
    if w_rows_are_outputs:
        w_spec, w_block = pl.BlockSpec((None, tn, k), lambda j, i: (0, j, 0)), (tn, k)
    else:
        w_spec, w_block = pl.BlockSpec((None, k, tn), lambda j, i: (0, 0, j)), (k, tn)
    return pl.pallas_call(
        functools.partial(_mm_body, w_rows_are_outputs=w_rows_are_outputs),
        grid=(n // tn, t // tm),
        in_specs=[pl.BlockSpec((tm, k), lambda j, i: (i, 0)), w_spec],
        out_specs=pl.BlockSpec((tm, tn), lambda j, i: (i, j)),
        out_shape=jax.ShapeDtypeStruct((t, n), BF16),
        scratch_shapes=[pltpu.VMEM(w_block, BF16)],
        compiler_params=_params("parallel", "arbitrary"),
        name="proj_in",
    )(h, w)


def _sb_attn_body(q_ref, k_ref, v_ref, cm_ref, o_ref, *, scale, nblk, hg, kc):
    blk = SB_BLOCK
    dh = SB_BLOCK
    wid = kc * blk
    rel = lax.broadcasted_iota(jnp.int32, (blk, wid), 1) - lax.broadcasted_iota(jnp.int32, (blk, wid), 0)

    def chunk(qbs, q0, c, rs, accs, masked):
        k0 = pl.multiple_of(c * wid, wid)
        if masked:
            valid = rel < (q0 - k0)
        z = jnp.concatenate(
            [lax.dot_general(qbs[h], k_ref[pl.ds(k0, wid), h * dh:(h + 1) * dh], NT_DIMS,
                             preferred_element_type=F32) for h in range(hg)], axis=1)
        sp = jnp.maximum(z, 0.0) + jnp.log(1.0 + jnp.exp2(-jnp.abs(z))) * LOG2_E
        log_beta = z - sp
        if masked:
            valid_all = jnp.concatenate([valid] * hg, axis=1)
            sp = jnp.where(valid_all, sp, 0.0)
        nb = hg * kc
        x = jnp.concatenate([sp[:, j * blk:(j + 1) * blk] for j in range(nb)], axis=0).astype(BF16)
        cs = jnp.dot(x, cm_ref[...], preferred_element_type=F32)
        new_rs = []
        suffix = [None] * nb
        for h in range(hg):
            r = rs[h]
            for j in reversed(range(kc)):
                csj = cs[(h * kc + j) * blk:(h * kc + j + 1) * blk]
                suffix[h * kc + j] = csj[:, :blk] + r
                r = r + csj[:, blk:]
            new_rs.append(r)
        a = jnp.exp2(log_beta + jnp.concatenate(suffix, axis=1))
        if masked:
            a = jnp.where(valid_all, a, 0.0)
        a = a.astype(BF16)
        new_accs = [accs[h] + jnp.dot(a[:, h * wid:(h + 1) * wid], v_ref[pl.ds(k0, wid), h * dh:(h + 1) * dh],
                                      preferred_element_type=F32) for h in range(hg)]
        return tuple(new_rs), tuple(new_accs)

    def q_step(qi, carry):
        q0 = pl.multiple_of(qi * blk, blk)
        qbs = [(q_ref[pl.ds(q0, blk), h * dh:(h + 1) * dh].astype(F32) * scale).astype(BF16) for h in range(hg)]
        zeros = tuple(jnp.zeros((blk, blk), F32) for _ in range(hg))
        top = qi // kc
        rs, accs = chunk(qbs, q0, top, zeros, zeros, True)

        def live(rs):
            return jnp.max(functools.reduce(jnp.maximum, rs)) > EXP2_F32_UNDERFLOW

        def k_cond(c):
            return (c[0] < top) & c[1]

        def k_step(c):
            rs, accs = chunk(qbs, q0, top - 1 - c[0], c[2], c[3], False)
            return c[0] + 1, live(rs), rs, accs

        accs = lax.while_loop(k_cond, k_step, (jnp.int32(0), live(rs), rs, accs))[3]
        for h in range(hg):
            o_ref[pl.ds(q0, blk), h * dh:(h + 1) * dh] = accs[h].astype(o_ref.dtype)
        return carry

    lax.fori_loop(0, nblk, q_step, 0)


def _cumsum_matrix(blk):
    j = jnp.arange(blk)[:, None]
    c = jnp.arange(2 * blk)[None, :]
    return jnp.where((c >= blk) | (j > c), -1.0, 0.0).astype(BF16)


def sb_attention(qkv, *, batch, seq, heads, dh, hg=SB_HEAD_GROUP, kc=SB_KEY_BLOCKS):
    assert dh == SB_BLOCK
    nblk = seq // SB_BLOCK
    assert nblk % kc == 0 and heads % hg == 0
    body = functools.partial(_sb_attn_body, scale=dh ** -0.5 * LOG2_E, nblk=nblk, hg=hg, kc=kc)
    cm = _cumsum_matrix(SB_BLOCK)
    ng = heads // hg
    return pl.pallas_call(
        body,
        grid=(batch, ng),
        in_specs=[
            pl.BlockSpec((seq, hg * dh), lambda b, h: (b, h)),
            pl.BlockSpec((seq, hg * dh), lambda b, h: (b, ng + h)),
            pl.BlockSpec((seq, hg * dh), lambda b, h: (b, 2 * ng + h)),
            pl.BlockSpec((SB_BLOCK, 2 * SB_BLOCK), lambda b, h: (0, 0)),
        ],
        out_specs=pl.BlockSpec((seq, hg * dh), lambda b, h: (b, h)),
        out_shape=jax.ShapeDtypeStruct((batch * seq, heads * dh), BF16),
        compiler_params=_params("parallel", "parallel"),
        name="sb_attn",
    )(qkv, qkv, qkv, cm)


def _proj_route_body(o_ref, w_ref, x_ref, g_ref, wrh_ref, wrl_ref, br_ref, tri_ref,
                     x1_ref, ri_ref, rw_ref, cnt_ref, base_ref, xprev_ref, wb_ref, *, tm):
    i = pl.program_id(0)
    n = pl.num_programs(0) - 1

    @pl.when(i == 0)
    def _():
        base_ref[...] = jnp.zeros_like(base_ref)
        wb_ref[...] = w_ref[...].astype(BF16)

    @pl.when(i > 0)
    def _():
        _route_tile(xprev_ref[...], g_ref, wrh_ref, wrl_ref, br_ref, tri_ref, ri_ref, rw_ref, cnt_ref, base_ref, tm=tm)

    @pl.when(i < n)
    def _():
        x1 = x_ref[...] + jnp.dot(o_ref[...], wb_ref[...], preferred_element_type=F32)
        x1_ref[...] = x1
        xprev_ref[...] = x1


def _route_tile(x1, g_ref, wrh_ref, wrl_ref, br_ref, tri_ref, ri_ref, rw_ref, cnt_ref, base_ref, *, tm):
    h2 = _rms(x1, g_ref[...])

    hh, hl = _split2(h2)
    wrh = wrh_ref[...]
    lg = (lax.dot_general(wrh, hh, NT_DIMS, preferred_element_type=F32)
          + lax.dot_general(wrh, hl, NT_DIMS, preferred_element_type=F32)
          + lax.dot_general(wrl_ref[...], hh, NT_DIMS, preferred_element_type=F32)) + br_ref[...]

    r8 = lax.broadcasted_iota(jnp.int32, (SUBLANES, tm), 0)
    neg_inf = jnp.float32(-jnp.inf)
    gl = jnp.where(r8 < N_GROUPS, lg[N_EXPERTS:N_EXPERTS + SUBLANES, :], neg_inf)
    gmax = jnp.max(gl, axis=0, keepdims=True)
    gsel = jnp.min(jnp.where(gl == gmax, r8, SUBLANES), axis=0, keepdims=True)
    p_sel = 1.0 / jnp.sum(jnp.exp(gl - gmax), axis=0, keepdims=True)

    es = lg[0:EXPERTS_PER_GROUP, :]
    for g in range(1, N_GROUPS):
        es = jnp.where(gsel == g, lg[g * EXPERTS_PER_GROUP:(g + 1) * EXPERTS_PER_GROUP, :], es)
    m1 = jnp.max(es, axis=0, keepdims=True)
    i1 = jnp.min(jnp.where(es == m1, r8, SUBLANES), axis=0, keepdims=True)
    es2 = jnp.where(r8 == i1, neg_inf, es)
    m2 = jnp.max(es2, axis=0, keepdims=True)
    i2 = jnp.min(jnp.where(es2 == m2, r8, SUBLANES), axis=0, keepdims=True)
    t = jnp.exp(m2 - m1)
    w1 = p_sel / (1.0 + t)
    w2 = w1 * t
    e1 = gsel * EXPERTS_PER_GROUP + i1
    e2 = gsel * EXPERTS_PER_GROUP + i2

    r32 = lax.broadcasted_iota(jnp.int32, (N_EXPERTS, tm), 0)
    hit1 = r32 == e1
    hit2 = r32 == e2
    onehot = (jnp.where(hit1, 1.0, 0.0) + jnp.where(hit2, 1.0, 0.0)).astype(BF16)
    cs = jnp.dot(onehot, tri_ref[...], preferred_element_type=F32)
    base = base_ref[...]
    before = cs[:, :tm] + base
    rank1 = jnp.sum(jnp.where(hit1, before, 0.0), axis=0, keepdims=True)
    rank2 = jnp.sum(jnp.where(hit2, before, 0.0), axis=0, keepdims=True)
    tot = cs[:, tm:]
    new_base = base + jnp.concatenate([tot] * (tm // LANES), axis=1)
    base_ref[...] = new_base
    cnt_ref[...] = new_base[:, :LANES]

    zi = jnp.zeros((SUBLANES, tm), jnp.int32)
    ri = jnp.where(r8 == 0, e1, zi)
    ri = jnp.where(r8 == 1, e2, ri)
    ri = jnp.where(r8 == 2, rank1.astype(jnp.int32), ri)
    ri = jnp.where(r8 == 3, rank2.astype(jnp.int32), ri)
    ri_ref[...] = ri
    zf = jnp.zeros((SUBLANES, tm), F32)
    rw = jnp.where(r8 == 0, w1, zf)
    rw = jnp.where(r8 == 1, w2, rw)
    rw_ref[...] = jnp.concatenate([rw, jnp.zeros((LANES - SUBLANES, tm), F32)], axis=0).T


def proj_route(o, w_out, x, g, w_group, b_group, w_expert, b_expert, *, tm=TILE_PROJ_ROUTE):
    t, d = x.shape
    k = o.shape[1]
    wr = jnp.zeros((ROUTER_ROWS, d), F32)
    wr = wr.at[:N_EXPERTS].set(w_expert.T).at[N_EXPERTS:N_EXPERTS + N_GROUPS].set(w_group.T)
    wrh, wrl = _split2(wr)
    br = jnp.zeros((ROUTER_ROWS,), F32).at[:N_EXPERTS].set(b_expert).at[N_EXPERTS:N_EXPERTS + N_GROUPS].set(b_group)
    br = jnp.broadcast_to(br[:, None], (ROUTER_ROWS, tm))
    tp = jnp.arange(tm)[:, None]
    tri = jnp.concatenate([(tp < jnp.arange(tm)[None, :]).astype(BF16), jnp.ones((tm, LANES), BF16)], axis=1)
    body = functools.partial(_proj_route_body, tm=tm)
    n = t // tm
    row = lambda i: (jnp.minimum(i, n - 1), 0)
    routed = lambda i: (0, jnp.maximum(i - 1, 0))
    const = lambda i: (0, 0)
    return pl.pallas_call(
        body,
        grid=(n + 1,),
        in_specs=[
            pl.BlockSpec((tm, k), row),
            pl.BlockSpec((None, k, d), lambda i: (0, 0, 0), pipeline_mode=pl.Buffered(1)),
            pl.BlockSpec((tm, d), row),
            pl.BlockSpec((1, d), const),
            pl.BlockSpec((ROUTER_ROWS, d), const),
            pl.BlockSpec((ROUTER_ROWS, d), const),
            pl.BlockSpec((ROUTER_ROWS, tm), const),
            pl.BlockSpec((tm, tm + LANES), const),
        ],
        out_specs=[
            pl.BlockSpec((tm, d), row),
            pl.BlockSpec((SUBLANES, tm), routed),
            pl.BlockSpec((tm, LANES), lambda i: (jnp.maximum(i - 1, 0), 0)),
            pl.BlockSpec((N_EXPERTS, LANES), const),
        ],
        out_shape=[
            jax.ShapeDtypeStruct((t, d), F32),
            jax.ShapeDtypeStruct((SUBLANES, t), jnp.int32),
            jax.ShapeDtypeStruct((t, LANES), F32),
            jax.ShapeDtypeStruct((N_EXPERTS, LANES), F32),
        ],
        scratch_shapes=[pltpu.VMEM((N_EXPERTS, tm), F32), pltpu.VMEM((tm, d), F32), pltpu.VMEM((k, d), BF16)],
        compiler_params=_params("arbitrary"),
        name="proj_route",
    )(o, w_out, x, g.reshape(1, d), wrh, wrl, br, tri)


def _dispatch_body(pos_ref, x_ref, g_ref, xs_hbm, hs, sem, *, tm, n_pairs, spare):
    i = pl.program_id(0)
    n = pl.num_programs(0)
    slot = lax.rem(i, 2)
    d = x_ref.shape[1]
    pitch = _slab_pitch(d)

    def row_out(p, r, s):
        return pltpu.make_async_copy(hs.at[s, pl.ds(r * pitch, pitch)], xs_hbm.at[pl.ds(p * pitch, pitch)], sem.at[s])

    def wait_tile(s):
        for k in range(TOP_K):
            for r in range(tm):
                row_out(0, r, s).wait()

    def tail(s):
        return pltpu.make_async_copy(hs.at[s, pl.ds(0, spare * pitch)],
                                     xs_hbm.at[pl.ds(n_pairs * pitch, spare * pitch)], sem.at[s])

    @pl.when(i >= 2)
    def _():
        wait_tile(slot)

    _slab_store(hs.at[slot], _rms(x_ref[...], g_ref[...]), tm)
    for k in range(TOP_K):
        for r in range(tm):
            row_out(pos_ref[0, 0, k * tm + r], r, slot).start(priority=r % DMA_PRIORITIES)

    @pl.when(i == n - 1)
    def _():
        wait_tile(slot)

        @pl.when(i >= 1)
        def _():
            wait_tile(1 - slot)

        hs[slot] = jnp.zeros(hs.shape[1:], hs.dtype)
        tail(slot).start()
        tail(slot).wait()


def dispatch_rows(x1, g, pos, *, spare, tm=TILE_DISPATCH):
    t, d = x1.shape
    nt = t // tm
    pitch = _slab_pitch(d)
    n_pairs = TOP_K * t
    assert spare <= tm
    pos3 = pos.reshape(TOP_K, nt, tm).transpose(1, 0, 2).reshape(nt, 1, TOP_K * tm)
    return pl.pallas_call(
        functools.partial(_dispatch_body, tm=tm, n_pairs=n_pairs, spare=spare),
        grid=(nt,),
        in_specs=[
            pl.BlockSpec((1, 1, TOP_K * tm), lambda i: (i, 0, 0), memory_space=pltpu.SMEM),
            pl.BlockSpec((tm, d), lambda i: (i, 0)),
            pl.BlockSpec((1, d), lambda i: (0, 0)),
        ],
        out_specs=pl.BlockSpec(memory_space=pl.ANY),
        out_shape=jax.ShapeDtypeStruct(((n_pairs + spare) * pitch, LANES), F32),
        scratch_shapes=[pltpu.VMEM((2, tm * pitch, LANES), F32), pltpu.SemaphoreType.DMA((2,))],
        compiler_params=_params("arbitrary"),
        name="dispatch",
    )(pos3, x1, g.reshape(1, d))


def _experts_body(te_ref, ts_ref, nt_ref, run_ref, nxt_ref, xs_hbm, w1_hbm, w3_hbm, w2_hbm, ys_hbm,
                  xbuf, ybuf, wf1, wf3, wf2, w1b, w3b, w2b, isem, osem, wsem, *, tm, d, n_pairs, layer):
    i = pl.program_id(0)
    n_tiles = nt_ref[0]
    slot = lax.rem(i, 2)
    pitch = _slab_pitch(d)

    def tile_in(j, s):
        return pltpu.make_async_copy(xs_hbm.at[pl.ds(ts_ref[j] * pitch, tm * pitch)], xbuf.at[s], isem.at[s])

    def tile_out(j, s):
        return pltpu.make_async_copy(ybuf.at[s], ys_hbm.at[pl.ds(ts_ref[j] * pitch, tm * pitch)], osem.at[s])

    def weights_in(e, s):
        return [pltpu.make_async_copy(w1_hbm.at[layer, e], wf1.at[s], wsem.at[s]),
                pltpu.make_async_copy(w3_hbm.at[layer, e], wf3.at[s], wsem.at[s]),
                pltpu.make_async_copy(w2_hbm.at[layer, e], wf2.at[s], wsem.at[s])]

    @pl.when(i == 0)
    def _():
        for c in weights_in(te_ref[0], 0):
            c.start()
        tile_in(0, 0).start()
        ybuf[1] = jnp.zeros(ybuf.shape[1:], ybuf.dtype)
        zero_tail = pltpu.make_async_copy(ybuf.at[1], ys_hbm.at[pl.ds(n_pairs * pitch, tm * pitch)], osem.at[1])
        zero_tail.start()
        zero_tail.wait()

    @pl.when(i < n_tiles)
    def _():
        tile_in(i, slot).wait()

        @pl.when(i + 1 < n_tiles)
        def _():
            tile_in(i + 1, 1 - slot).start(priority=DMA_PRIORITIES - 1)

        @pl.when((i == 0) | (te_ref[i] != te_ref[jnp.maximum(i - 1, 0)]))
        def _():
            ws = lax.rem(run_ref[i], 2)
            for c in weights_in(te_ref[i], ws):
                c.wait()
            w1b[...] = wf1[ws].astype(BF16)
            w3b[...] = wf3[ws].astype(BF16)
            w2b[...] = wf2[ws].astype(BF16)

            @pl.when(nxt_ref[i] >= 0)
            def _():
                for c in weights_in(nxt_ref[i], 1 - ws):
                    c.start()

        xb = _slab_load(xbuf, (slot,), tm, d).astype(BF16)
        a = jnp.dot(xb, w1b[...], preferred_element_type=F32)
        b = jnp.dot(xb, w3b[...], preferred_element_type=F32)
        hid = (a * (1.0 / (1.0 + jnp.exp(-a))) * b).astype(BF16)
        _slab_store(ybuf.at[slot], jnp.dot(hid, w2b[...], preferred_element_type=F32), tm)

        @pl.when(i >= 1)
        def _():
            tile_out(i - 1, 1 - slot).wait()

        tile_out(i, slot).start()

        @pl.when(i == n_tiles - 1)
        def _():
            tile_out(i, slot).wait()


def _route_tables(ri, cnt, *, tm, n_tok):
    nt_max = TOP_K * n_tok // tm + N_EXPERTS - 1
    counts = cnt[:, 0].astype(jnp.int32)
    ends = jnp.cumsum(counts)
    offs = ends - counts
    eid = jnp.arange(N_EXPERTS, dtype=jnp.int32)
    sel = ri[0:TOP_K, :, None] == eid
    pos = jnp.sum(jnp.where(sel, offs, 0), axis=-1) + ri[TOP_K:2 * TOP_K]
    tiles = (counts + tm - 1) // tm
    tile_ends = jnp.cumsum(tiles)
    n_tiles = tile_ends[-1].astype(jnp.int32)
    step = jnp.minimum(jnp.arange(nt_max, dtype=jnp.int32), n_tiles - 1)
    te = jnp.clip(jnp.sum((step[:, None] >= tile_ends[None, :]).astype(jnp.int32), axis=1), 0, N_EXPERTS - 1)
    first = jnp.sum(jnp.where(te[:, None] == eid, tile_ends - tiles, 0), axis=-1)
    ts = jnp.sum(jnp.where(te[:, None] == eid, offs, 0), axis=-1) + (step - first) * tm
    run = jnp.sum(((tiles > 0) & (eid < te[:, None])).astype(jnp.int32), axis=-1)
    after = jnp.sum(jnp.where(te[:, None] == eid, tile_ends, 0), axis=-1)
    te_after = jnp.clip(jnp.sum((after[:, None] >= tile_ends[None, :]).astype(jnp.int32), axis=1), 0, N_EXPERTS - 1)
    nxt = jnp.where(after < n_tiles, te_after, -1)
    tables = (te, ts.astype(jnp.int32), n_tiles.reshape(1), run.astype(jnp.int32), nxt.astype(jnp.int32))
    return pos, tables, nt_max


def routed_experts(xs, tables, w1, w3, w2, layer, *, tm, nt_max, n_tok):
    d, f = w1.shape[-2:]
    pitch = _slab_pitch(d)
    body = functools.partial(_experts_body, tm=tm, d=d, n_pairs=TOP_K * n_tok, layer=layer)
    any_space = pl.BlockSpec(memory_space=pl.ANY)
    grid_spec = pltpu.PrefetchScalarGridSpec(
        num_scalar_prefetch=len(tables),
        grid=(nt_max,),
        in_specs=[any_space, any_space, any_space, any_space],
        out_specs=any_space,
        scratch_shapes=[
            pltpu.VMEM((2, tm * pitch, LANES), F32),
            pltpu.VMEM((2, tm * pitch, LANES), F32),
            pltpu.VMEM((2, d, f), F32),
            pltpu.VMEM((2, d, f), F32),
            pltpu.VMEM((2, f, d), F32),
            pltpu.VMEM((d, f), BF16),
            pltpu.VMEM((d, f), BF16),
            pltpu.VMEM((f, d), BF16),
            pltpu.SemaphoreType.DMA((2,)),
            pltpu.SemaphoreType.DMA((2,)),
            pltpu.SemaphoreType.DMA((2,)),
        ],
    )
    return pl.pallas_call(
        body,
        grid_spec=grid_spec,
        out_shape=jax.ShapeDtypeStruct(xs.shape, F32),
        compiler_params=_params("arbitrary"),
        name="experts",
    )(*tables, xs, w1, w3, w2)


def _combine_body(pos_ref, nx1_ref, nx2_ref, x_ref, w_ref, g_ref, ys_hbm, *rest, tm, final, gate_groups):
    x2_ref = gate_in = gate_out = None
    if final:
        h_ref, ybuf, sem = rest
    elif gate_groups is None:
        x2_ref, h_ref, ybuf, sem = rest
    else:
        gate_in, (x2_ref, h_ref), gate_out, (ybuf, sem) = rest[0:3], rest[3:5], rest[5:7], rest[7:9]
    i = pl.program_id(0)
    n = pl.num_programs(0)
    slot = lax.rem(i, COMBINE_SLOTS)
    d = x_ref.shape[1]
    rows, pitch = _slab_rows(d), _slab_pitch(d)

    def row_in(src, k, r, s):
        return pltpu.make_async_copy(ys_hbm.at[pl.ds(src * pitch, rows)], ybuf.at[s, k, pl.ds(r * pitch, rows)],
                                     sem.at[s])

    def gather_start(idx_ref, s):
        for k in range(TOP_K):
            for r in range(tm):
                row_in(idx_ref[0, 0, k * tm + r], k, r, s).start(priority=r % DMA_PRIORITIES)

    def gather_wait(s):
        for k in range(TOP_K):
            for r in range(tm):
                row_in(0, k, r, s).wait()

    @pl.when(i == 0)
    def _():
        gather_start(pos_ref, 0)

        @pl.when(n > 1)
        def _():
            gather_start(nx1_ref, 1)

    gather_wait(slot)

    @pl.when(i + 2 < n)
    def _():
        gather_start(nx2_ref, lax.rem(i + 2, COMBINE_SLOTS))

    w = w_ref[...]
    x2 = x_ref[...] + w[:, 0:1] * _slab_load(ybuf, (slot, 0), tm, d) + w[:, 1:2] * _slab_load(ybuf, (slot, 1), tm, d)
    if x2_ref is not None:
        x2_ref[...] = x2
    h = _rms(x2, g_ref[...]).astype(h_ref.dtype)
    h_ref[...] = h
    if gate_groups is not None:
        _gates(h, *gate_in, *gate_out, hg=gate_groups[0], ng=gate_groups[1])


def moe_combine(x1, ys, pos, rw, g, *, final, gate_params=None, tm=TILE_COMBINE):
    t, d = x1.shape
    nt = t // tm
    pos3 = pos.reshape(TOP_K, nt, tm).transpose(1, 0, 2).reshape(nt, 1, TOP_K * tm)
    row = lambda i: (i, 0)
    const = lambda i: (0, 0)
    smem_tile = lambda ahead: pl.BlockSpec((1, 1, TOP_K * tm), lambda i: (jnp.minimum(i + ahead, nt - 1), 0, 0),
                                           memory_space=pltpu.SMEM)
    in_specs = [
        smem_tile(0), smem_tile(1), smem_tile(2),
        pl.BlockSpec((tm, d), row),
        pl.BlockSpec((tm, LANES), row),
        pl.BlockSpec((1, d), const),
        pl.BlockSpec(memory_space=pl.ANY),
    ]
    scratch = [pltpu.VMEM((COMBINE_SLOTS, TOP_K, tm * _slab_pitch(d), LANES), F32),
               pltpu.SemaphoreType.DMA((COMBINE_SLOTS,))]
    args = [pos3, pos3, pos3, x1, rw, g.reshape(1, d), ys]
    if final:
        return pl.pallas_call(
            functools.partial(_combine_body, tm=tm, final=True, gate_groups=None), grid=(nt,), in_specs=in_specs,
            out_specs=pl.BlockSpec((tm, d), row),
            out_shape=jax.ShapeDtypeStruct((t, d), F32),
            scratch_shapes=scratch,
            compiler_params=_params("arbitrary"), name="combine_final",
        )(*args)
    out_specs = [pl.BlockSpec((tm, d), row), pl.BlockSpec((tm, d), row)]
    out_shape = [jax.ShapeDtypeStruct((t, d), F32), jax.ShapeDtypeStruct((t, d), BF16)]
    gate_groups = None
    if gate_params is not None:
        w_hi, w_lo, bias, hg, ng = gate_params
        gate_groups = (hg, ng)
        gw = ng * LANES
        in_specs += [pl.BlockSpec((gw, d), const), pl.BlockSpec((gw, d), const), pl.BlockSpec((1, gw), const)]
        args += [w_hi, w_lo, bias]
        out_specs += [pl.BlockSpec((tm, gw), row), pl.BlockSpec((ng * 2 * hg, tm), lambda i: (0, i))]
        out_shape += [jax.ShapeDtypeStruct((t, gw), F32), jax.ShapeDtypeStruct((ng * 2 * hg, t), F32)]
    return pl.pallas_call(
        functools.partial(_combine_body, tm=tm, final=False, gate_groups=gate_groups), grid=(nt,), in_specs=in_specs,
        out_specs=out_specs, out_shape=out_shape,
        scratch_shapes=scratch,
        compiler_params=_params("arbitrary"), name="combine",
    )(*args)


def _log_sigmoid(x):
    return jnp.minimum(x, 0.0) - jnp.log(1.0 + jnp.exp(-jnp.abs(x)))


def _gates(h, wch_ref, wcl_ref, bc_ref, gc_ref, gr_ref, *, hg, ng):
    col = (lax.dot_general(h, wch_ref[...], NT_DIMS, preferred_element_type=F32)
           + lax.dot_general(h, wcl_ref[...], NT_DIMS, preferred_element_type=F32)) + bc_ref[...]
    lane = lax.broadcasted_iota(jnp.int32, col.shape, 1) % LANES
    gates = jnp.where(lane < hg, col, _log_sigmoid(col))
    gc_ref[...] = gates
    gt = gates.T
    gr_ref[...] = jnp.concatenate([gt[g * LANES:g * LANES + 2 * hg] for g in range(ng)], axis=0)


def _gate_weights(w_if_t, b_i, b_f, hg):
    nh = b_i.shape[0]
    ng = nh // hg

    def grouped(x_i, x_f):
        pad = jnp.zeros((ng, LANES - 2 * hg) + x_i.shape[1:], F32)
        blocks = jnp.concatenate([x_i.reshape((ng, hg) + x_i.shape[1:]), x_f.reshape((ng, hg) + x_f.shape[1:]), pad],
                                 axis=1)
        return blocks.reshape((ng * LANES,) + x_i.shape[1:])

    w_hi, w_lo = _split2(grouped(w_if_t[:nh], w_if_t[nh:]))
    return w_hi, w_lo, grouped(b_i, b_f).reshape(1, ng * LANES), hg, ng


def _mlstm_body(q_ref, k_ref, v_ref, o_ref, gc_ref, gr_ref, gh_ref, tril_ref, triu_ref, out_ref,
                c_ref, n_ref, m_ref, *, scale, nchunk, hg, dqk, dv):
    L = ML_CHUNK
    row = lax.broadcasted_iota(jnp.int32, (L, L), 0)
    col = lax.broadcasted_iota(jnp.int32, (L, L), 1)
    heads = range(hg)
    stack = lambda parts: jnp.concatenate(parts, axis=0)
    rows_of = lambda x, h: x[h * L:(h + 1) * L]
    per_row = lambda scalars: stack([jnp.broadcast_to(s, (L, 1)) for s in scalars])
    tmask = stack([col <= row] * hg)

    @pl.when(pl.program_id(2) == 0)
    def _():
        c_ref[...] = jnp.zeros_like(c_ref)
        n_ref[...] = jnp.zeros_like(n_ref)
        m_ref[...] = jnp.zeros_like(m_ref)

    def cumsum_exact(parts, tri, left):
        acc = None
        for p in parts:
            d = (jnp.dot(tri, p, preferred_element_type=F32) if left
                 else jnp.dot(p, tri, preferred_element_type=F32))
            acc = d if acc is None else acc + d
        return acc

    def step(c, carry):
        c0 = pl.multiple_of(c * L, L)
        qc = [(q_ref[pl.ds(c0, L), h * dqk:(h + 1) * dqk].astype(F32) * scale).astype(BF16) for h in heads]
        kc = [k_ref[pl.ds(c0, L), h * dqk:(h + 1) * dqk] for h in heads]
        vc = [v_ref[pl.ds(c0, L), h * dv:(h + 1) * dv] for h in heads]
        gc = gc_ref[pl.ds(c0, L), :]
        gr = gr_ref[:, pl.ds(c0, L)]
        i_col = stack([gc[:, h:h + 1] for h in heads])
        f_col = stack([gc[:, hg + h:hg + h + 1] for h in heads])
        m_col = m_ref[...]
        m_prev = [m_col[h * L:h * L + 1] for h in heads]

        f_parts = _split3(jnp.broadcast_to(f_col, (hg * L, L)))
        b_col = stack([cumsum_exact([rows_of(p, h) for p in f_parts], tril_ref[...], True) for h in heads])
        u_rows = []
        for h in heads:
            f_row = gr[hg + h:hg + h + 1, :]
            b_row = cumsum_exact(_split3(jnp.broadcast_to(f_row, (SUBLANES, L))), triu_ref[...], False)[0:1, :]
            u_rows.append(jnp.broadcast_to(gr[h:h + 1, :] - b_row, (L, L)))
        bt = b_col[:, 0:1]
        b_last = [b_col[h * L + L - 1:(h + 1) * L, 0:1] for h in heads]

        d_mat = jnp.where(tmask, b_col + stack(u_rows), -1e30)
        m_t = jnp.maximum(bt + m_col, jnp.max(d_mat, axis=1, keepdims=True))
        e = jnp.where(tmask, jnp.exp(d_mat - m_t), 0.0)
        w = stack([lax.dot_general(qc[h], kc[h], NT_DIMS, preferred_element_type=F32) for h in heads]) * e
        wb = w.astype(BF16)
        inter = jnp.exp(bt + m_col - m_t)
        c_prev = [c_ref[h] for h in heads]
        n_prev = [n_ref[h] for h in heads]
        num = inter * stack([jnp.dot(qc[h], c_prev[h].astype(BF16), preferred_element_type=F32) for h in heads]) \
            + stack([jnp.dot(rows_of(wb, h), vc[h], preferred_element_type=F32) for h in heads])
        qn = jnp.sum(stack([qc[h].astype(F32) * n_prev[h] for h in heads]), axis=1, keepdims=True)
        den = inter * qn + jnp.sum(w, axis=1, keepdims=True)
        hh = num / jnp.maximum(jnp.abs(den), jnp.exp(-m_t))

        gains = stack([jnp.broadcast_to(gh_ref[:, h * dv:(h + 1) * dv], (L, dv)) for h in heads])
        hs = hh * lax.rsqrt(jnp.mean(hh * hh, axis=1, keepdims=True) + EPS) * gains
        og = stack([o_ref[pl.ds(c0, L), h * dv:(h + 1) * dv] for h in heads]).astype(F32)
        gated = (hs / (1.0 + jnp.exp(-og))).astype(out_ref.dtype)
        for h in heads:
            out_ref[pl.ds(c0, L), h * dv:(h + 1) * dv] = rows_of(gated, h)

        key_log = per_row(b_last) - bt + i_col
        m_new = [jnp.maximum(b_last[h] + m_prev[h], jnp.max(rows_of(key_log, h), axis=0, keepdims=True))
                 for h in heads]
        m_new_col = per_row(m_new)
        kd = stack([kc[h].astype(F32) for h in heads]) * jnp.exp(key_log - m_new_col)
        for h in heads:
            keep = jnp.exp(b_last[h] + m_prev[h] - m_new[h])
            kd_h = rows_of(kd, h)
            c_ref[h] = keep * c_prev[h] + jnp.dot(kd_h.T.astype(BF16), vc[h], preferred_element_type=F32)
            n_ref[h] = keep * n_prev[h] + jnp.sum(kd_h, axis=0, keepdims=True)
        m_ref[...] = m_new_col
        return carry

    lax.fori_loop(0, nchunk, step, 0)


def mlstm(qkvo, gcol, grow, g_head, *, batch, seq, heads, dqk, dv, hg):
    t = batch * seq
    nh = heads
    L = ML_CHUNK
    nchunk = seq // L
    idx = jnp.arange(L)
    tril = (idx[None, :] <= idx[:, None]).astype(BF16)
    triu = (idx[:, None] <= idx[None, :]).astype(BF16)
    assert nh % hg == 0 and seq % (ML_SEGMENTS * L) == 0
    ng = nh // hg
    ns = ML_SEGMENTS
    sl = seq // ns
    body = functools.partial(_mlstm_body, scale=dqk ** -0.5, nchunk=sl // L, hg=hg, dqk=dqk, dv=dv)
    v_blk0 = 2 * nh * dqk // (hg * dv)
    o_blk0 = v_blk0 + ng
    const = lambda b, g, s: (0, 0)
    return pl.pallas_call(
        body,
        grid=(batch, ng, ns),
        in_specs=[
            pl.BlockSpec((sl, hg * dqk), lambda b, g, s: (b * ns + s, g)),
            pl.BlockSpec((sl, hg * dqk), lambda b, g, s: (b * ns + s, ng + g)),
            pl.BlockSpec((sl, hg * dv), lambda b, g, s: (b * ns + s, v_blk0 + g)),
            pl.BlockSpec((sl, hg * dv), lambda b, g, s: (b * ns + s, o_blk0 + g)),
            pl.BlockSpec((sl, LANES), lambda b, g, s: (b * ns + s, g)),
            pl.BlockSpec((2 * hg, sl), lambda b, g, s: (g, b * ns + s)),
            pl.BlockSpec((1, hg * dv), lambda b, g, s: (0, g)),
            pl.BlockSpec((L, L), const),
            pl.BlockSpec((L, L), const),
        ],
        out_specs=pl.BlockSpec((sl, hg * dv), lambda b, g, s: (b * ns + s, g)),
        out_shape=jax.ShapeDtypeStruct((t, nh * dv), BF16),
        scratch_shapes=[pltpu.VMEM((hg, dqk, dv), F32), pltpu.VMEM((hg, 1, dqk), F32), pltpu.VMEM((hg * L, 1), F32)],
        compiler_params=_params("parallel", "parallel", "arbitrary"),
        name="mlstm",
    )(qkvo, qkvo, qkvo, qkvo, gcol, grow, g_head.reshape(1, nh * dv), tril, triu)


def _moe(o, w_out, x, norm_ffn, w_group, b_group, w_expert, b_expert, w1, w3, w2, layer, g_next, *, final,
         gate_params=None, tm=TILE_EXPERT):
    n_tok = x.shape[0]
    x1, ri, rw, cnt = proj_route(o, w_out, x, norm_ffn, w_group, b_group, w_expert, b_expert)
    pos, tables, nt_max = _route_tables(ri, cnt, tm=tm, n_tok=n_tok)
    xs = dispatch_rows(x1, norm_ffn, pos, spare=tm)
    ys = routed_experts(xs, tables, w1, w3, w2, layer, tm=tm, nt_max=nt_max, n_tok=n_tok)
    return moe_combine(x1, ys, pos, rw, g_next, final=final, gate_params=gate_params)


def kernel(x, norm_mix, norm_ffn, norm_final, sb_w_in, sb_w_out, ml_w_in, ml_b_i, ml_b_f, ml_g_head, ml_w_out,
           moe_w_group, moe_b_group, moe_w_expert, moe_b_expert, moe_w1, moe_w3, moe_w2):
    batch, seq, d = x.shape
    xt = x.reshape(batch * seq, d)

    qkv = norm_matmul_bf16(xt, norm_mix[0], sb_w_in, 3 * d)
    o = sb_attention(qkv, batch=batch, seq=seq, heads=SB_HEADS, dh=d // SB_HEADS)
    dqk = (d // 2) // ML_HEADS
    dv = d // ML_HEADS
    n_main = 2 * ML_HEADS * dqk + 2 * ML_HEADS * dv
    w_in_t = jnp.swapaxes(ml_w_in, 1, 2)
    gate_params = _gate_weights(w_in_t[0, n_main:], ml_b_i[0], ml_b_f[0], ML_HEAD_GROUP)
    xt, h, gcol, grow = _moe(o, sb_w_out, xt, norm_ffn[0], moe_w_group[0], moe_b_group[0], moe_w_expert[0],
                             moe_b_expert[0], moe_w1, moe_w3, moe_w2, 0, norm_mix[1], final=False,
                             gate_params=gate_params)

    qkvo = matmul_bf16(h, w_in_t, n_main, w_rows_are_outputs=True)
    o = mlstm(qkvo, gcol, grow, ml_g_head[0], batch=batch, seq=seq, heads=ML_HEADS, dqk=dqk, dv=dv,
              hg=ML_HEAD_GROUP)
    out = _moe(o, ml_w_out, xt, norm_ffn[1], moe_w_group[1], moe_b_group[1], moe_w_expert[1],
               moe_b_expert[1], moe_w1, moe_w3, moe_w2, 1, norm_final, final=True)
    return out.reshape(batch, seq, d)
```
